```python
import math
import functools
import jax
import jax.numpy as jnp
from jax import lax
import numpy as np

D_MODEL = 1024
BATCH = 2
SEQ = 8192
DEPTH = 1
DEC_BATCH = 128
DEC_SEQ = 8
PAST_LEN = 8192
PAGE_SIZE = 128

GLA_HEADS = 4
GLA_DK = 64
GLA_DV = 128
GLA_LOWRANK = 16
GLA_TAU = 16.0
GLA_CHUNK = 64
FOX_HEADS = 8
FOX_DH = 64
FOX_BLOCK = 128
FOX_F_BIAS = 3.0
N_GROUPS = 4
EXPERTS_PER_GROUP = 8
N_EXPERTS = N_GROUPS * EXPERTS_PER_GROUP
TOP_K_INNER = 2
D_EXPERT = D_MODEL // 4
RMS_EPS = 1e-6

GLA_QK = GLA_HEADS * GLA_DK
GLA_VW = GLA_HEADS * GLA_DV
FOX_W = FOX_HEADS * FOX_DH
IN_SPLITS = (GLA_QK, GLA_QK, GLA_VW, GLA_VW, GLA_LOWRANK, FOX_W, FOX_W, FOX_W, FOX_HEADS, D_MODEL, D_MODEL)
IN_COLS = 2 * GLA_QK + 2 * GLA_VW + GLA_LOWRANK + 3 * FOX_W + FOX_HEADS + 2 * D_MODEL

kernel_name = 'hybrid_gla_fox_hmoe_step'


def rmsnorm(x, w):
    xf = x.astype(jnp.float32)
    y = xf * lax.rsqrt(jnp.mean(xf * xf, axis=-1, keepdims=True) + RMS_EPS)
    return (y * w.astype(jnp.float32)).astype(x.dtype)


def split_projection(p):
    bounds = [int(b) for b in np.cumsum(IN_SPLITS)[:-1]]
    return jnp.split(p, bounds, axis=-1)


def gla_chunked(q, k, v, log_a, S0):
    B, T, H, dk = q.shape
    dv = v.shape[-1]
    C = math.gcd(T, GLA_CHUNK)
    N = T // C

    def blk(a):
        return a.reshape(B, N, C, H, a.shape[-1]).transpose(1, 0, 3, 2, 4).astype(jnp.float32)

    qc, kc, vc, gc = blk(q), blk(k), blk(v), blk(log_a)
    b = jnp.cumsum(gc, axis=3)
    b_last = b[:, :, :, -1:, :]
    q_i = qc * jnp.exp(b) * (dk ** -0.5)
    k_i = kc * jnp.exp(-b)
    k_end = kc * jnp.exp(b_last - b)
    causal = jnp.tril(jnp.ones((C, C), dtype=bool))
    A = jnp.where(causal, jnp.einsum('nbhtd,nbhsd->nbhts', q_i, k_i), 0.0)
    o_intra = jnp.einsum('nbhts,nbhsv->nbhtv', A, vc)
    kv = jnp.einsum('nbhsd,nbhsv->nbhdv', k_end, vc)
    decay = jnp.exp(b_last[:, :, :, 0, :])

    def step(S, xs):
        q_n, kv_n, d_n = xs
        o = jnp.einsum('bhtd,bhdv->bhtv', q_n, S)
        return d_n[..., None] * S + kv_n, o

    S_fin, o_inter = lax.scan(step, S0.astype(jnp.float32), (q_i, kv, decay))
    o = (o_intra + o_inter).transpose(1, 0, 3, 2, 4).reshape(B, T, H, dv)
    return o.astype(v.dtype), S_fin.astype(S0.dtype)


def fox_prompt(q, k, v, log_f):
    B, S, H, dh = q.shape
    nb = S // FOX_BLOCK
    c = jnp.cumsum(log_f.astype(jnp.float32), axis=1)
    qb = (q.astype(jnp.float32) * dh ** -0.5).reshape(B, nb, FOX_BLOCK, H, dh).transpose(1, 0, 2, 3, 4)
    cb = c.reshape(B, nb, FOX_BLOCK, H).transpose(1, 0, 2, 3)
    kf = k.astype(jnp.float32)
    vf = v.astype(jnp.float32)
    c_k = c.transpose(0, 2, 1)
    key_pos = jnp.arange(S)

    def one_block(args):
        i, q_blk, c_blk = args
        s = jnp.einsum('bqhd,bkhd->bhqk', q_blk, kf)
        s = s + c_blk.transpose(0, 2, 1)[..., None] - c_k[:, :, None, :]
        q_pos = i * FOX_BLOCK + jnp.arange(FOX_BLOCK)
        s = jnp.where(key_pos[None, :] <= q_pos[:, None], s, -jnp.inf)
        p = jax.nn.softmax(s, axis=-1)
        return jnp.einsum('bhqk,bkhd->bqhd', p, vf)

    o = lax.map(one_block, (jnp.arange(nb), qb, cb))
    return o.transpose(1, 0, 2, 3, 4).reshape(B, S, H, dh).astype(v.dtype)


def fox_sample(layer, cache_k, cache_v, cache_logf, page_table, q, k, v, log_f):
    Bd, T, H, dh = q.shape
    n_pages = page_table.shape[1]
    lf_past = cache_logf[layer, page_table].astype(jnp.float32).reshape(Bd, n_pages * PAGE_SIZE, H)
    c_past = jnp.cumsum(lf_past, axis=1)
    bias_past = (c_past[:, -1:, :] - c_past).reshape(Bd, n_pages, PAGE_SIZE, H).transpose(1, 0, 3, 2)
    cn = jnp.cumsum(log_f.astype(jnp.float32), axis=1).transpose(0, 2, 1)
    qf = q.astype(jnp.float32) * dh ** -0.5

    def page_step(carry, xs):
        m, l, acc = carry
        pages, bp = xs
        kp = cache_k[layer, pages].astype(jnp.float32)
        vp = cache_v[layer, pages].astype(jnp.float32)
        s = jnp.einsum('bthd,bshd->bhts', qf, kp) + cn[..., None] + bp[:, :, None, :]
        m_new = jnp.maximum(m, jnp.max(s, axis=-1))
        a = jnp.exp(m - m_new)
        p = jnp.exp(s - m_new[..., None])
        acc = acc * a[..., None] + jnp.einsum('bhts,bshd->bhtd', p, vp)
        return (m_new, l * a + jnp.sum(p, axis=-1), acc), None

    init = (jnp.full((Bd, H, T), -1e30, jnp.float32),
            jnp.zeros((Bd, H, T), jnp.float32),
            jnp.zeros((Bd, H, T, dh), jnp.float32))
    (m, l, acc), _ = lax.scan(page_step, init, (page_table.T, bias_past))
    s_new = jnp.einsum('bthd,bshd->bhts', qf, k.astype(jnp.float32)) + cn[..., :, None] - cn[..., None, :]
    causal = jnp.tril(jnp.ones((T, T), dtype=bool))
    s_new = jnp.where(causal, s_new, -jnp.inf)
    m_new = jnp.maximum(m, jnp.max(s_new, axis=-1))
    a = jnp.exp(m - m_new)
    p = jnp.exp(s_new - m_new[..., None])
    l = l * a + jnp.sum(p, axis=-1)
    acc = acc * a[..., None] + jnp.einsum('bhts,bshd->bhtd', p, v.astype(jnp.float32))
    o = (acc / l[..., None]).transpose(0, 2, 1, 3)
    return o.astype(v.dtype)


def hier_moe(x, w_rg, b_rg, w_re, b_re, w_g, w_u, w_d):
    xt = x.reshape(-1, D_MODEL)
    xf = xt.astype(jnp.float32)
    p_group = jax.nn.softmax(xf @ w_rg.astype(jnp.float32) + b_rg.astype(jnp.float32), axis=-1)
    g_sel = jnp.argmax(p_group, axis=-1)
    p_g = jnp.take_along_axis(p_group, g_sel[:, None], axis=-1)
    e_logits = (xf @ w_re.astype(jnp.float32) + b_re.astype(jnp.float32)).reshape(-1, N_GROUPS, EXPERTS_PER_GROUP)
    e_sel = jnp.take_along_axis(e_logits, g_sel[:, None, None], axis=1)[:, 0]
    w_top, i_top = lax.top_k(jax.nn.softmax(e_sel, axis=-1), TOP_K_INNER)
    w_top = w_top / jnp.sum(w_top, axis=-1, keepdims=True)
    expert_id = g_sel[:, None] * EXPERTS_PER_GROUP + i_top
    gates = jnp.sum(jax.nn.one_hot(expert_id, N_EXPERTS, dtype=jnp.float32) * (p_g * w_top)[..., None], axis=1)
    y = jnp.zeros(xf.shape, jnp.float32)
    for e in range(N_EXPERTS):
        h = jax.nn.silu(xt @ w_g[e]) * (xt @ w_u[e])
        y = y + gates[:, e:e + 1] * (h @ w_d[e]).astype(jnp.float32)
    return y.astype(x.dtype).reshape(x.shape)


def trunk_layer(x, S0, fox_fn, norm_mix_w, w_in, w_gla_a2, b_gla_a2, gla_norm_w, b_fox_f,
                w_up_gla, w_up_fox, w_out, norm_ffn_w, w_rg, b_rg, w_re, b_re, w_g, w_u, w_d):
    B, T, _ = x.shape
    xn = rmsnorm(x, norm_mix_w)
    (q_g, k_g, v_g, r_g, z_a, q_f, k_f, v_f, f_pre, gate_a, gate_b) = split_projection(xn @ w_in)

    def heads(a, h):
        return a.reshape(B, T, h, -1)

    log_a = jax.nn.log_sigmoid((z_a @ w_gla_a2 + b_gla_a2).astype(jnp.float32)) / GLA_TAU
    o_gla, S_new = gla_chunked(heads(q_g, GLA_HEADS), heads(k_g, GLA_HEADS), heads(v_g, GLA_HEADS),
                               heads(log_a, GLA_HEADS), S0)
    o_gla = rmsnorm(o_gla, gla_norm_w).reshape(B, T, GLA_VW) * jax.nn.silu(r_g)
    log_f = jax.nn.log_sigmoid((f_pre + b_fox_f).astype(jnp.float32))
    k_f = heads(k_f, FOX_HEADS)
    v_f = heads(v_f, FOX_HEADS)
    o_fox = fox_fn(heads(q_f, FOX_HEADS), k_f, v_f, log_f).reshape(B, T, FOX_W)
    merged = jax.nn.sigmoid(gate_a) * (o_gla @ w_up_gla) + jax.nn.sigmoid(gate_b) * (o_fox @ w_up_fox)
    h = x + merged @ w_out
    h = h + hier_moe(rmsnorm(h, norm_ffn_w), w_rg, b_rg, w_re, b_re, w_g, w_u, w_d)
    return h, S_new, k_f, v_f, log_f


def setup_inputs(seed: int = 0) -> dict:
    key = jax.random.key(seed)
    ks = jax.random.split(key, 25)
    n_pages = PAST_LEN // PAGE_SIZE
    n_phys = (DEC_BATCH * n_pages * 5) // 4
    f32 = jnp.float32

    def nrm(k, shape):
        return jax.random.normal(k, shape, f32)

    def gain(k, shape):
        return 1.0 + 0.02 * nrm(k, shape)

    x_prompt = nrm(ks[0], (BATCH, SEQ, D_MODEL))
    x_sample = nrm(ks[1], (DEC_BATCH, DEC_SEQ, D_MODEL))
    state_gla = nrm(ks[2], (DEPTH, DEC_BATCH, GLA_HEADS, GLA_DK, GLA_DV))
    cache_k = nrm(ks[3], (DEPTH, n_phys, PAGE_SIZE, FOX_HEADS, FOX_DH))
    cache_v = nrm(ks[4], (DEPTH, n_phys, PAGE_SIZE, FOX_HEADS, FOX_DH))
    cache_logf = jax.nn.log_sigmoid(FOX_F_BIAS + 0.5 * nrm(ks[5], (DEPTH, n_phys, PAGE_SIZE, FOX_HEADS)))
    page_table = jax.random.permutation(ks[6], n_phys)[: DEC_BATCH * n_pages].reshape(DEC_BATCH, n_pages).astype(jnp.int32)
    norm_mix_w = gain(ks[7], (DEPTH, D_MODEL))
    w_in = nrm(ks[8], (DEPTH, D_MODEL, IN_COLS)) * D_MODEL ** -0.5
    w_gla_a2 = nrm(ks[9], (DEPTH, GLA_LOWRANK, GLA_QK)) * GLA_LOWRANK ** -0.5
    b_gla_a2 = 0.1 * nrm(ks[10], (DEPTH, GLA_QK))
    gla_norm_w = gain(ks[11], (DEPTH, GLA_DV))
    b_fox_f = FOX_F_BIAS + 0.1 * nrm(ks[12], (DEPTH, FOX_HEADS))
    w_up_gla = nrm(ks[13], (DEPTH, GLA_VW, D_MODEL)) * GLA_VW ** -0.5
    w_up_fox = nrm(ks[14], (DEPTH, FOX_W, D_MODEL)) * FOX_W ** -0.5
    w_out = nrm(ks[15], (DEPTH, D_MODEL, D_MODEL)) * D_MODEL ** -0.5
    norm_ffn_w = gain(ks[16], (DEPTH, D_MODEL))
    w_router_group = nrm(ks[17], (DEPTH, D_MODEL, N_GROUPS)) * D_MODEL ** -0.5
    b_router_group = 0.01 * nrm(ks[18], (DEPTH, N_GROUPS))
    w_router_expert = nrm(ks[19], (DEPTH, D_MODEL, N_EXPERTS)) * D_MODEL ** -0.5
    b_router_expert = 0.01 * nrm(ks[20], (DEPTH, N_EXPERTS))
    w_exp_gate = nrm(ks[21], (DEPTH, N_EXPERTS, D_MODEL, D_EXPERT)) * D_MODEL ** -0.5
    w_exp_up = nrm(ks[22], (DEPTH, N_EXPERTS, D_MODEL, D_EXPERT)) * D_MODEL ** -0.5
    w_exp_down = nrm(ks[23], (DEPTH, N_EXPERTS, D_EXPERT, D_MODEL)) * D_EXPERT ** -0.5
    final_norm_w = gain(ks[24], (D_MODEL,))
    return {'x_prompt': x_prompt, 'x_sample': x_sample, 'state_gla': state_gla,
            'cache_k': cache_k, 'cache_v': cache_v, 'cache_logf': cache_logf, 'page_table': page_table,
            'norm_mix_w': norm_mix_w, 'w_in': w_in, 'w_gla_a2': w_gla_a2, 'b_gla_a2': b_gla_a2,
            'gla_norm_w': gla_norm_w, 'b_fox_f': b_fox_f, 'w_up_gla': w_up_gla, 'w_up_fox': w_up_fox,
            'w_out': w_out, 'norm_ffn_w': norm_ffn_w, 'w_router_group': w_router_group,
            'b_router_group': b_router_group, 'w_router_expert': w_router_expert,
            'b_router_expert': b_router_expert, 'w_exp_gate': w_exp_gate, 'w_exp_up': w_exp_up,
            'w_exp_down': w_exp_down, 'final_norm_w': final_norm_w}


def reference(x_prompt, x_sample, state_gla, cache_k, cache_v, cache_logf, page_table,
              norm_mix_w, w_in, w_gla_a2, b_gla_a2, gla_norm_w, b_fox_f, w_up_gla, w_up_fox,
              w_out, norm_ffn_w, w_router_group, b_router_group, w_router_expert, b_router_expert,
              w_exp_gate, w_exp_up, w_exp_down, final_norm_w):
    hp, hs = x_prompt, x_sample
    gla_p, gla_s, k_p, v_p, lf_p, k_s, v_s, lf_s = [], [], [], [], [], [], [], []
    for l in range(DEPTH):
        lw = (norm_mix_w[l], w_in[l], w_gla_a2[l], b_gla_a2[l], gla_norm_w[l], b_fox_f[l],
              w_up_gla[l], w_up_fox[l], w_out[l], norm_ffn_w[l], w_router_group[l], b_router_group[l],
              w_router_expert[l], b_router_expert[l], w_exp_gate[l], w_exp_up[l], w_exp_down[l])
        S0_prompt = jnp.zeros((hp.shape[0], GLA_HEADS, GLA_DK, GLA_DV), state_gla.dtype)
        hp, Sp, kp, vp, lfp = trunk_layer(hp, S0_prompt, fox_prompt, *lw)
        fox_s = functools.partial(fox_sample, l, cache_k, cache_v, cache_logf, page_table)
        hs, Ss, ks_, vs_, lfs = trunk_layer(hs, state_gla[l], fox_s, *lw)
        gla_p.append(Sp)
        gla_s.append(Ss)
        k_p.append(kp)
        v_p.append(vp)
        lf_p.append(lfp)
        k_s.append(ks_)
        v_s.append(vs_)
        lf_s.append(lfs)
    y_prompt = rmsnorm(hp, final_norm_w)
    y_sample = rmsnorm(hs, final_norm_w)
    return (y_prompt, y_sample, jnp.stack(gla_p), jnp.stack(gla_s), jnp.stack(k_p), jnp.stack(v_p),
            jnp.stack(lf_p), jnp.stack(k_s), jnp.stack(v_s), jnp.stack(lf_s))
```

```python
import functools

import jax
import jax.numpy as jnp
import numpy as np
from jax import lax
from jax.experimental import pallas as pl
from jax.experimental.pallas import tpu as pltpu

F32 = jnp.float32
BF16 = jnp.bfloat16

RMS_EPS = 1e-6
GLA_TAU = 16.0
GLA_CHUNK = 64
TOP_K_INNER = 2
NEG = -1e30

V7X_VMEM_LIMIT_BYTES = 56 * 1024 * 1024
LANES = 128


def _cparams(sem):
    return pltpu.CompilerParams(dimension_semantics=sem, vmem_limit_bytes=V7X_VMEM_LIMIT_BYTES)


def _sigmoid(x):
    return 1.0 / (1.0 + jnp.exp(-x))


def _log_sigmoid(x):
    return jnp.minimum(x, 0.0) - jnp.log(1.0 + jnp.exp(-jnp.abs(x)))


def _rms(x, w):
    ms = jnp.mean(x * x, axis=-1, keepdims=True)
    return x * lax.rsqrt(ms + RMS_EPS) * w


def _split3(x):
    p1 = x.astype(BF16)
    r1 = x - p1.astype(F32)
    p2 = r1.astype(BF16)
    p3 = (r1 - p2.astype(F32)).astype(BF16)
    return p1, p2, p3


def _dot(a, b):
    return jnp.dot(a, b, preferred_element_type=F32)


def _dot_nt(a, b):
    return lax.dot_general(a, b, (((1,), (1,)), ((), ())), preferred_element_type=F32)


def _in_proj_kernel(x_ref, nw_ref, w_ref, w2_ref, b2_ref, bf_ref,
                    gla_ref, loga_ref, qf_ref, kf_ref, vf_ref, sga_ref, sgb_ref, lft_ref, *, dims):
    n_gla, n_fox, d_model, n_lr, n_fh = dims
    xn = _rms(x_ref[...], nw_ref[...]).astype(BF16)

    def mm(lo, hi):
        return _dot(xn, w_ref[:, lo:hi])

    o = 0
    gla_ref[...] = mm(o, o + n_gla)
    o += n_gla
    fw = n_fox // 3
    qf_ref[...] = mm(o, o + fw)
    kf_ref[...] = mm(o + fw, o + 2 * fw)
    vf_ref[...] = mm(o + 2 * fw, o + 3 * fw)
    o += n_fox
    sga_ref[...] = _sigmoid(mm(o, o + d_model))
    sgb_ref[...] = _sigmoid(mm(o + d_model, o + 2 * d_model))
    o += 2 * d_model
    small = mm(o, o + LANES)
    pre = _dot(small.astype(BF16), w2_ref[...]) + b2_ref[...]
    loga_ref[...] = _log_sigmoid(pre) * (1.0 / GLA_TAU)
    small_t = small.T
    lft_ref[...] = _log_sigmoid(small_t[n_lr:n_lr + n_fh, :] + bf_ref[...])


def _in_proj(x2, nw, w_all, w2p, b2, bfc, dims, tm):
    n_gla, n_fox, d_model, n_lr, n_fh = dims
    t = x2.shape[0]
    fw = n_fox // 3
    n_qk = w2p.shape[1]
    row = lambda i: (i, 0)
    const = lambda i: (0, 0)
    out_shape = (
        jax.ShapeDtypeStruct((t, n_gla), F32),
        jax.ShapeDtypeStruct((t, n_qk), F32),
        jax.ShapeDtypeStruct((t, fw), F32),
        jax.ShapeDtypeStruct((t, fw), F32),
        jax.ShapeDtypeStruct((t, fw), F32),
        jax.ShapeDtypeStruct((t, d_model), F32),
        jax.ShapeDtypeStruct((t, d_model), F32),
        jax.ShapeDtypeStruct((n_fh, t), F32),
    )
    return pl.pallas_call(
        functools.partial(_in_proj_kernel, dims=dims),
        grid=(t // tm,),
        in_specs=[
            pl.BlockSpec((tm, d_model), row),
            pl.BlockSpec((1, d_model), const),
            pl.BlockSpec(w_all.shape, const),
            pl.BlockSpec(w2p.shape, const),
            pl.BlockSpec((1, n_qk), const),
            pl.BlockSpec((n_fh, 1), const),
        ],
        out_specs=(
            pl.BlockSpec((tm, n_gla), row),
            pl.BlockSpec((tm, n_qk), row),
            pl.BlockSpec((tm, fw), row),
            pl.BlockSpec((tm, fw), row),
            pl.BlockSpec((tm, fw), row),
            pl.BlockSpec((tm, d_model), row),
            pl.BlockSpec((tm, d_model), row),
            pl.BlockSpec((n_fh, tm), lambda i: (0, i)),
        ),
        out_shape=out_shape,
        compiler_params=_cparams(("parallel",)),
        name="in_proj",
    )(x2, nw, w_all, w2p, b2, bfc)


def _gla_kernel(gin_ref, loga_ref, s0_ref, nw_ref, og_ref, sout_ref, s_scr, *,
                chunk, n_chunks, n_seq, heads, dk, dv, mx_dtype):
    j = pl.program_id(1)
    qk = heads * dk
    vw = heads * dv
    kpad = max(chunk, LANES)

    @pl.when(j == 0)
    def _():
        s_scr[...] = s0_ref[...]

    rio = lax.broadcasted_iota(jnp.int32, (chunk, kpad), 0)
    cio = lax.broadcasted_iota(jnp.int32, (chunk, kpad), 1)
    tril = cio <= rio
    tril_m = jnp.where(tril, 1.0, 0.0).astype(mx_dtype)
    nw = nw_ref[...]

    def pad_rows(a):
        if kpad == chunk:
            return a
        return jnp.concatenate([a, jnp.zeros((kpad - chunk, a.shape[1]), a.dtype)], axis=0)

    def body(idx, carry):
        seq = idx // n_chunks
        r0 = pl.multiple_of(idx * chunk, chunk)
        g = loga_ref[pl.ds(r0, chunk), :]
        gs = pad_rows(jnp.concatenate([p.astype(mx_dtype) for p in _split3(g)], axis=1))
        bs = _dot(tril_m, gs)
        b = bs[:, :qk] + bs[:, qk:2 * qk] + bs[:, 2 * qk:]
        b_last = b[chunk - 1:chunk, :]
        q = gin_ref[pl.ds(r0, chunk), 0:qk]
        k = gin_ref[pl.ds(r0, chunk), qk:2 * qk]
        v = gin_ref[pl.ds(r0, chunk), 2 * qk:2 * qk + vw]
        r = gin_ref[pl.ds(r0, chunk), 2 * qk + vw:2 * qk + 2 * vw]
        q_i = (q * jnp.exp(b) * dk ** -0.5).astype(mx_dtype)
        k_i = pad_rows(k * jnp.exp(-b)).astype(mx_dtype)
        k_end_t = pad_rows(k * jnp.exp(b_last - b)).T.astype(mx_dtype)
        decay = jnp.exp(pad_rows(b).T[:, chunk - 1:chunk])
        vb = pad_rows(v).astype(mx_dtype)
        for h in range(heads):
            qh = q_i[:, h * dk:(h + 1) * dk]
            kh = k_i[:, h * dk:(h + 1) * dk]
            vh = vb[:, h * dv:(h + 1) * dv]
            a = jnp.where(tril, _dot_nt(qh, kh), 0.0).astype(mx_dtype)
            s_old = s_scr[seq, h]
            o = _dot(a, vh) + _dot(qh, s_old.astype(mx_dtype))
            kv = _dot(k_end_t[h * dk:(h + 1) * dk, :], vh)
            s_scr[seq, h] = decay[h * dk:(h + 1) * dk, :] * s_old + kv
            rh = r[:, h * dv:(h + 1) * dv]
            og_ref[pl.ds(r0, chunk), h * dv:(h + 1) * dv] = (_rms(o, nw) * (rh * _sigmoid(rh))).astype(og_ref.dtype)
        return carry

    lax.fori_loop(0, n_seq * n_chunks, body, 0)

    @pl.when(j == pl.num_programs(1) - 1)
    def _():
        sout_ref[...] = s_scr[...]


def _gla(gin, loga, s0, nw, *, n_outer, n_steps, n_seq, n_chunks, chunk, mx_dtype, out_dtype):
    t = gin.shape[0]
    _, heads, dk, dv = s0.shape
    rows = n_seq * n_chunks * chunk
    assert t == n_outer * n_steps * rows
    kern = functools.partial(_gla_kernel, chunk=chunk, n_chunks=n_chunks, n_seq=n_seq, heads=heads,
                             dk=dk, dv=dv, mx_dtype=mx_dtype)
    tok = lambda i, j: (i * n_steps + j, 0)
    st = lambda i, j: (i, 0, 0, 0)
    return pl.pallas_call(
        kern,
        grid=(n_outer, n_steps),
        in_specs=[
            pl.BlockSpec((rows, gin.shape[1]), tok),
            pl.BlockSpec((rows, loga.shape[1]), tok),
            pl.BlockSpec((n_seq, heads, dk, dv), st),
            pl.BlockSpec((1, dv), lambda i, j: (0, 0)),
        ],
        out_specs=(
            pl.BlockSpec((rows, heads * dv), tok),
            pl.BlockSpec((n_seq, heads, dk, dv), st),
        ),
        out_shape=(
            jax.ShapeDtypeStruct((t, heads * dv), out_dtype),
            jax.ShapeDtypeStruct(s0.shape, F32),
        ),
        scratch_shapes=[pltpu.VMEM((n_seq, heads, dk, dv), F32)],
        compiler_params=_cparams(("parallel", "arbitrary")),
        name="gla",
    )(gin, loga, s0, nw)


def _fox_prep_kernel(qf_ref, kf_ref, lft_ref, rq_ref, rk_ref, qa_ref, ka_ref, carry_scr, *, heads, dh):
    j = pl.program_id(1)
    tp = qf_ref.shape[0]

    @pl.when(j == 0)
    def _():
        carry_scr[...] = jnp.zeros_like(carry_scr)

    rio = lax.broadcasted_iota(jnp.int32, (tp, tp), 0)
    cio = lax.broadcasted_iota(jnp.int32, (tp, tp), 1)
    upper = jnp.where(rio <= cio, 1.0, 0.0).astype(BF16)
    l1, l2, l3 = _split3(lft_ref[...])
    cs = _dot(jnp.concatenate([l1, l2, l3, jnp.zeros_like(l1)], axis=0), upper)
    c_t = cs[0:heads] + cs[heads:2 * heads] + cs[2 * heads:3 * heads] + carry_scr[:, 0:1]
    carry_scr[...] = jnp.broadcast_to(c_t[:, tp - 1:tp], carry_scr.shape)
    c1, c2, c3 = _split3(c_t)
    pad_rows = LANES - 3 * heads - 8
    aug_t = jnp.concatenate([c1.astype(F32), c2.astype(F32), c3.astype(F32),
                             jnp.ones((8, tp), F32), jnp.zeros((pad_rows, tp), F32)], axis=0)
    aug = aug_t.T.astype(BF16)
    lq = jnp.concatenate([(qf_ref[...] * dh ** -0.5).astype(BF16), aug], axis=1)
    lk = jnp.concatenate([kf_ref[...].astype(BF16), aug], axis=1)
    qa = _dot(lq, rq_ref[...]).astype(BF16)
    ka = _dot(lk, rk_ref[...]).astype(BF16)
    for h in range(heads):
        qa_ref[h] = qa[:, h * LANES:(h + 1) * LANES]
        ka_ref[h] = ka[:, h * LANES:(h + 1) * LANES]


def _placement_matrices(heads, dh):
    kin = heads * dh + LANES
    rq = np.zeros((kin, heads * LANES), np.float32)
    rk = np.zeros((kin, heads * LANES), np.float32)
    base = heads * dh
    ones_row = base + 3 * heads
    for h in range(heads):
        for d in range(dh):
            rq[h * dh + d, h * LANES + d] = 1.0
            rk[h * dh + d, h * LANES + d] = 1.0
        for p in range(3):
            rq[base + p * heads + h, h * LANES + dh + p] = 1.0
            rq[ones_row, h * LANES + dh + 3 + p] = 1.0
            rk[ones_row, h * LANES + dh + p] = 1.0
            rk[base + p * heads + h, h * LANES + dh + 3 + p] = -1.0
    return jnp.asarray(rq, BF16), jnp.asarray(rk, BF16)


def _fox_prep(qf, kf, lft, batch, seq, heads, dh, tp):
    rq, rk = _placement_matrices(heads, dh)
    n_steps = seq // tp
    tok = lambda b, j: (b * n_steps + j, 0)
    const = lambda b, j: (0, 0)
    aug_spec = pl.BlockSpec((None, heads, tp, LANES), lambda b, j: (b, 0, j, 0))
    return pl.pallas_call(
        functools.partial(_fox_prep_kernel, heads=heads, dh=dh),
        grid=(batch, n_steps),
        in_specs=[
            pl.BlockSpec((tp, heads * dh), tok),
            pl.BlockSpec((tp, heads * dh), tok),
            pl.BlockSpec((heads, tp), lambda b, j: (0, b * n_steps + j)),
            pl.BlockSpec(rq.shape, const),
            pl.BlockSpec(rk.shape, const),
        ],
        out_specs=(aug_spec, aug_spec),
        out_shape=(jax.ShapeDtypeStruct((batch, heads, seq, LANES), BF16),) * 2,
        scratch_shapes=[pltpu.VMEM((heads, LANES), F32)],
        compiler_params=_cparams(("parallel", "arbitrary")),
        name="fox_prep",
    )(qf, kf, lft, rq, rk)


def _fox_flash_kernel(qi_ref, ki_ref, qa_ref, ka_ref, v_ref, o_ref, m_scr, l_scr, acc_scr, *, dh):
    step = pl.program_id(2)
    qi = qi_ref[step]
    ki = ki_ref[step]
    tq = qa_ref.shape[1]
    tk = ka_ref.shape[1]

    @pl.when(ki == 0)
    def _():
        m_scr[...] = jnp.full_like(m_scr, NEG)
        l_scr[...] = jnp.zeros_like(l_scr)
        acc_scr[...] = jnp.zeros_like(acc_scr)

    def update(masked):
        vb = v_ref[...].astype(BF16)
        for h in range(2):
            s = _dot_nt(qa_ref[h], ka_ref[h])
            if masked:
                rio = lax.broadcasted_iota(jnp.int32, (tq, tk), 0)
                cio = lax.broadcasted_iota(jnp.int32, (tq, tk), 1)
                s = jnp.where(cio <= rio, s, NEG)
            m_prev = m_scr[h]
            m_new = jnp.maximum(m_prev, jnp.max(s, axis=1, keepdims=True))
            a = jnp.exp(m_prev - m_new)
            p = jnp.exp(s - m_new)
            l_scr[h] = a * l_scr[h] + jnp.sum(p, axis=1, keepdims=True)
            acc_scr[h] = a * acc_scr[h] + _dot(p.astype(BF16), vb)
            m_scr[h] = m_new

    @pl.when(ki < qi)
    def _():
        update(False)

    @pl.when(ki == qi)
    def _():
        update(True)
        lane = lax.broadcasted_iota(jnp.int32, (tq, 2 * dh), 1)
        o0 = acc_scr[0] / l_scr[0]
        o1 = acc_scr[1] / l_scr[1]
        o_ref[...] = jnp.where(lane < dh, o0, o1).astype(o_ref.dtype)


def _fox_flash(qa, ka, vf, batch, seq, heads, dh, tq):
    nq = seq // tq
    qi_l, ki_l = [], []
    for q in range(nq):
        for k in range(q + 1):
            qi_l.append(q)
            ki_l.append(k)
    qi_a = jnp.asarray(np.array(qi_l, np.int32))
    ki_a = jnp.asarray(np.array(ki_l, np.int32))
    n_tri = len(qi_l)
    pairs = heads // 2
    grid_spec = pltpu.PrefetchScalarGridSpec(
        num_scalar_prefetch=2,
        grid=(batch, pairs, n_tri),
        in_specs=[
            pl.BlockSpec((None, 2, tq, LANES), lambda b, p, s, qi, ki: (b, p, qi[s], 0)),
            pl.BlockSpec((None, 2, tq, LANES), lambda b, p, s, qi, ki: (b, p, ki[s], 0)),
            pl.BlockSpec((tq, 2 * dh), lambda b, p, s, qi, ki: (b * nq + ki[s], p)),
        ],
        out_specs=pl.BlockSpec((tq, 2 * dh), lambda b, p, s, qi, ki: (b * nq + qi[s], p)),
        scratch_shapes=[
            pltpu.VMEM((2, tq, 1), F32),
            pltpu.VMEM((2, tq, 1), F32),
            pltpu.VMEM((2, tq, 2 * dh), F32),
        ],
    )
    return pl.pallas_call(
        functools.partial(_fox_flash_kernel, dh=dh),
        grid_spec=grid_spec,
        out_shape=jax.ShapeDtypeStruct((batch * seq, heads * dh), BF16),
        compiler_params=_cparams(("parallel", "parallel", "arbitrary")),
        name="fox_flash",
    )(qi_a, ki_a, qa, ka, vf)


def _fox_paged_kernel(pt_ref, q_ref, kn_ref, vn_ref, lfnt_ref, *refs, heads, dh, n_new, ppb, page):
    k_refs = refs[0:ppb]
    v_refs = refs[ppb:2 * ppb]
    lf_refs = refs[2 * ppb:3 * ppb]
    o_ref = refs[3 * ppb]
    qbd_scr, cncol_scr, cns_scr, base_scr, m_scr, l_scr, acc_scr = refs[3 * ppb + 1:]
    j = pl.program_id(1)
    rows = heads * n_new
    width = heads * dh

    rr = lax.broadcasted_iota(jnp.int32, (rows, width), 0)
    cc = lax.broadcasted_iota(jnp.int32, (rows, width), 1)
    bd_mask = (rr // n_new) == (cc // dh)

    @pl.when(j == 0)
    def _():
        q = q_ref[...] * dh ** -0.5
        q_rep = jnp.broadcast_to(q[None], (heads, n_new, width)).reshape(rows, width)
        qbd_scr[...] = jnp.where(bd_mask, q_rep, 0.0).astype(BF16)
        lfnt = lfnt_ref[...]
        lane = lax.broadcasted_iota(jnp.int32, lfnt.shape, 1)
        cnt = jnp.zeros_like(lfnt)
        for s in range(n_new):
            col = jnp.sum(jnp.where(lane <= s, lfnt, 0.0), axis=1, keepdims=True)
            cnt = cnt + jnp.where(lane == s, col, 0.0)
        cns = jnp.broadcast_to(cnt[:, None, :], (heads, n_new, LANES)).reshape(rows, LANES)
        cns_scr[...] = cns
        r2 = lax.broadcasted_iota(jnp.int32, (rows, LANES), 0)
        l2 = lax.broadcasted_iota(jnp.int32, (rows, LANES), 1)
        cncol_scr[...] = jnp.sum(jnp.where(l2 == r2 % n_new, cns, 0.0), axis=1, keepdims=True)
        base_scr[...] = jnp.zeros_like(base_scr)
        m_scr[...] = jnp.full_like(m_scr, NEG)
        l_scr[...] = jnp.zeros_like(l_scr)
        acc_scr[...] = jnp.zeros_like(acc_scr)

    rio = lax.broadcasted_iota(jnp.int32, (page, page), 0)
    cio = lax.broadcasted_iota(jnp.int32, (page, page), 1)
    after = jnp.where(rio > cio, 1.0, 0.0).astype(BF16)
    base = base_scr[:, 0:1]
    biases = []
    for i in range(ppb):
        lf = lf_refs[i][...]
        p1, p2, p3 = _split3(lf)
        ss = _dot(jnp.concatenate([p1, p2, p3, jnp.zeros_like(p1)], axis=0), after)
        suf = ss[0:heads] + ss[heads:2 * heads] + ss[2 * heads:3 * heads]
        biases.append(base + suf)
        base = base + suf[:, 0:1] + lf[:, 0:1]
    base_scr[...] = jnp.broadcast_to(base, base_scr.shape)
    bias = jnp.concatenate(biases, axis=1)
    bias_rows = jnp.broadcast_to(bias[:, None, :], (heads, n_new, ppb * page)).reshape(rows, ppb * page)

    kt = jnp.concatenate([k_refs[i][...].astype(BF16) for i in range(ppb)], axis=1)
    vt = jnp.concatenate([v_refs[i][...].astype(BF16) for i in range(ppb)], axis=1)
    qbd = qbd_scr[...]
    s = _dot(qbd, kt) + cncol_scr[...] + bias_rows
    m_prev = m_scr[...]
    m_new = jnp.maximum(m_prev, jnp.max(s, axis=1, keepdims=True))
    a = jnp.exp(m_prev - m_new)
    p = jnp.exp(s - m_new)
    l_scr[...] = a * l_scr[...] + jnp.sum(p, axis=1, keepdims=True)
    acc_scr[...] = a * acc_scr[...] + _dot_nt(p.astype(BF16), vt)
    m_scr[...] = m_new

    @pl.when(j == pl.num_programs(1) - 1)
    def _():
        pad = jnp.zeros((LANES - n_new, width), F32)
        kn = jnp.concatenate([kn_ref[...], pad], axis=0).astype(BF16)
        vn = jnp.concatenate([vn_ref[...], pad], axis=0).astype(BF16)
        r2 = lax.broadcasted_iota(jnp.int32, (rows, LANES), 0)
        l2 = lax.broadcasted_iota(jnp.int32, (rows, LANES), 1)
        sn = _dot_nt(qbd, kn) + cncol_scr[...] - cns_scr[...]
        sn = jnp.where(l2 <= r2 % n_new, sn, NEG)
        m_p = m_scr[...]
        m_n = jnp.maximum(m_p, jnp.max(sn, axis=1, keepdims=True))
        a2 = jnp.exp(m_p - m_n)
        pn = jnp.exp(sn - m_n)
        l_f = a2 * l_scr[...] + jnp.sum(pn, axis=1, keepdims=True)
        acc_f = a2 * acc_scr[...] + _dot(pn.astype(BF16), vn)
        o = jnp.where(bd_mask, acc_f / l_f, 0.0)
        o_ref[...] = jnp.sum(o.reshape(heads, n_new, width), axis=0)


def _fox_paged(pt_flat, qf, kf, vf, lfnt_pad, ck, cv, clf, *, n_seq, n_new, n_pages, heads, dh, ppb):
    page = ck.shape[-1]
    width = heads * dh
    rows = heads * n_new
    n_steps = n_pages // ppb

    def page_map(i):
        return lambda b, j, pt: (pt[b * n_pages + (n_pages - 1 - (j * ppb + i))], 0, 0)

    tok = pl.BlockSpec((n_new, width), lambda b, j, pt: (b, 0))
    in_specs = [tok, tok, tok, pl.BlockSpec((None, heads, LANES), lambda b, j, pt: (b, 0, 0))]
    in_specs += [pl.BlockSpec((None, width, page), page_map(i)) for i in range(ppb)]
    in_specs += [pl.BlockSpec((None, width, page), page_map(i)) for i in range(ppb)]
    in_specs += [pl.BlockSpec((None, heads, page), page_map(i)) for i in range(ppb)]
    grid_spec = pltpu.PrefetchScalarGridSpec(
        num_scalar_prefetch=1,
        grid=(n_seq, n_steps),
        in_specs=in_specs,
        out_specs=pl.BlockSpec((n_new, width), lambda b, j, pt: (b, 0)),
        scratch_shapes=[
            pltpu.VMEM((rows, width), BF16),
            pltpu.VMEM((rows, 1), F32),
            pltpu.VMEM((rows, LANES), F32),
            pltpu.VMEM((heads, LANES), F32),
            pltpu.VMEM((rows, 1), F32),
            pltpu.VMEM((rows, 1), F32),
            pltpu.VMEM((rows, width), F32),
        ],
    )
    kern = functools.partial(_fox_paged_kernel, heads=heads, dh=dh, n_new=n_new, ppb=ppb, page=page)
    return pl.pallas_call(
        kern,
        grid_spec=grid_spec,
        out_shape=jax.ShapeDtypeStruct((n_seq * n_new, width), F32),
        compiler_params=_cparams(("parallel", "arbitrary")),
        name="fox_paged",
    )(pt_flat, qf, kf, vf, lfnt_pad, *([ck] * ppb), *([cv] * ppb), *([clf] * ppb))


def _out_proj_kernel(x_ref, og_ref, of_ref, sga_ref, sgb_ref, wug_ref, wuf_ref, wo_ref, nfw_ref, wr_ref, br_ref,
                     h_ref, xn_ref, gates_ref, *, n_experts, n_groups):
    up_a = _dot(og_ref[...].astype(BF16), wug_ref[...])
    up_b = _dot(of_ref[...].astype(BF16), wuf_ref[...])
    merged = sga_ref[...] * up_a + sgb_ref[...] * up_b
    h = x_ref[...] + _dot(merged.astype(BF16), wo_ref[...])
    h_ref[...] = h
    xn = _rms(h, nfw_ref[...])
    xn_ref[...] = xn.astype(BF16)

    logits = jnp.dot(xn, wr_ref[...], preferred_element_type=F32, precision=lax.Precision.HIGHEST) + br_ref[...]
    lane = lax.broadcasted_iota(jnp.int32, logits.shape, 1).astype(F32)
    epg = n_experts // n_groups
    big = 4.0 * LANES
    is_g = (lane >= n_experts) & (lane < n_experts + n_groups)
    gl = jnp.where(is_g, logits, NEG)
    gmax = jnp.max(gl, axis=1, keepdims=True)
    gsum = jnp.sum(jnp.exp(gl - gmax), axis=1, keepdims=True)
    p_g = 1.0 / gsum
    g_sel = jnp.min(jnp.where(gl == gmax, lane, big), axis=1, keepdims=True) - n_experts
    e_lo = g_sel * epg
    in_grp = (lane >= e_lo) & (lane < e_lo + epg)
    el = jnp.where(in_grp, logits, NEG)
    emax = jnp.max(el, axis=1, keepdims=True)
    ee = jnp.exp(el - emax)
    pe = ee / jnp.sum(ee, axis=1, keepdims=True)
    cand = jnp.where(in_grp, pe, -1.0)
    v1 = jnp.max(cand, axis=1, keepdims=True)
    i1 = jnp.min(jnp.where(cand == v1, lane, big), axis=1, keepdims=True)
    cand2 = jnp.where(lane == i1, -1.0, cand)
    v2 = jnp.max(cand2, axis=1, keepdims=True)
    i2 = jnp.min(jnp.where(cand2 == v2, lane, big), axis=1, keepdims=True)
    tot = v1 + v2
    gates_ref[...] = jnp.where(lane == i1, p_g * (v1 / tot), 0.0) + jnp.where(lane == i2, p_g * (v2 / tot), 0.0)


def _out_proj(x2, og, of, sga, sgb, wug, wuf, wo, nfw, wr, br, *, n_experts, n_groups, tm):
    t, d_model = x2.shape
    row = lambda i: (i, 0)
    const = lambda i: (0, 0)
    return pl.pallas_call(
        functools.partial(_out_proj_kernel, n_experts=n_experts, n_groups=n_groups),
        grid=(t // tm,),
        in_specs=[
            pl.BlockSpec((tm, d_model), row),
            pl.BlockSpec((tm, og.shape[1]), row),
            pl.BlockSpec((tm, of.shape[1]), row),
            pl.BlockSpec((tm, d_model), row),
            pl.BlockSpec((tm, d_model), row),
            pl.BlockSpec(wug.shape, const),
            pl.BlockSpec(wuf.shape, const),
            pl.BlockSpec(wo.shape, const),
            pl.BlockSpec((1, d_model), const),
            pl.BlockSpec(wr.shape, const),
            pl.BlockSpec((1, LANES), const),
        ],
        out_specs=(
            pl.BlockSpec((tm, d_model), row),
            pl.BlockSpec((tm, d_model), row),
            pl.BlockSpec((tm, LANES), row),
        ),
        out_shape=(
            jax.ShapeDtypeStruct((t, d_model), F32),
            jax.ShapeDtypeStruct((t, d_model), BF16),
            jax.ShapeDtypeStruct((t, LANES), F32),
        ),
        compiler_params=_cparams(("parallel",)),
        name="out_proj",
    )(x2, og, of, sga, sgb, wug, wuf, wo, nfw, wr, br)


def _moe_kernel(xn_ref, gates_ref, h_ref, wg_ref, wu_ref, wd_ref, fnw_ref, y_ref, acc_scr):
    e = pl.program_id(1)

    @pl.when(e == 0)
    def _():
        acc_scr[...] = jnp.zeros_like(acc_scr)

    x = xn_ref[...]
    g = _dot(x, wg_ref[...])
    u = _dot(x, wu_ref[...])
    hid = (g * _sigmoid(g)) * u
    gates = gates_ref[...]
    lane = lax.broadcasted_iota(jnp.int32, gates.shape, 1)
    gate_e = jnp.sum(jnp.where(lane == e, gates, 0.0), axis=1, keepdims=True)
    acc_scr[...] += gate_e * _dot(hid.astype(BF16), wd_ref[...])

    @pl.when(e == pl.num_programs(1) - 1)
    def _():
        y_ref[...] = _rms(h_ref[...] + acc_scr[...], fnw_ref[...])


def _moe(xn, gates, h, wg, wu, wd, fnw, *, tm):
    t, d_model = h.shape
    n_experts, _, d_exp = wg.shape
    row = lambda i, e: (i, 0)
    return pl.pallas_call(
        _moe_kernel,
        grid=(t // tm, n_experts),
        in_specs=[
            pl.BlockSpec((tm, d_model), row),
            pl.BlockSpec((tm, LANES), row),
            pl.BlockSpec((tm, d_model), row),
            pl.BlockSpec((None, d_model, d_exp), lambda i, e: (e, 0, 0)),
            pl.BlockSpec((None, d_model, d_exp), lambda i, e: (e, 0, 0)),
            pl.BlockSpec((None, d_exp, d_model), lambda i, e: (e, 0, 0)),
            pl.BlockSpec((1, d_model), lambda i, e: (0, 0)),
        ],
        out_specs=pl.BlockSpec((tm, d_model), row),
        out_shape=jax.ShapeDtypeStruct((t, d_model), F32),
        scratch_shapes=[pltpu.VMEM((tm, d_model), F32)],
        compiler_params=_cparams(("parallel", "arbitrary")),
        name="moe",
    )(xn, gates, h, wg, wu, wd, fnw)


def _pick_tile(n, pref):
    t = min(n, pref)
    while n % t:
        t //= 2
    return t


def kernel(x_prompt, x_sample, state_gla, cache_k, cache_v, cache_logf, page_table, norm_mix_w, w_in, w_gla_a2,
           b_gla_a2, gla_norm_w, b_fox_f, w_up_gla, w_up_fox, w_out, norm_ffn_w, w_router_group, b_router_group,
           w_router_expert, b_router_expert, w_exp_gate, w_exp_up, w_exp_down, final_norm_w):
    depth = w_in.shape[0]
    assert depth == 1
    batch, seq, d_model = x_prompt.shape
    n_seq, n_new, _ = x_sample.shape
    _, _, g_heads, dk, dv = state_gla.shape
    _, n_phys, page, f_heads, dh = cache_k.shape
    n_pages = page_table.shape[1]
    n_lr = w_gla_a2.shape[1]
    n_groups = w_router_group.shape[2]
    n_experts = w_router_expert.shape[2]
    qk = g_heads * dk
    vw = g_heads * dv
    fw = f_heads * dh
    n_gla = 2 * qk + 2 * vw
    n_fox = 3 * fw
    dims = (n_gla, n_fox, d_model, n_lr, f_heads)

    wi = w_in[0]
    o_za = n_gla
    o_fox = o_za + n_lr
    o_fp = o_fox + n_fox
    o_gate = o_fp + f_heads
    w_small = jnp.concatenate([wi[:, o_za:o_fox], wi[:, o_fp:o_gate],
                               jnp.zeros((d_model, LANES - n_lr - f_heads), wi.dtype)], axis=1)
    w_all = jnp.concatenate([wi[:, :o_za], wi[:, o_fox:o_fp], wi[:, o_gate:], w_small], axis=1).astype(BF16)
    w2p = jnp.concatenate([w_gla_a2[0], jnp.zeros((LANES - n_lr, qk), F32)], axis=0).astype(BF16)
    b2 = b_gla_a2[0].reshape(1, qk)
    bfc = b_fox_f[0].reshape(f_heads, 1)
    nw_mix = norm_mix_w[0].reshape(1, d_model)
    nw_gla = gla_norm_w[0].reshape(1, dv)
    wug = w_up_gla[0].astype(BF16)
    wuf = w_up_fox[0].astype(BF16)
    wo = w_out[0].astype(BF16)
    nfw = norm_ffn_w[0].reshape(1, d_model)
    wr = jnp.concatenate([w_router_expert[0], w_router_group[0],
                          jnp.zeros((d_model, LANES - n_experts - n_groups), F32)], axis=1)
    br = jnp.concatenate([b_router_expert[0], b_router_group[0],
                          jnp.zeros((LANES - n_experts - n_groups,), F32)]).reshape(1, LANES)
    wg = w_exp_gate[0].astype(BF16)
    wu = w_exp_up[0].astype(BF16)
    wd = w_exp_down[0].astype(BF16)
    fnw = final_norm_w.reshape(1, d_model)

    def token_path(x2, mixers):
        tm = _pick_tile(x2.shape[0], 256)
        gin, loga, qf, kf, vf, sga, sgb, lft = _in_proj(x2, nw_mix, w_all, w2p, b2, bfc, dims, tm)
        og, s_new, of = mixers(gin, loga, qf, kf, vf, lft)
        h, xn, gates = _out_proj(x2, og, of, sga, sgb, wug, wuf, wo, nfw, wr, br,
                                 n_experts=n_experts, n_groups=n_groups, tm=tm)
        y = _moe(xn, gates, h, wg, wu, wd, fnw, tm=_pick_tile(x2.shape[0], 1024))
        return y, s_new, kf, vf, lft.T

    def prompt_mixers(gin, loga, qf, kf, vf, lft):
        chunk = np.gcd(seq, GLA_CHUNK)
        n_chunks = _pick_tile(seq // chunk, 8)
        s0 = jnp.zeros((batch, g_heads, dk, dv), F32)
        og, s_new = _gla(gin, loga, s0, nw_gla, n_outer=batch, n_steps=seq // (chunk * n_chunks), n_seq=1,
                         n_chunks=n_chunks, chunk=int(chunk), mx_dtype=BF16, out_dtype=BF16)
        tp = _pick_tile(seq, 512)
        qa, ka = _fox_prep(qf, kf, lft, batch, seq, f_heads, dh, tp)
        of = _fox_flash(qa, ka, vf, batch, seq, f_heads, dh, tp)
        return og, s_new, of

    y_p, s_p, k_p, v_p, lf_p = token_path(x_prompt.reshape(batch * seq, d_model), prompt_mixers)

    ck = jnp.transpose(cache_k[0], (0, 2, 3, 1)).reshape(n_phys, fw, page)
    cv = jnp.transpose(cache_v[0], (0, 2, 3, 1)).reshape(n_phys, fw, page)
    clf = jnp.transpose(cache_logf[0], (0, 2, 1))
    pt_flat = page_table.reshape(-1).astype(jnp.int32)

    def sample_mixers(gin, loga, qf, kf, vf, lft):
        chunk = int(np.gcd(n_new, GLA_CHUNK))
        assert chunk == n_new
        gs = _pick_tile(n_seq, 8)
        og, s_new = _gla(gin, loga, state_gla[0], nw_gla, n_outer=n_seq // gs, n_steps=1, n_seq=gs,
                         n_chunks=1, chunk=chunk, mx_dtype=F32, out_dtype=F32)
        lfnt = jnp.transpose(lft.reshape(f_heads, n_seq, n_new), (1, 0, 2))
        lfnt_pad = jnp.pad(lfnt, ((0, 0), (0, 0), (0, LANES - n_new)))
        of = _fox_paged(pt_flat, qf, kf, vf, lfnt_pad, ck, cv, clf, n_seq=n_seq, n_new=n_new,
                        n_pages=n_pages, heads=f_heads, dh=dh, ppb=_pick_tile(n_pages, 4))
        return og, s_new, of

    y_s, s_s, k_s, v_s, lf_s = token_path(x_sample.reshape(n_seq * n_new, d_model), sample_mixers)

    return (y_p.reshape(batch, seq, d_model),
            y_s.reshape(n_seq, n_new, d_model),
            s_p[None],
            s_s[None],
            k_p.reshape(1, batch, seq, f_heads, dh),
            v_p.reshape(1, batch, seq, f_heads, dh),
            lf_p.reshape(1, batch, seq, f_heads),
            k_s.reshape(1, n_seq, n_new, f_heads, dh),
            v_s.reshape(1, n_seq, n_new, f_heads, dh),
            lf_s.reshape(1, n_seq, n_new, f_heads))
```

```python
import functools
import math

import jax
import jax.numpy as jnp
import numpy as np
from jax import lax
from jax.experimental import pallas as pl
from jax.experimental.pallas import tpu as pltpu

F32 = jnp.float32
BF16 = jnp.bfloat16

RMS_EPS = 1e-6
GLA_TAU = 16.0
GLA_CHUNK = 64
FLASH_HEADS_PER_STEP = 2
PAGES_PER_STEP = 16
NEG = -1e30

V7X_VMEM_LIMIT_BYTES = 56 * 1024 * 1024
LANES = 128


def _cparams(sem):
    return pltpu.CompilerParams(dimension_semantics=sem, vmem_limit_bytes=V7X_VMEM_LIMIT_BYTES)


def _sigmoid(x):
    return 1.0 / (1.0 + jnp.exp(-x))


def _log_sigmoid(x):
    return jnp.minimum(x, 0.0) - jnp.log(1.0 + jnp.exp(-jnp.abs(x)))


def _rms(x, w):
    ms = jnp.mean(x * x, axis=-1, keepdims=True)
    return x * lax.rsqrt(ms + RMS_EPS) * w


def _split3(x):
    p1 = x.astype(BF16)
    r1 = x - p1.astype(F32)
    p2 = r1.astype(BF16)
    p3 = (r1 - p2.astype(F32)).astype(BF16)
    return p1, p2, p3


def _dot(a, b):
    return jnp.dot(a, b, preferred_element_type=F32)


def _dot_nt(a, b):
    return lax.dot_general(a, b, (((1,), (1,)), ((), ())), preferred_element_type=F32)


def _in_proj_kernel(x_ref, nw_ref, w_ref, w2_ref, b2_ref, bf_ref,
                    gla_ref, loga_ref, qf_ref, kf_ref, vf_ref, sga_ref, sgb_ref, lft_ref, *, dims):
    n_gla, n_fox, d_model, n_lr, n_fh = dims
    xn = _rms(x_ref[...], nw_ref[...]).astype(BF16)

    def mm(lo, hi):
        return _dot(xn, w_ref[:, lo:hi])

    o = 0
    gla_ref[...] = mm(o, o + n_gla)
    o += n_gla
    fw = n_fox // 3
    qf_ref[...] = mm(o, o + fw)
    kf_ref[...] = mm(o + fw, o + 2 * fw)
    vf_ref[...] = mm(o + 2 * fw, o + 3 * fw)
    o += n_fox
    sga_ref[...] = _sigmoid(mm(o, o + d_model))
    sgb_ref[...] = _sigmoid(mm(o + d_model, o + 2 * d_model))
    o += 2 * d_model
    small = mm(o, o + LANES)
    pre = _dot(small.astype(BF16), w2_ref[...]) + b2_ref[...]
    loga_ref[...] = _log_sigmoid(pre) * (1.0 / GLA_TAU)
    small_t = small.T
    lft_ref[...] = _log_sigmoid(small_t[n_lr:n_lr + n_fh, :] + bf_ref[...])


def _in_proj(x2, nw, w_all, w2p, b2, bfc, dims, tm):
    n_gla, n_fox, d_model, n_lr, n_fh = dims
    t = x2.shape[0]
    fw = n_fox // 3
    n_qk = w2p.shape[1]
    row = lambda i: (i, 0)
    const = lambda i: (0, 0)
    out_shape = (
        jax.ShapeDtypeStruct((t, n_gla), F32),
        jax.ShapeDtypeStruct((t, n_qk), F32),
        jax.ShapeDtypeStruct((t, fw), F32),
        jax.ShapeDtypeStruct((t, fw), F32),
        jax.ShapeDtypeStruct((t, fw), F32),
        jax.ShapeDtypeStruct((t, d_model), F32),
        jax.ShapeDtypeStruct((t, d_model), F32),
        jax.ShapeDtypeStruct((n_fh, t), F32),
    )
    return pl.pallas_call(
        functools.partial(_in_proj_kernel, dims=dims),
        grid=(t // tm,),
        in_specs=[
            pl.BlockSpec((tm, d_model), row),
            pl.BlockSpec((1, d_model), const),
            pl.BlockSpec(w_all.shape, const),
            pl.BlockSpec(w2p.shape, const),
            pl.BlockSpec((1, n_qk), const),
            pl.BlockSpec((n_fh, 1), const),
        ],
        out_specs=(
            pl.BlockSpec((tm, n_gla), row),
            pl.BlockSpec((tm, n_qk), row),
            pl.BlockSpec((tm, fw), row),
            pl.BlockSpec((tm, fw), row),
            pl.BlockSpec((tm, fw), row),
            pl.BlockSpec((tm, d_model), row),
            pl.BlockSpec((tm, d_model), row),
            pl.BlockSpec((n_fh, tm), lambda i: (0, i)),
        ),
        out_shape=out_shape,
        compiler_params=_cparams(("parallel",)),
        name="in_proj",
    )(x2, nw, w_all, w2p, b2, bfc)


def _gla_kernel(gin_ref, loga_ref, s0_ref, nw_ref, og_ref, sout_ref, s_scr, *,
                chunk, n_chunks, n_seq, heads, dk, dv, mx_dtype):
    j = pl.program_id(1)
    qk = heads * dk
    vw = heads * dv
    kpad = max(chunk, LANES)

    @pl.when(j == 0)
    def _():
        s_scr[...] = s0_ref[...]

    rio = lax.broadcasted_iota(jnp.int32, (chunk, kpad), 0)
    cio = lax.broadcasted_iota(jnp.int32, (chunk, kpad), 1)
    tril = cio <= rio
    tril_m = jnp.where(tril, 1.0, 0.0).astype(mx_dtype)
    nw = nw_ref[...]

    def pad_rows(a):
        if kpad == chunk:
            return a
        return jnp.concatenate([a, jnp.zeros((kpad - chunk, a.shape[1]), a.dtype)], axis=0)

    def body(idx, carry):
        seq = idx // n_chunks
        r0 = pl.multiple_of(idx * chunk, chunk)
        g = loga_ref[pl.ds(r0, chunk), :]
        gs = pad_rows(jnp.concatenate([p.astype(mx_dtype) for p in _split3(g)], axis=1))
        bs = _dot(tril_m, gs)
        b = bs[:, :qk] + bs[:, qk:2 * qk] + bs[:, 2 * qk:]
        b_last = b[chunk - 1:chunk, :]
        q = gin_ref[pl.ds(r0, chunk), 0:qk]
        k = gin_ref[pl.ds(r0, chunk), qk:2 * qk]
        v = gin_ref[pl.ds(r0, chunk), 2 * qk:2 * qk + vw]
        r = gin_ref[pl.ds(r0, chunk), 2 * qk + vw:2 * qk + 2 * vw]
        q_i = (q * jnp.exp(b) * dk ** -0.5).astype(mx_dtype)
        k_i = pad_rows(k * jnp.exp(-b)).astype(mx_dtype)
        k_end_t = pad_rows(k * jnp.exp(b_last - b)).T.astype(mx_dtype)
        decay = jnp.exp(pad_rows(b).T[:, chunk - 1:chunk])
        vb = pad_rows(v).astype(mx_dtype)
        for h in range(heads):
            qh = q_i[:, h * dk:(h + 1) * dk]
            kh = k_i[:, h * dk:(h + 1) * dk]
            vh = vb[:, h * dv:(h + 1) * dv]
            a = jnp.where(tril, _dot_nt(qh, kh), 0.0).astype(mx_dtype)
            s_old = s_scr[seq, h]
            o = _dot(a, vh) + _dot(qh, s_old.astype(mx_dtype))
            kv = _dot(k_end_t[h * dk:(h + 1) * dk, :], vh)
            s_scr[seq, h] = decay[h * dk:(h + 1) * dk, :] * s_old + kv
            rh = r[:, h * dv:(h + 1) * dv]
            og_ref[pl.ds(r0, chunk), h * dv:(h + 1) * dv] = (_rms(o, nw) * (rh * _sigmoid(rh))).astype(og_ref.dtype)
        return carry

    lax.fori_loop(0, n_seq * n_chunks, body, 0)

    @pl.when(j == pl.num_programs(1) - 1)
    def _():
        sout_ref[...] = s_scr[...]


def _gla(gin, loga, s0, nw, *, n_outer, n_steps, n_seq, n_chunks, chunk, mx_dtype, out_dtype):
    t = gin.shape[0]
    _, heads, dk, dv = s0.shape
    rows = n_seq * n_chunks * chunk
    assert t == n_outer * n_steps * rows
    kern = functools.partial(_gla_kernel, chunk=chunk, n_chunks=n_chunks, n_seq=n_seq, heads=heads,
                             dk=dk, dv=dv, mx_dtype=mx_dtype)
    tok = lambda i, j: (i * n_steps + j, 0)
    st = lambda i, j: (i, 0, 0, 0)
    return pl.pallas_call(
        kern,
        grid=(n_outer, n_steps),
        in_specs=[
            pl.BlockSpec((rows, gin.shape[1]), tok),
            pl.BlockSpec((rows, loga.shape[1]), tok),
            pl.BlockSpec((n_seq, heads, dk, dv), st),
            pl.BlockSpec((1, dv), lambda i, j: (0, 0)),
        ],
        out_specs=(
            pl.BlockSpec((rows, heads * dv), tok),
            pl.BlockSpec((n_seq, heads, dk, dv), st),
        ),
        out_shape=(
            jax.ShapeDtypeStruct((t, heads * dv), out_dtype),
            jax.ShapeDtypeStruct(s0.shape, F32),
        ),
        scratch_shapes=[pltpu.VMEM((n_seq, heads, dk, dv), F32)],
        compiler_params=_cparams(("parallel", "arbitrary")),
        name="gla",
    )(gin, loga, s0, nw)


def _fox_prep_kernel(qf_ref, kf_ref, vf_ref, lft_ref, rq_ref, rk_ref, qat_ref, ka_ref, vt_ref, carry_scr, *,
                     heads, dh):
    j = pl.program_id(1)
    tp = qf_ref.shape[0]

    @pl.when(j == 0)
    def _():
        carry_scr[...] = jnp.zeros_like(carry_scr)

    rio = lax.broadcasted_iota(jnp.int32, (tp, tp), 0)
    cio = lax.broadcasted_iota(jnp.int32, (tp, tp), 1)
    upper = jnp.where(rio <= cio, 1.0, 0.0).astype(BF16)
    l1, l2, l3 = _split3(lft_ref[...])
    cs = _dot(jnp.concatenate([l1, l2, l3, jnp.zeros_like(l1)], axis=0), upper)
    c_t = cs[0:heads] + cs[heads:2 * heads] + cs[2 * heads:3 * heads] + carry_scr[:, 0:1]
    carry_scr[...] = jnp.broadcast_to(c_t[:, tp - 1:tp], carry_scr.shape)
    c1, c2, c3 = _split3(c_t)
    pad_rows = LANES - 3 * heads - 8
    aug_t = jnp.concatenate([c1.astype(F32), c2.astype(F32), c3.astype(F32),
                             jnp.ones((8, tp), F32), jnp.zeros((pad_rows, tp), F32)], axis=0)
    aug = aug_t.T.astype(BF16)
    lq_t = jnp.concatenate([(qf_ref[...].T * dh ** -0.5).astype(BF16), aug_t.astype(BF16)], axis=0)
    lk = jnp.concatenate([kf_ref[...].astype(BF16), aug], axis=1)
    qa_t = _dot(rq_ref[...], lq_t).astype(BF16)
    ka = _dot(lk, rk_ref[...]).astype(BF16)
    for h in range(heads):
        qat_ref[h] = qa_t[h * LANES:(h + 1) * LANES, :]
        ka_ref[h] = ka[:, h * LANES:(h + 1) * LANES]
    v_t = vf_ref[...].T.astype(BF16)
    for p in range(heads * dh // LANES):
        vt_ref[p] = v_t[p * LANES:(p + 1) * LANES, :]


def _placement_matrices(heads, dh):
    kin = heads * dh + LANES
    rq = np.zeros((kin, heads * LANES), np.float32)
    rk = np.zeros((kin, heads * LANES), np.float32)
    base = heads * dh
    ones_row = base + 3 * heads
    for h in range(heads):
        for d in range(dh):
            rq[h * dh + d, h * LANES + d] = 1.0
            rk[h * dh + d, h * LANES + d] = 1.0
        for p in range(3):
            rq[base + p * heads + h, h * LANES + dh + p] = 1.0
            rq[ones_row, h * LANES + dh + 3 + p] = 1.0
            rk[ones_row, h * LANES + dh + p] = 1.0
            rk[base + p * heads + h, h * LANES + dh + 3 + p] = -1.0
    return jnp.asarray(rq.T, BF16), jnp.asarray(rk, BF16)


def _fox_prep(qf, kf, vf, lft, batch, seq, heads, dh, tp):
    rq, rk = _placement_matrices(heads, dh)
    n_steps = seq // tp
    pairs = heads * dh // LANES
    tok = lambda b, j: (b * n_steps + j, 0)
    const = lambda b, j: (0, 0)
    return pl.pallas_call(
        functools.partial(_fox_prep_kernel, heads=heads, dh=dh),
        grid=(batch, n_steps),
        in_specs=[
            pl.BlockSpec((tp, heads * dh), tok),
            pl.BlockSpec((tp, heads * dh), tok),
            pl.BlockSpec((tp, heads * dh), tok),
            pl.BlockSpec((heads, tp), lambda b, j: (0, b * n_steps + j)),
            pl.BlockSpec(rq.shape, const),
            pl.BlockSpec(rk.shape, const),
        ],
        out_specs=(
            pl.BlockSpec((None, heads, LANES, tp), lambda b, j: (b, 0, 0, j)),
            pl.BlockSpec((None, heads, tp, LANES), lambda b, j: (b, 0, j, 0)),
            pl.BlockSpec((None, pairs, LANES, tp), lambda b, j: (b, 0, 0, j)),
        ),
        out_shape=(
            jax.ShapeDtypeStruct((batch, heads, LANES, seq), BF16),
            jax.ShapeDtypeStruct((batch, heads, seq, LANES), BF16),
            jax.ShapeDtypeStruct((batch, pairs, LANES, seq), BF16),
        ),
        scratch_shapes=[pltpu.VMEM((heads, LANES), F32)],
        compiler_params=_cparams(("parallel", "arbitrary")),
        name="fox_prep",
    )(qf, kf, vf, lft, rq, rk)


def _fox_flash_kernel(qi_ref, ki_ref, qat_ref, ka_ref, vt_ref, o_ref, m_scr, l_scr, acc_scr, *, dh, hps):
    step = pl.program_id(2)
    qi = qi_ref[step]
    ki = ki_ref[step]
    tq = qat_ref.shape[2]
    tk = ka_ref.shape[1]
    hpl = LANES // dh

    @pl.when(ki == 0)
    def _():
        m_scr[...] = jnp.full_like(m_scr, NEG)
        l_scr[...] = jnp.zeros_like(l_scr)
        acc_scr[...] = jnp.zeros_like(acc_scr)

    def update(masked):
        for h in range(hps):
            s = _dot(ka_ref[h], qat_ref[h])
            if masked:
                rio = lax.broadcasted_iota(jnp.int32, (tk, tq), 0)
                cio = lax.broadcasted_iota(jnp.int32, (tk, tq), 1)
                s = jnp.where(rio <= cio, s, NEG)
            m_prev = m_scr[h]
            m_new = jnp.maximum(m_prev, jnp.max(s, axis=0, keepdims=True))
            a = jnp.exp(m_prev - m_new)
            p = jnp.exp(s - m_new)
            l_scr[h] = a * l_scr[h] + jnp.sum(p, axis=0, keepdims=True)
            v_t = vt_ref[h // hpl, (h % hpl) * dh:(h % hpl + 1) * dh, :]
            acc_scr[h] = a * acc_scr[h] + _dot(v_t, p.astype(BF16))
            m_scr[h] = m_new

    @pl.when(ki < qi)
    def _():
        update(False)

    @pl.when(ki == qi)
    def _():
        update(True)
        o_t = jnp.concatenate([acc_scr[h] / l_scr[h] for h in range(hps)], axis=0)
        o_ref[...] = o_t.T.astype(o_ref.dtype)


def _fox_flash(qat, ka, vt, batch, seq, heads, dh, tq, hps):
    nq = seq // tq
    qi_l, ki_l = [], []
    for q in range(nq):
        for k in range(q + 1):
            qi_l.append(q)
            ki_l.append(k)
    qi_a = jnp.asarray(np.array(qi_l, np.int32))
    ki_a = jnp.asarray(np.array(ki_l, np.int32))
    n_tri = len(qi_l)
    hpl = LANES // dh
    grid_spec = pltpu.PrefetchScalarGridSpec(
        num_scalar_prefetch=2,
        grid=(batch, heads // hps, n_tri),
        in_specs=[
            pl.BlockSpec((None, hps, LANES, tq), lambda b, g, s, qi, ki: (b, g, 0, qi[s])),
            pl.BlockSpec((None, hps, tq, LANES), lambda b, g, s, qi, ki: (b, g, ki[s], 0)),
            pl.BlockSpec((None, hps // hpl, LANES, tq), lambda b, g, s, qi, ki: (b, g, 0, ki[s])),
        ],
        out_specs=pl.BlockSpec((tq, hps * dh), lambda b, g, s, qi, ki: (b * nq + qi[s], g)),
        scratch_shapes=[
            pltpu.VMEM((hps, 1, tq), F32),
            pltpu.VMEM((hps, 1, tq), F32),
            pltpu.VMEM((hps, dh, tq), F32),
        ],
    )
    return pl.pallas_call(
        functools.partial(_fox_flash_kernel, dh=dh, hps=hps),
        grid_spec=grid_spec,
        out_shape=jax.ShapeDtypeStruct((batch * seq, heads * dh), BF16),
        compiler_params=_cparams(("parallel", "parallel", "arbitrary")),
        name="fox_flash",
    )(qi_a, ki_a, qat, ka, vt)


def _fox_paged_kernel(pt_ref, q_ref, kn_ref, vn_ref, lfnt_ref, *refs, heads, dh, n_new, ppb, page):
    k_refs = refs[0:ppb]
    v_refs = refs[ppb:2 * ppb]
    lf_refs = refs[2 * ppb:3 * ppb]
    o_ref = refs[3 * ppb]
    qbd_scr, cncol_scr, cns_scr, base_scr, m_scr, l_scr, acc_scr = refs[3 * ppb + 1:]
    j = pl.program_id(1)
    rows = heads * n_new
    width = heads * dh

    rr = lax.broadcasted_iota(jnp.int32, (rows, width), 0)
    cc = lax.broadcasted_iota(jnp.int32, (rows, width), 1)
    bd_mask = (rr // n_new) == (cc // dh)

    @pl.when(j == 0)
    def _():
        q = q_ref[...] * dh ** -0.5
        q_rep = jnp.broadcast_to(q[None], (heads, n_new, width)).reshape(rows, width)
        qbd_scr[...] = jnp.where(bd_mask, q_rep, 0.0).astype(BF16)
        lfnt = lfnt_ref[...]
        lane = lax.broadcasted_iota(jnp.int32, lfnt.shape, 1)
        cnt = jnp.zeros_like(lfnt)
        for s in range(n_new):
            col = jnp.sum(jnp.where(lane <= s, lfnt, 0.0), axis=1, keepdims=True)
            cnt = cnt + jnp.where(lane == s, col, 0.0)
        cns = jnp.broadcast_to(cnt[:, None, :], (heads, n_new, LANES)).reshape(rows, LANES)
        cns_scr[...] = cns
        r2 = lax.broadcasted_iota(jnp.int32, (rows, LANES), 0)
        l2 = lax.broadcasted_iota(jnp.int32, (rows, LANES), 1)
        cncol_scr[...] = jnp.sum(jnp.where(l2 == r2 % n_new, cns, 0.0), axis=1, keepdims=True)
        base_scr[...] = jnp.zeros_like(base_scr)
        m_scr[...] = jnp.full_like(m_scr, NEG)
        l_scr[...] = jnp.zeros_like(l_scr)
        acc_scr[...] = jnp.zeros_like(acc_scr)

    rio = lax.broadcasted_iota(jnp.int32, (page, page), 0)
    cio = lax.broadcasted_iota(jnp.int32, (page, page), 1)
    after = jnp.where(rio > cio, 1.0, 0.0).astype(BF16)
    lf_all = jnp.concatenate([lf_refs[i][...] for i in range(ppb)], axis=0)
    ss = _dot(jnp.concatenate(_split3(lf_all), axis=0), after)
    nr = ppb * heads
    suf_all = ss[0:nr] + ss[nr:2 * nr] + ss[2 * nr:3 * nr]
    tot_all = suf_all[:, 0:1] + lf_all[:, 0:1]
    base = base_scr[:, 0:1]
    biases = []
    for i in range(ppb):
        biases.append(base + suf_all[i * heads:(i + 1) * heads])
        base = base + tot_all[i * heads:(i + 1) * heads]
    base_scr[...] = jnp.broadcast_to(base, base_scr.shape)
    bias = jnp.concatenate(biases, axis=1)
    bias_rows = jnp.broadcast_to(bias[:, None, :], (heads, n_new, ppb * page)).reshape(rows, ppb * page)

    kt = jnp.concatenate([k_refs[i][...].astype(BF16) for i in range(ppb)], axis=1)
    vt = jnp.concatenate([v_refs[i][...].astype(BF16) for i in range(ppb)], axis=1)
    qbd = qbd_scr[...]
    s = _dot(qbd, kt) + cncol_scr[...] + bias_rows
    m_prev = m_scr[...]
    m_new = jnp.maximum(m_prev, jnp.max(s, axis=1, keepdims=True))
    a = jnp.exp(m_prev - m_new)
    p = jnp.exp(s - m_new)
    l_scr[...] = a * l_scr[...] + jnp.sum(p, axis=1, keepdims=True)
    acc_scr[...] = a * acc_scr[...] + _dot_nt(p.astype(BF16), vt)
    m_scr[...] = m_new

    @pl.when(j == pl.num_programs(1) - 1)
    def _():
        pad = jnp.zeros((LANES - n_new, width), F32)
        kn = jnp.concatenate([kn_ref[...], pad], axis=0).astype(BF16)
        vn = jnp.concatenate([vn_ref[...], pad], axis=0).astype(BF16)
        r2 = lax.broadcasted_iota(jnp.int32, (rows, LANES), 0)
        l2 = lax.broadcasted_iota(jnp.int32, (rows, LANES), 1)
        sn = _dot_nt(qbd, kn) + cncol_scr[...] - cns_scr[...]
        sn = jnp.where(l2 <= r2 % n_new, sn, NEG)
        m_p = m_scr[...]
        m_n = jnp.maximum(m_p, jnp.max(sn, axis=1, keepdims=True))
        a2 = jnp.exp(m_p - m_n)
        pn = jnp.exp(sn - m_n)
        l_f = a2 * l_scr[...] + jnp.sum(pn, axis=1, keepdims=True)
        acc_f = a2 * acc_scr[...] + _dot(pn.astype(BF16), vn)
        o = jnp.where(bd_mask, acc_f / l_f, 0.0)
        o_ref[...] = jnp.sum(o.reshape(heads, n_new, width), axis=0)


def _fox_paged(pt_flat, qf, kf, vf, lfnt_pad, ck, cv, clf, *, n_seq, n_new, n_pages, heads, dh, ppb):
    page = ck.shape[-1]
    width = heads * dh
    rows = heads * n_new
    n_steps = n_pages // ppb

    def page_map(i):
        return lambda b, j, pt: (pt[b * n_pages + (n_pages - 1 - (j * ppb + i))], 0, 0)

    tok = pl.BlockSpec((n_new, width), lambda b, j, pt: (b, 0))
    in_specs = [tok, tok, tok, pl.BlockSpec((None, heads, LANES), lambda b, j, pt: (b, 0, 0))]
    in_specs += [pl.BlockSpec((None, width, page), page_map(i)) for i in range(ppb)]
    in_specs += [pl.BlockSpec((None, width, page), page_map(i)) for i in range(ppb)]
    in_specs += [pl.BlockSpec((None, heads, page), page_map(i)) for i in range(ppb)]
    grid_spec = pltpu.PrefetchScalarGridSpec(
        num_scalar_prefetch=1,
        grid=(n_seq, n_steps),
        in_specs=in_specs,
        out_specs=pl.BlockSpec((n_new, width), lambda b, j, pt: (b, 0)),
        scratch_shapes=[
            pltpu.VMEM((rows, width), BF16),
            pltpu.VMEM((rows, 1), F32),
            pltpu.VMEM((rows, LANES), F32),
            pltpu.VMEM((heads, LANES), F32),
            pltpu.VMEM((rows, 1), F32),
            pltpu.VMEM((rows, 1), F32),
            pltpu.VMEM((rows, width), F32),
        ],
    )
    kern = functools.partial(_fox_paged_kernel, heads=heads, dh=dh, n_new=n_new, ppb=ppb, page=page)
    return pl.pallas_call(
        kern,
        grid_spec=grid_spec,
        out_shape=jax.ShapeDtypeStruct((n_seq * n_new, width), F32),
        compiler_params=_cparams(("parallel", "arbitrary")),
        name="fox_paged",
    )(pt_flat, qf, kf, vf, lfnt_pad, *([ck] * ppb), *([cv] * ppb), *([clf] * ppb))


def _out_proj_kernel(x_ref, og_ref, of_ref, sga_ref, sgb_ref, wug_ref, wuf_ref, wo_ref, nfw_ref, wr_ref, br_ref,
                     h_ref, xn_ref, gates_ref, *, n_experts, n_groups):
    up_a = _dot(og_ref[...].astype(BF16), wug_ref[...])
    up_b = _dot(of_ref[...].astype(BF16), wuf_ref[...])
    merged = sga_ref[...] * up_a + sgb_ref[...] * up_b
    h = x_ref[...] + _dot(merged.astype(BF16), wo_ref[...])
    h_ref[...] = h
    xn = _rms(h, nfw_ref[...])
    xn_ref[...] = xn.astype(BF16)

    logits = jnp.dot(xn, wr_ref[...], preferred_element_type=F32, precision=lax.Precision.HIGHEST) + br_ref[...]
    lane = lax.broadcasted_iota(jnp.int32, logits.shape, 1).astype(F32)
    epg = n_experts // n_groups
    big = 4.0 * LANES
    is_g = (lane >= n_experts) & (lane < n_experts + n_groups)
    gl = jnp.where(is_g, logits, NEG)
    gmax = jnp.max(gl, axis=1, keepdims=True)
    gsum = jnp.sum(jnp.exp(gl - gmax), axis=1, keepdims=True)
    p_g = 1.0 / gsum
    g_sel = jnp.min(jnp.where(gl == gmax, lane, big), axis=1, keepdims=True) - n_experts
    e_lo = g_sel * epg
    in_grp = (lane >= e_lo) & (lane < e_lo + epg)
    el = jnp.where(in_grp, logits, NEG)
    emax = jnp.max(el, axis=1, keepdims=True)
    ee = jnp.exp(el - emax)
    pe = ee / jnp.sum(ee, axis=1, keepdims=True)
    cand = jnp.where(in_grp, pe, -1.0)
    v1 = jnp.max(cand, axis=1, keepdims=True)
    i1 = jnp.min(jnp.where(cand == v1, lane, big), axis=1, keepdims=True)
    cand2 = jnp.where(lane == i1, -1.0, cand)
    v2 = jnp.max(cand2, axis=1, keepdims=True)
    i2 = jnp.min(jnp.where(cand2 == v2, lane, big), axis=1, keepdims=True)
    tot = v1 + v2
    gates_ref[...] = jnp.where(lane == i1, p_g * (v1 / tot), 0.0) + jnp.where(lane == i2, p_g * (v2 / tot), 0.0)


def _out_proj(x2, og, of, sga, sgb, wug, wuf, wo, nfw, wr, br, *, n_experts, n_groups, tm):
    t, d_model = x2.shape
    row = lambda i: (i, 0)
    const = lambda i: (0, 0)
    return pl.pallas_call(
        functools.partial(_out_proj_kernel, n_experts=n_experts, n_groups=n_groups),
        grid=(t // tm,),
        in_specs=[
            pl.BlockSpec((tm, d_model), row),
            pl.BlockSpec((tm, og.shape[1]), row),
            pl.BlockSpec((tm, of.shape[1]), row),
            pl.BlockSpec((tm, d_model), row),
            pl.BlockSpec((tm, d_model), row),
            pl.BlockSpec(wug.shape, const),
            pl.BlockSpec(wuf.shape, const),
            pl.BlockSpec(wo.shape, const),
            pl.BlockSpec((1, d_model), const),
            pl.BlockSpec(wr.shape, const),
            pl.BlockSpec((1, LANES), const),
        ],
        out_specs=(
            pl.BlockSpec((tm, d_model), row),
            pl.BlockSpec((tm, d_model), row),
            pl.BlockSpec((tm, LANES), row),
        ),
        out_shape=(
            jax.ShapeDtypeStruct((t, d_model), F32),
            jax.ShapeDtypeStruct((t, d_model), BF16),
            jax.ShapeDtypeStruct((t, LANES), F32),
        ),
        compiler_params=_cparams(("parallel",)),
        name="out_proj",
    )(x2, og, of, sga, sgb, wug, wuf, wo, nfw, wr, br)


def _moe_kernel(xn_ref, gates_ref, h_ref, wg_ref, wu_ref, wd_ref, fnw_ref, y_ref, acc_scr):
    e = pl.program_id(1)

    @pl.when(e == 0)
    def _():
        acc_scr[...] = jnp.zeros_like(acc_scr)

    x = xn_ref[...]
    g = _dot(x, wg_ref[...])
    u = _dot(x, wu_ref[...])
    hid = (g * _sigmoid(g)) * u
    gates = gates_ref[...]
    lane = lax.broadcasted_iota(jnp.int32, gates.shape, 1)
    gate_e = jnp.sum(jnp.where(lane == e, gates, 0.0), axis=1, keepdims=True)
    acc_scr[...] += gate_e * _dot(hid.astype(BF16), wd_ref[...])

    @pl.when(e == pl.num_programs(1) - 1)
    def _():
        y_ref[...] = _rms(h_ref[...] + acc_scr[...], fnw_ref[...])


def _moe(xn, gates, h, wg, wu, wd, fnw, *, tm):
    t, d_model = h.shape
    n_experts, _, d_exp = wg.shape
    row = lambda i, e: (i, 0)
    return pl.pallas_call(
        _moe_kernel,
        grid=(t // tm, n_experts),
        in_specs=[
            pl.BlockSpec((tm, d_model), row),
            pl.BlockSpec((tm, LANES), row),
            pl.BlockSpec((tm, d_model), row),
            pl.BlockSpec((None, d_model, d_exp), lambda i, e: (e, 0, 0)),
            pl.BlockSpec((None, d_model, d_exp), lambda i, e: (e, 0, 0)),
            pl.BlockSpec((None, d_exp, d_model), lambda i, e: (e, 0, 0)),
            pl.BlockSpec((1, d_model), lambda i, e: (0, 0)),
        ],
        out_specs=pl.BlockSpec((tm, d_model), row),
        out_shape=jax.ShapeDtypeStruct((t, d_model), F32),
        scratch_shapes=[pltpu.VMEM((tm, d_model), F32)],
        compiler_params=_cparams(("parallel", "arbitrary")),
        name="moe",
    )(xn, gates, h, wg, wu, wd, fnw)


def _pick_tile(n, pref):
    t = min(n, pref)
    while n % t:
        t //= 2
    return t


def kernel(x_prompt, x_sample, state_gla, cache_k, cache_v, cache_logf, page_table, norm_mix_w, w_in, w_gla_a2,
           b_gla_a2, gla_norm_w, b_fox_f, w_up_gla, w_up_fox, w_out, norm_ffn_w, w_router_group, b_router_group,
           w_router_expert, b_router_expert, w_exp_gate, w_exp_up, w_exp_down, final_norm_w):
    depth = w_in.shape[0]
    assert depth == 1
    batch, seq, d_model = x_prompt.shape
    n_seq, n_new, _ = x_sample.shape
    _, _, g_heads, dk, dv = state_gla.shape
    _, n_phys, page, f_heads, dh = cache_k.shape
    n_pages = page_table.shape[1]
    n_lr = w_gla_a2.shape[1]
    n_groups = w_router_group.shape[2]
    n_experts = w_router_expert.shape[2]
    qk = g_heads * dk
    vw = g_heads * dv
    fw = f_heads * dh
    n_gla = 2 * qk + 2 * vw
    n_fox = 3 * fw
    dims = (n_gla, n_fox, d_model, n_lr, f_heads)

    wi = w_in[0]
    o_za = n_gla
    o_fox = o_za + n_lr
    o_fp = o_fox + n_fox
    o_gate = o_fp + f_heads
    w_small = jnp.concatenate([wi[:, o_za:o_fox], wi[:, o_fp:o_gate],
                               jnp.zeros((d_model, LANES - n_lr - f_heads), wi.dtype)], axis=1)
    w_all = jnp.concatenate([wi[:, :o_za], wi[:, o_fox:o_fp], wi[:, o_gate:], w_small], axis=1).astype(BF16)
    w2p = jnp.concatenate([w_gla_a2[0], jnp.zeros((LANES - n_lr, qk), F32)], axis=0).astype(BF16)
    b2 = b_gla_a2[0].reshape(1, qk)
    bfc = b_fox_f[0].reshape(f_heads, 1)
    nw_mix = norm_mix_w[0].reshape(1, d_model)
    nw_gla = gla_norm_w[0].reshape(1, dv)
    wug = w_up_gla[0].astype(BF16)
    wuf = w_up_fox[0].astype(BF16)
    wo = w_out[0].astype(BF16)
    nfw = norm_ffn_w[0].reshape(1, d_model)
    wr = jnp.concatenate([w_router_expert[0], w_router_group[0],
                          jnp.zeros((d_model, LANES - n_experts - n_groups), F32)], axis=1)
    br = jnp.concatenate([b_router_expert[0], b_router_group[0],
                          jnp.zeros((LANES - n_experts - n_groups,), F32)]).reshape(1, LANES)
    wg = w_exp_gate[0].astype(BF16)
    wu = w_exp_up[0].astype(BF16)
    wd = w_exp_down[0].astype(BF16)
    fnw = final_norm_w.reshape(1, d_model)

    def token_path(x2, mixers):
        tm = _pick_tile(x2.shape[0], 256)
        gin, loga, qf, kf, vf, sga, sgb, lft = _in_proj(x2, nw_mix, w_all, w2p, b2, bfc, dims, tm)
        og, s_new, of = mixers(gin, loga, qf, kf, vf, lft)
        h, xn, gates = _out_proj(x2, og, of, sga, sgb, wug, wuf, wo, nfw, wr, br,
                                 n_experts=n_experts, n_groups=n_groups, tm=tm)
        y = _moe(xn, gates, h, wg, wu, wd, fnw, tm=_pick_tile(x2.shape[0], 1024))
        return y, s_new, kf, vf, lft.T

    def prompt_mixers(gin, loga, qf, kf, vf, lft):
        chunk = math.gcd(seq, GLA_CHUNK)
        n_chunks = _pick_tile(seq // chunk, 8)
        s0 = jnp.zeros((batch, g_heads, dk, dv), F32)
        og, s_new = _gla(gin, loga, s0, nw_gla, n_outer=batch, n_steps=seq // (chunk * n_chunks), n_seq=1,
                         n_chunks=n_chunks, chunk=chunk, mx_dtype=BF16, out_dtype=BF16)
        tp = _pick_tile(seq, 512)
        qat, ka, vt = _fox_prep(qf, kf, vf, lft, batch, seq, f_heads, dh, tp)
        of = _fox_flash(qat, ka, vt, batch, seq, f_heads, dh, tp, hps=FLASH_HEADS_PER_STEP)
        return og, s_new, of

    y_p, s_p, k_p, v_p, lf_p = token_path(x_prompt.reshape(batch * seq, d_model), prompt_mixers)

    ck = jnp.transpose(cache_k[0], (0, 2, 3, 1)).reshape(n_phys, fw, page)
    cv = jnp.transpose(cache_v[0], (0, 2, 3, 1)).reshape(n_phys, fw, page)
    clf = jnp.transpose(cache_logf[0], (0, 2, 1))
    pt_flat = page_table.reshape(-1).astype(jnp.int32)

    def sample_mixers(gin, loga, qf, kf, vf, lft):
        chunk = math.gcd(n_new, GLA_CHUNK)
        assert chunk == n_new
        gs = _pick_tile(n_seq, 8)
        og, s_new = _gla(gin, loga, state_gla[0], nw_gla, n_outer=n_seq // gs, n_steps=1, n_seq=gs,
                         n_chunks=1, chunk=chunk, mx_dtype=F32, out_dtype=F32)
        lfnt = jnp.transpose(lft.reshape(f_heads, n_seq, n_new), (1, 0, 2))
        lfnt_pad = jnp.pad(lfnt, ((0, 0), (0, 0), (0, LANES - n_new)))
        of = _fox_paged(pt_flat, qf, kf, vf, lfnt_pad, ck, cv, clf, n_seq=n_seq, n_new=n_new,
                        n_pages=n_pages, heads=f_heads, dh=dh, ppb=_pick_tile(n_pages, PAGES_PER_STEP))
        return og, s_new, of

    y_s, s_s, k_s, v_s, lf_s = token_path(x_sample.reshape(n_seq * n_new, d_model), sample_mixers)

    return (y_p.reshape(batch, seq, d_model),
            y_s.reshape(n_seq, n_new, d_model),
            s_p[None],
            s_s[None],
            k_p.reshape(1, batch, seq, f_heads, dh),
            v_p.reshape(1, batch, seq, f_heads, dh),
            lf_p.reshape(1, batch, seq, f_heads),
            k_s.reshape(1, n_seq, n_new, f_heads, dh),
            v_s.reshape(1, n_seq, n_new, f_heads, dh),
            lf_s.reshape(1, n_seq, n_new, f_heads))
```

```python
import functools
import math

import jax
import jax.numpy as jnp
import numpy as np
from jax import lax
from jax.experimental import pallas as pl
from jax.experimental.pallas import tpu as pltpu

F32 = jnp.float32
BF16 = jnp.bfloat16

RMS_EPS = 1e-6
GLA_TAU = 16.0
GLA_CHUNK = 64
FLASH_HEADS_PER_STEP = 8
PAGES_PER_STEP = 32
NEG = -1e30

V7X_VMEM_LIMIT_BYTES = 56 * 1024 * 1024
LANES = 128


def _cparams(sem):
    return pltpu.CompilerParams(dimension_semantics=sem, vmem_limit_bytes=V7X_VMEM_LIMIT_BYTES)


def _sigmoid(x):
    return 1.0 / (1.0 + jnp.exp(-x))


def _log_sigmoid(x):
    return jnp.minimum(x, 0.0) - jnp.log(1.0 + jnp.exp(-jnp.abs(x)))


def _rms(x, w):
    ms = jnp.mean(x * x, axis=-1, keepdims=True)
    return x * lax.rsqrt(ms + RMS_EPS) * w


def _split3(x):
    p1 = x.astype(BF16)
    r1 = x - p1.astype(F32)
    p2 = r1.astype(BF16)
    p3 = (r1 - p2.astype(F32)).astype(BF16)
    return p1, p2, p3


def _dot(a, b):
    return jnp.dot(a, b, preferred_element_type=F32)


def _dot_nt(a, b):
    return lax.dot_general(a, b, (((1,), (1,)), ((), ())), preferred_element_type=F32)


def _in_proj_kernel(x_ref, nw_ref, w_ref, w2_ref, b2_ref, bf_ref,
                    gla_ref, loga_ref, qf_ref, kf_ref, vf_ref, sga_ref, sgb_ref, lft_ref, *, dims):
    n_gla, n_fox, d_model, n_lr, n_fh = dims
    xn = _rms(x_ref[...], nw_ref[...]).astype(BF16)

    def mm(lo, hi):
        return _dot(xn, w_ref[:, lo:hi])

    o = 0
    gla_ref[...] = mm(o, o + n_gla)
    o += n_gla
    fw = n_fox // 3
    qf_ref[...] = mm(o, o + fw)
    kf_ref[...] = mm(o + fw, o + 2 * fw)
    vf_ref[...] = mm(o + 2 * fw, o + 3 * fw)
    o += n_fox
    sga_ref[...] = _sigmoid(mm(o, o + d_model))
    sgb_ref[...] = _sigmoid(mm(o + d_model, o + 2 * d_model))
    o += 2 * d_model
    small = mm(o, o + LANES)
    pre = _dot(small.astype(BF16), w2_ref[...]) + b2_ref[...]
    loga_ref[...] = _log_sigmoid(pre) * (1.0 / GLA_TAU)
    small_t = small.T
    lft_ref[...] = _log_sigmoid(small_t[n_lr:n_lr + n_fh, :] + bf_ref[...])


def _in_proj(x2, nw, w_all, w2p, b2, bfc, dims, tm):
    n_gla, n_fox, d_model, n_lr, n_fh = dims
    t = x2.shape[0]
    fw = n_fox // 3
    n_qk = w2p.shape[1]
    row = lambda i: (i, 0)
    const = lambda i: (0, 0)
    out_shape = (
        jax.ShapeDtypeStruct((t, n_gla), F32),
        jax.ShapeDtypeStruct((t, n_qk), F32),
        jax.ShapeDtypeStruct((t, fw), F32),
        jax.ShapeDtypeStruct((t, fw), F32),
        jax.ShapeDtypeStruct((t, fw), F32),
        jax.ShapeDtypeStruct((t, d_model), F32),
        jax.ShapeDtypeStruct((t, d_model), F32),
        jax.ShapeDtypeStruct((n_fh, t), F32),
    )
    return pl.pallas_call(
        functools.partial(_in_proj_kernel, dims=dims),
        grid=(t // tm,),
        in_specs=[
            pl.BlockSpec((tm, d_model), row),
            pl.BlockSpec((1, d_model), const),
            pl.BlockSpec(w_all.shape, const, pipeline_mode=pl.Buffered(1)),
            pl.BlockSpec(w2p.shape, const),
            pl.BlockSpec((1, n_qk), const),
            pl.BlockSpec((n_fh, 1), const),
        ],
        out_specs=(
            pl.BlockSpec((tm, n_gla), row),
            pl.BlockSpec((tm, n_qk), row),
            pl.BlockSpec((tm, fw), row),
            pl.BlockSpec((tm, fw), row),
            pl.BlockSpec((tm, fw), row),
            pl.BlockSpec((tm, d_model), row),
            pl.BlockSpec((tm, d_model), row),
            pl.BlockSpec((n_fh, tm), lambda i: (0, i)),
        ),
        out_shape=out_shape,
        compiler_params=_cparams(("parallel",)),
        name="in_proj",
    )(x2, nw, w_all, w2p, b2, bfc)


def _gla_kernel(gin_ref, loga_ref, s0_ref, nw_ref, og_ref, sout_ref, s_scr, *,
                chunk, n_chunks, n_seq, heads, dk, dv, mx_dtype):
    j = pl.program_id(1)
    qk = heads * dk
    vw = heads * dv
    kpad = max(chunk, LANES)

    @pl.when(j == 0)
    def _():
        s_scr[...] = s0_ref[...]

    rio = lax.broadcasted_iota(jnp.int32, (chunk, kpad), 0)
    cio = lax.broadcasted_iota(jnp.int32, (chunk, kpad), 1)
    tril = cio <= rio
    tril_m = jnp.where(tril, 1.0, 0.0).astype(mx_dtype)
    nw = nw_ref[...]

    def pad_rows(a):
        if kpad == chunk:
            return a
        return jnp.concatenate([a, jnp.zeros((kpad - chunk, a.shape[1]), a.dtype)], axis=0)

    def body(idx, carry):
        seq = idx // n_chunks
        r0 = pl.multiple_of(idx * chunk, chunk)
        g = loga_ref[pl.ds(r0, chunk), :]
        gs = pad_rows(jnp.concatenate([p.astype(mx_dtype) for p in _split3(g)], axis=1))
        bs = _dot(tril_m, gs)
        b = bs[:, :qk] + bs[:, qk:2 * qk] + bs[:, 2 * qk:]
        b_last = b[chunk - 1:chunk, :]
        q = gin_ref[pl.ds(r0, chunk), 0:qk]
        k = gin_ref[pl.ds(r0, chunk), qk:2 * qk]
        v = gin_ref[pl.ds(r0, chunk), 2 * qk:2 * qk + vw]
        r = gin_ref[pl.ds(r0, chunk), 2 * qk + vw:2 * qk + 2 * vw]
        q_i = (q * jnp.exp(b) * dk ** -0.5).astype(mx_dtype)
        k_i = pad_rows(k * jnp.exp(-b)).astype(mx_dtype)
        k_end_t = pad_rows(k * jnp.exp(b_last - b)).T.astype(mx_dtype)
        decay = jnp.exp(pad_rows(b).T[:, chunk - 1:chunk])
        vb = pad_rows(v).astype(mx_dtype)
        for h in range(heads):
            qh = q_i[:, h * dk:(h + 1) * dk]
            kh = k_i[:, h * dk:(h + 1) * dk]
            vh = vb[:, h * dv:(h + 1) * dv]
            a = jnp.where(tril, _dot_nt(qh, kh), 0.0).astype(mx_dtype)
            s_old = s_scr[seq, h]
            o = _dot(a, vh) + _dot(qh, s_old.astype(mx_dtype))
            kv = _dot(k_end_t[h * dk:(h + 1) * dk, :], vh)
            s_scr[seq, h] = decay[h * dk:(h + 1) * dk, :] * s_old + kv
            rh = r[:, h * dv:(h + 1) * dv]
            og_ref[pl.ds(r0, chunk), h * dv:(h + 1) * dv] = (_rms(o, nw) * (rh * _sigmoid(rh))).astype(og_ref.dtype)
        return carry

    lax.fori_loop(0, n_seq * n_chunks, body, 0, unroll=2)

    @pl.when(j == pl.num_programs(1) - 1)
    def _():
        sout_ref[...] = s_scr[...]


def _gla(gin, loga, s0, nw, *, n_outer, n_steps, n_seq, n_chunks, chunk, mx_dtype, out_dtype):
    t = gin.shape[0]
    _, heads, dk, dv = s0.shape
    rows = n_seq * n_chunks * chunk
    assert t == n_outer * n_steps * rows
    kern = functools.partial(_gla_kernel, chunk=chunk, n_chunks=n_chunks, n_seq=n_seq, heads=heads,
                             dk=dk, dv=dv, mx_dtype=mx_dtype)
    tok = lambda i, j: (i * n_steps + j, 0)
    st = lambda i, j: (i, 0, 0, 0)
    return pl.pallas_call(
        kern,
        grid=(n_outer, n_steps),
        in_specs=[
            pl.BlockSpec((rows, gin.shape[1]), tok),
            pl.BlockSpec((rows, loga.shape[1]), tok),
            pl.BlockSpec((n_seq, heads, dk, dv), st),
            pl.BlockSpec((1, dv), lambda i, j: (0, 0)),
        ],
        out_specs=(
            pl.BlockSpec((rows, heads * dv), tok),
            pl.BlockSpec((n_seq, heads, dk, dv), st),
        ),
        out_shape=(
            jax.ShapeDtypeStruct((t, heads * dv), out_dtype),
            jax.ShapeDtypeStruct(s0.shape, F32),
        ),
        scratch_shapes=[pltpu.VMEM((n_seq, heads, dk, dv), F32)],
        compiler_params=_cparams(("parallel", "arbitrary")),
        name="gla",
    )(gin, loga, s0, nw)


def _fox_prep_kernel(qf_ref, kf_ref, vf_ref, lft_ref, rq_ref, rk_ref, qat_ref, ka_ref, vt_ref, carry_scr, *,
                     heads, dh):
    j = pl.program_id(1)
    tp = qf_ref.shape[0]

    @pl.when(j == 0)
    def _():
        carry_scr[...] = jnp.zeros_like(carry_scr)

    rio = lax.broadcasted_iota(jnp.int32, (tp, tp), 0)
    cio = lax.broadcasted_iota(jnp.int32, (tp, tp), 1)
    upper = jnp.where(rio <= cio, 1.0, 0.0).astype(BF16)
    l1, l2, l3 = _split3(lft_ref[...])
    cs = _dot(jnp.concatenate([l1, l2, l3, jnp.zeros_like(l1)], axis=0), upper)
    c_t = cs[0:heads] + cs[heads:2 * heads] + cs[2 * heads:3 * heads] + carry_scr[:, 0:1]
    carry_scr[...] = jnp.broadcast_to(c_t[:, tp - 1:tp], carry_scr.shape)
    c1, c2, c3 = _split3(c_t)
    pad_rows = LANES - 3 * heads - 8
    aug_t = jnp.concatenate([c1.astype(F32), c2.astype(F32), c3.astype(F32),
                             jnp.ones((8, tp), F32), jnp.zeros((pad_rows, tp), F32)], axis=0)
    aug = aug_t.T.astype(BF16)
    lq_t = jnp.concatenate([(qf_ref[...].T * dh ** -0.5).astype(BF16), aug_t.astype(BF16)], axis=0)
    lk = jnp.concatenate([kf_ref[...].astype(BF16), aug], axis=1)
    qa_t = _dot(rq_ref[...], lq_t).astype(BF16)
    ka = _dot(lk, rk_ref[...]).astype(BF16)
    for h in range(heads):
        qat_ref[h] = qa_t[h * LANES:(h + 1) * LANES, :]
        ka_ref[h] = ka[:, h * LANES:(h + 1) * LANES]
    v_t = vf_ref[...].T.astype(BF16)
    for p in range(heads * dh // LANES):
        vt_ref[p] = v_t[p * LANES:(p + 1) * LANES, :]


def _placement_matrices(heads, dh):
    kin = heads * dh + LANES
    rq = np.zeros((kin, heads * LANES), np.float32)
    rk = np.zeros((kin, heads * LANES), np.float32)
    base = heads * dh
    ones_row = base + 3 * heads
    for h in range(heads):
        for d in range(dh):
            rq[h * dh + d, h * LANES + d] = 1.0
            rk[h * dh + d, h * LANES + d] = 1.0
        for p in range(3):
            rq[base + p * heads + h, h * LANES + dh + p] = 1.0
            rq[ones_row, h * LANES + dh + 3 + p] = 1.0
            rk[ones_row, h * LANES + dh + p] = 1.0
            rk[base + p * heads + h, h * LANES + dh + 3 + p] = -1.0
    return jnp.asarray(rq.T, BF16), jnp.asarray(rk, BF16)


def _fox_prep(qf, kf, vf, lft, batch, seq, heads, dh, tp):
    rq, rk = _placement_matrices(heads, dh)
    n_steps = seq // tp
    pairs = heads * dh // LANES
    tok = lambda b, j: (b * n_steps + j, 0)
    const = lambda b, j: (0, 0)
    return pl.pallas_call(
        functools.partial(_fox_prep_kernel, heads=heads, dh=dh),
        grid=(batch, n_steps),
        in_specs=[
            pl.BlockSpec((tp, heads * dh), tok),
            pl.BlockSpec((tp, heads * dh), tok),
            pl.BlockSpec((tp, heads * dh), tok),
            pl.BlockSpec((heads, tp), lambda b, j: (0, b * n_steps + j)),
            pl.BlockSpec(rq.shape, const),
            pl.BlockSpec(rk.shape, const),
        ],
        out_specs=(
            pl.BlockSpec((None, heads, LANES, tp), lambda b, j: (b, 0, 0, j)),
            pl.BlockSpec((None, heads, tp, LANES), lambda b, j: (b, 0, j, 0)),
            pl.BlockSpec((None, pairs, LANES, tp), lambda b, j: (b, 0, 0, j)),
        ),
        out_shape=(
            jax.ShapeDtypeStruct((batch, heads, LANES, seq), BF16),
            jax.ShapeDtypeStruct((batch, heads, seq, LANES), BF16),
            jax.ShapeDtypeStruct((batch, pairs, LANES, seq), BF16),
        ),
        scratch_shapes=[pltpu.VMEM((heads, LANES), F32)],
        compiler_params=_cparams(("parallel", "arbitrary")),
        name="fox_prep",
    )(qf, kf, vf, lft, rq, rk)


def _fox_flash_kernel(qi_ref, ki_ref, qat_ref, ka_ref, vt_ref, o_ref, m_scr, l_scr, acc_scr, *, dh, hps):
    step = pl.program_id(2)
    qi = qi_ref[step]
    ki = ki_ref[step]
    tq = qat_ref.shape[2]
    tk = ka_ref.shape[1]
    hpl = LANES // dh

    @pl.when(ki == 0)
    def _():
        m_scr[...] = jnp.full_like(m_scr, NEG)
        l_scr[...] = jnp.zeros_like(l_scr)
        acc_scr[...] = jnp.zeros_like(acc_scr)

    def update(masked):
        for h in range(hps):
            s = _dot(ka_ref[h], qat_ref[h])
            if masked:
                rio = lax.broadcasted_iota(jnp.int32, (tk, tq), 0)
                cio = lax.broadcasted_iota(jnp.int32, (tk, tq), 1)
                s = jnp.where(rio <= cio, s, NEG)
            m_prev = m_scr[h]
            m_new = jnp.maximum(m_prev, jnp.max(s, axis=0, keepdims=True))
            a = jnp.exp(m_prev - m_new)
            p = jnp.exp(s - m_new)
            l_scr[h] = a * l_scr[h] + jnp.sum(p, axis=0, keepdims=True)
            v_t = vt_ref[h // hpl, (h % hpl) * dh:(h % hpl + 1) * dh, :]
            acc_scr[h] = a * acc_scr[h] + _dot(v_t, p.astype(BF16))
            m_scr[h] = m_new

    @pl.when(ki < qi)
    def _():
        update(False)

    @pl.when(ki == qi)
    def _():
        update(True)
        o_t = jnp.concatenate([acc_scr[h] / l_scr[h] for h in range(hps)], axis=0)
        o_ref[...] = o_t.T.astype(o_ref.dtype)


def _fox_flash(qat, ka, vt, batch, seq, heads, dh, tq, hps):
    nq = seq // tq
    qi_l, ki_l = [], []
    for q in range(nq):
        for k in range(q + 1):
            qi_l.append(q)
            ki_l.append(k)
    qi_a = jnp.asarray(np.array(qi_l, np.int32))
    ki_a = jnp.asarray(np.array(ki_l, np.int32))
    n_tri = len(qi_l)
    hpl = LANES // dh
    grid_spec = pltpu.PrefetchScalarGridSpec(
        num_scalar_prefetch=2,
        grid=(batch, heads // hps, n_tri),
        in_specs=[
            pl.BlockSpec((None, hps, LANES, tq), lambda b, g, s, qi, ki: (b, g, 0, qi[s])),
            pl.BlockSpec((None, hps, tq, LANES), lambda b, g, s, qi, ki: (b, g, ki[s], 0)),
            pl.BlockSpec((None, hps // hpl, LANES, tq), lambda b, g, s, qi, ki: (b, g, 0, ki[s])),
        ],
        out_specs=pl.BlockSpec((tq, hps * dh), lambda b, g, s, qi, ki: (b * nq + qi[s], g)),
        scratch_shapes=[
            pltpu.VMEM((hps, 1, tq), F32),
            pltpu.VMEM((hps, 1, tq), F32),
            pltpu.VMEM((hps, dh, tq), F32),
        ],
    )
    return pl.pallas_call(
        functools.partial(_fox_flash_kernel, dh=dh, hps=hps),
        grid_spec=grid_spec,
        out_shape=jax.ShapeDtypeStruct((batch * seq, heads * dh), BF16),
        compiler_params=_cparams(("parallel", "parallel", "arbitrary")),
        name="fox_flash",
    )(qi_a, ki_a, qat, ka, vt)


def _fox_paged_kernel(pt_ref, q_ref, kn_ref, vn_ref, lfnt_ref, *refs, heads, dh, n_new, ppb, page):
    k_refs = refs[0:ppb]
    v_refs = refs[ppb:2 * ppb]
    lf_refs = refs[2 * ppb:3 * ppb]
    o_ref = refs[3 * ppb]
    qbd_scr, cncol_scr, cns_scr, base_scr, m_scr, l_scr, acc_scr = refs[3 * ppb + 1:]
    j = pl.program_id(1)
    rows = heads * n_new
    width = heads * dh

    rr = lax.broadcasted_iota(jnp.int32, (rows, width), 0)
    cc = lax.broadcasted_iota(jnp.int32, (rows, width), 1)
    bd_mask = (rr // n_new) == (cc // dh)

    @pl.when(j == 0)
    def _():
        q = q_ref[...] * dh ** -0.5
        q_rep = jnp.broadcast_to(q[None], (heads, n_new, width)).reshape(rows, width)
        qbd_scr[...] = jnp.where(bd_mask, q_rep, 0.0).astype(BF16)
        lfnt = lfnt_ref[...]
        lane = lax.broadcasted_iota(jnp.int32, lfnt.shape, 1)
        cnt = jnp.zeros_like(lfnt)
        for s in range(n_new):
            col = jnp.sum(jnp.where(lane <= s, lfnt, 0.0), axis=1, keepdims=True)
            cnt = cnt + jnp.where(lane == s, col, 0.0)
        cns = jnp.broadcast_to(cnt[:, None, :], (heads, n_new, LANES)).reshape(rows, LANES)
        cns_scr[...] = cns
        r2 = lax.broadcasted_iota(jnp.int32, (rows, LANES), 0)
        l2 = lax.broadcasted_iota(jnp.int32, (rows, LANES), 1)
        cncol_scr[...] = jnp.sum(jnp.where(l2 == r2 % n_new, cns, 0.0), axis=1, keepdims=True)
        base_scr[...] = jnp.zeros_like(base_scr)
        m_scr[...] = jnp.full_like(m_scr, NEG)
        l_scr[...] = jnp.zeros_like(l_scr)
        acc_scr[...] = jnp.zeros_like(acc_scr)

    rio = lax.broadcasted_iota(jnp.int32, (page, page), 0)
    cio = lax.broadcasted_iota(jnp.int32, (page, page), 1)
    after = jnp.where(rio > cio, 1.0, 0.0).astype(BF16)
    lf_all = jnp.concatenate([lf_refs[i][...] for i in range(ppb)], axis=0)
    ss = _dot(jnp.concatenate(_split3(lf_all), axis=0), after)
    nr = ppb * heads
    suf_all = ss[0:nr] + ss[nr:2 * nr] + ss[2 * nr:3 * nr]
    tot_all = suf_all[:, 0:1] + lf_all[:, 0:1]
    base = base_scr[:, 0:1]
    biases = []
    for i in range(ppb):
        biases.append(base + suf_all[i * heads:(i + 1) * heads])
        base = base + tot_all[i * heads:(i + 1) * heads]
    base_scr[...] = jnp.broadcast_to(base, base_scr.shape)
    bias = jnp.concatenate(biases, axis=1)
    bias_rows = jnp.broadcast_to(bias[:, None, :], (heads, n_new, ppb * page)).reshape(rows, ppb * page)

    kt = jnp.concatenate([k_refs[i][...].astype(BF16) for i in range(ppb)], axis=1)
    vt = jnp.concatenate([v_refs[i][...].astype(BF16) for i in range(ppb)], axis=1)
    qbd = qbd_scr[...]
    s = _dot(qbd, kt) + cncol_scr[...] + bias_rows
    m_prev = m_scr[...]
    m_new = jnp.maximum(m_prev, jnp.max(s, axis=1, keepdims=True))
    a = jnp.exp(m_prev - m_new)
    p = jnp.exp(s - m_new)
    l_scr[...] = a * l_scr[...] + jnp.sum(p, axis=1, keepdims=True)
    acc_scr[...] = a * acc_scr[...] + _dot_nt(p.astype(BF16), vt)
    m_scr[...] = m_new

    @pl.when(j == pl.num_programs(1) - 1)
    def _():
        pad = jnp.zeros((LANES - n_new, width), F32)
        kn = jnp.concatenate([kn_ref[...], pad], axis=0).astype(BF16)
        vn = jnp.concatenate([vn_ref[...], pad], axis=0).astype(BF16)
        r2 = lax.broadcasted_iota(jnp.int32, (rows, LANES), 0)
        l2 = lax.broadcasted_iota(jnp.int32, (rows, LANES), 1)
        sn = _dot_nt(qbd, kn) + cncol_scr[...] - cns_scr[...]
        sn = jnp.where(l2 <= r2 % n_new, sn, NEG)
        m_p = m_scr[...]
        m_n = jnp.maximum(m_p, jnp.max(sn, axis=1, keepdims=True))
        a2 = jnp.exp(m_p - m_n)
        pn = jnp.exp(sn - m_n)
        l_f = a2 * l_scr[...] + jnp.sum(pn, axis=1, keepdims=True)
        acc_f = a2 * acc_scr[...] + _dot(pn.astype(BF16), vn)
        o = jnp.where(bd_mask, acc_f / l_f, 0.0)
        o_ref[...] = jnp.sum(o.reshape(heads, n_new, width), axis=0)


def _fox_paged(pt_flat, qf, kf, vf, lfnt_pad, ck, cv, clf, *, n_seq, n_new, n_pages, heads, dh, ppb):
    page = ck.shape[-1]
    width = heads * dh
    rows = heads * n_new
    n_steps = n_pages // ppb

    def page_map(i):
        return lambda b, j, pt: (pt[b * n_pages + (n_pages - 1 - (j * ppb + i))], 0, 0)

    tok = pl.BlockSpec((n_new, width), lambda b, j, pt: (b, 0))
    in_specs = [tok, tok, tok, pl.BlockSpec((None, heads, LANES), lambda b, j, pt: (b, 0, 0))]
    in_specs += [pl.BlockSpec((None, width, page), page_map(i)) for i in range(ppb)]
    in_specs += [pl.BlockSpec((None, width, page), page_map(i)) for i in range(ppb)]
    in_specs += [pl.BlockSpec((None, heads, page), page_map(i)) for i in range(ppb)]
    grid_spec = pltpu.PrefetchScalarGridSpec(
        num_scalar_prefetch=1,
        grid=(n_seq, n_steps),
        in_specs=in_specs,
        out_specs=pl.BlockSpec((n_new, width), lambda b, j, pt: (b, 0)),
        scratch_shapes=[
            pltpu.VMEM((rows, width), BF16),
            pltpu.VMEM((rows, 1), F32),
            pltpu.VMEM((rows, LANES), F32),
            pltpu.VMEM((heads, LANES), F32),
            pltpu.VMEM((rows, 1), F32),
            pltpu.VMEM((rows, 1), F32),
            pltpu.VMEM((rows, width), F32),
        ],
    )
    kern = functools.partial(_fox_paged_kernel, heads=heads, dh=dh, n_new=n_new, ppb=ppb, page=page)
    return pl.pallas_call(
        kern,
        grid_spec=grid_spec,
        out_shape=jax.ShapeDtypeStruct((n_seq * n_new, width), F32),
        compiler_params=_cparams(("parallel", "arbitrary")),
        name="fox_paged",
    )(pt_flat, qf, kf, vf, lfnt_pad, *([ck] * ppb), *([cv] * ppb), *([clf] * ppb))


def _out_proj_kernel(x_ref, og_ref, of_ref, sga_ref, sgb_ref, wug_ref, wuf_ref, wo_ref, nfw_ref, wr_ref, br_ref,
                     h_ref, xn_ref, gates_ref, *, n_experts, n_groups):
    up_a = _dot(og_ref[...].astype(BF16), wug_ref[...])
    up_b = _dot(of_ref[...].astype(BF16), wuf_ref[...])
    merged = sga_ref[...] * up_a + sgb_ref[...] * up_b
    h = x_ref[...] + _dot(merged.astype(BF16), wo_ref[...])
    h_ref[...] = h
    xn = _rms(h, nfw_ref[...])
    xn_ref[...] = xn.astype(BF16)

    x_hi = xn.astype(BF16)
    x_lo = (xn - x_hi.astype(F32)).astype(BF16)
    l_hi = _dot(x_hi, wr_ref[...])
    logits = l_hi[:, :LANES] + l_hi[:, LANES:] + _dot(x_lo, wr_ref[:, :LANES]) + br_ref[...]
    lane = lax.broadcasted_iota(jnp.int32, logits.shape, 1).astype(F32)
    epg = n_experts // n_groups
    big = 4.0 * LANES
    is_g = (lane >= n_experts) & (lane < n_experts + n_groups)
    gl = jnp.where(is_g, logits, NEG)
    gmax = jnp.max(gl, axis=1, keepdims=True)
    gsum = jnp.sum(jnp.exp(gl - gmax), axis=1, keepdims=True)
    p_g = 1.0 / gsum
    g_sel = jnp.min(jnp.where(gl == gmax, lane, big), axis=1, keepdims=True) - n_experts
    e_lo = g_sel * epg
    in_grp = (lane >= e_lo) & (lane < e_lo + epg)
    el = jnp.where(in_grp, logits, NEG)
    emax = jnp.max(el, axis=1, keepdims=True)
    ee = jnp.exp(el - emax)
    pe = ee / jnp.sum(ee, axis=1, keepdims=True)
    cand = jnp.where(in_grp, pe, -1.0)
    v1 = jnp.max(cand, axis=1, keepdims=True)
    i1 = jnp.min(jnp.where(cand == v1, lane, big), axis=1, keepdims=True)
    cand2 = jnp.where(lane == i1, -1.0, cand)
    v2 = jnp.max(cand2, axis=1, keepdims=True)
    i2 = jnp.min(jnp.where(cand2 == v2, lane, big), axis=1, keepdims=True)
    tot = v1 + v2
    gates_ref[...] = jnp.where(lane == i1, p_g * (v1 / tot), 0.0) + jnp.where(lane == i2, p_g * (v2 / tot), 0.0)


def _out_proj(x2, og, of, sga, sgb, wug, wuf, wo, nfw, wr, br, *, n_experts, n_groups, tm):
    t, d_model = x2.shape
    row = lambda i: (i, 0)
    const = lambda i: (0, 0)
    return pl.pallas_call(
        functools.partial(_out_proj_kernel, n_experts=n_experts, n_groups=n_groups),
        grid=(t // tm,),
        in_specs=[
            pl.BlockSpec((tm, d_model), row),
            pl.BlockSpec((tm, og.shape[1]), row),
            pl.BlockSpec((tm, of.shape[1]), row),
            pl.BlockSpec((tm, d_model), row),
            pl.BlockSpec((tm, d_model), row),
            pl.BlockSpec(wug.shape, const, pipeline_mode=pl.Buffered(1)),
            pl.BlockSpec(wuf.shape, const, pipeline_mode=pl.Buffered(1)),
            pl.BlockSpec(wo.shape, const, pipeline_mode=pl.Buffered(1)),
            pl.BlockSpec((1, d_model), const),
            pl.BlockSpec(wr.shape, const),
            pl.BlockSpec((1, LANES), const),
        ],
        out_specs=(
            pl.BlockSpec((tm, d_model), row),
            pl.BlockSpec((tm, d_model), row),
            pl.BlockSpec((tm, LANES), row),
        ),
        out_shape=(
            jax.ShapeDtypeStruct((t, d_model), F32),
            jax.ShapeDtypeStruct((t, d_model), BF16),
            jax.ShapeDtypeStruct((t, LANES), F32),
        ),
        compiler_params=_cparams(("parallel",)),
        name="out_proj",
    )(x2, og, of, sga, sgb, wug, wuf, wo, nfw, wr, br)


def _moe_kernel(xn_ref, gates_ref, h_ref, wg_ref, wu_ref, wd_ref, fnw_ref, y_ref, acc_scr):
    e = pl.program_id(1)

    @pl.when(e == 0)
    def _():
        acc_scr[...] = jnp.zeros_like(acc_scr)

    x = xn_ref[...]
    g = _dot(x, wg_ref[...])
    u = _dot(x, wu_ref[...])
    hid = (g * _sigmoid(g)) * u
    gates = gates_ref[...]
    lane = lax.broadcasted_iota(jnp.int32, gates.shape, 1)
    gate_e = jnp.sum(jnp.where(lane == e, gates, 0.0), axis=1, keepdims=True)
    acc_scr[...] += gate_e * _dot(hid.astype(BF16), wd_ref[...])

    @pl.when(e == pl.num_programs(1) - 1)
    def _():
        y_ref[...] = _rms(h_ref[...] + acc_scr[...], fnw_ref[...])


def _moe(xn, gates, h, wg, wu, wd, fnw, *, tm):
    t, d_model = h.shape
    n_experts, _, d_exp = wg.shape
    row = lambda i, e: (i, 0)
    return pl.pallas_call(
        _moe_kernel,
        grid=(t // tm, n_experts),
        in_specs=[
            pl.BlockSpec((tm, d_model), row),
            pl.BlockSpec((tm, LANES), row),
            pl.BlockSpec((tm, d_model), row),
            pl.BlockSpec((None, d_model, d_exp), lambda i, e: (e, 0, 0)),
            pl.BlockSpec((None, d_model, d_exp), lambda i, e: (e, 0, 0)),
            pl.BlockSpec((None, d_exp, d_model), lambda i, e: (e, 0, 0)),
            pl.BlockSpec((1, d_model), lambda i, e: (0, 0)),
        ],
        out_specs=pl.BlockSpec((tm, d_model), row),
        out_shape=jax.ShapeDtypeStruct((t, d_model), F32),
        scratch_shapes=[pltpu.VMEM((tm, d_model), F32)],
        compiler_params=_cparams(("parallel", "arbitrary")),
        name="moe",
    )(xn, gates, h, wg, wu, wd, fnw)


def _pick_tile(n, pref):
    t = min(n, pref)
    while n % t:
        t //= 2
    return t


def kernel(x_prompt, x_sample, state_gla, cache_k, cache_v, cache_logf, page_table, norm_mix_w, w_in, w_gla_a2,
           b_gla_a2, gla_norm_w, b_fox_f, w_up_gla, w_up_fox, w_out, norm_ffn_w, w_router_group, b_router_group,
           w_router_expert, b_router_expert, w_exp_gate, w_exp_up, w_exp_down, final_norm_w):
    depth = w_in.shape[0]
    assert depth == 1
    batch, seq, d_model = x_prompt.shape
    n_seq, n_new, _ = x_sample.shape
    _, _, g_heads, dk, dv = state_gla.shape
    _, n_phys, page, f_heads, dh = cache_k.shape
    n_pages = page_table.shape[1]
    n_lr = w_gla_a2.shape[1]
    n_groups = w_router_group.shape[2]
    n_experts = w_router_expert.shape[2]
    qk = g_heads * dk
    vw = g_heads * dv
    fw = f_heads * dh
    n_gla = 2 * qk + 2 * vw
    n_fox = 3 * fw
    dims = (n_gla, n_fox, d_model, n_lr, f_heads)

    wi = w_in[0]
    o_za = n_gla
    o_fox = o_za + n_lr
    o_fp = o_fox + n_fox
    o_gate = o_fp + f_heads
    w_small = jnp.concatenate([wi[:, o_za:o_fox], wi[:, o_fp:o_gate],
                               jnp.zeros((d_model, LANES - n_lr - f_heads), wi.dtype)], axis=1)
    w_all = jnp.concatenate([wi[:, :o_za], wi[:, o_fox:o_fp], wi[:, o_gate:], w_small], axis=1).astype(BF16)
    w2p = jnp.concatenate([w_gla_a2[0], jnp.zeros((LANES - n_lr, qk), F32)], axis=0).astype(BF16)
    b2 = b_gla_a2[0].reshape(1, qk)
    bfc = b_fox_f[0].reshape(f_heads, 1)
    nw_mix = norm_mix_w[0].reshape(1, d_model)
    nw_gla = gla_norm_w[0].reshape(1, dv)
    wug = w_up_gla[0].astype(BF16)
    wuf = w_up_fox[0].astype(BF16)
    wo = w_out[0].astype(BF16)
    nfw = norm_ffn_w[0].reshape(1, d_model)
    wr32 = jnp.concatenate([w_router_expert[0], w_router_group[0],
                            jnp.zeros((d_model, LANES - n_experts - n_groups), F32)], axis=1)
    wr_hi = wr32.astype(BF16)
    wr = jnp.concatenate([wr_hi, (wr32 - wr_hi.astype(F32)).astype(BF16)], axis=1)
    br = jnp.concatenate([b_router_expert[0], b_router_group[0],
                          jnp.zeros((LANES - n_experts - n_groups,), F32)]).reshape(1, LANES)
    wg = w_exp_gate[0].astype(BF16)
    wu = w_exp_up[0].astype(BF16)
    wd = w_exp_down[0].astype(BF16)
    fnw = final_norm_w.reshape(1, d_model)

    def token_path(x2, mixers):
        tm = _pick_tile(x2.shape[0], 512)
        gin, loga, qf, kf, vf, sga, sgb, lft = _in_proj(x2, nw_mix, w_all, w2p, b2, bfc, dims, tm)
        og, s_new, of = mixers(gin, loga, qf, kf, vf, lft)
        h, xn, gates = _out_proj(x2, og, of, sga, sgb, wug, wuf, wo, nfw, wr, br,
                                 n_experts=n_experts, n_groups=n_groups, tm=tm)
        y = _moe(xn, gates, h, wg, wu, wd, fnw, tm=_pick_tile(x2.shape[0], 1024))
        return y, s_new, kf, vf, lft.T

    def prompt_mixers(gin, loga, qf, kf, vf, lft):
        chunk = math.gcd(seq, GLA_CHUNK)
        n_chunks = _pick_tile(seq // chunk, 8)
        s0 = jnp.zeros((batch, g_heads, dk, dv), F32)
        og, s_new = _gla(gin, loga, s0, nw_gla, n_outer=batch, n_steps=seq // (chunk * n_chunks), n_seq=1,
                         n_chunks=n_chunks, chunk=chunk, mx_dtype=BF16, out_dtype=BF16)
        tp = _pick_tile(seq, 512)
        qat, ka, vt = _fox_prep(qf, kf, vf, lft, batch, seq, f_heads, dh, tp)
        of = _fox_flash(qat, ka, vt, batch, seq, f_heads, dh, tp, hps=FLASH_HEADS_PER_STEP)
        return og, s_new, of

    y_p, s_p, k_p, v_p, lf_p = token_path(x_prompt.reshape(batch * seq, d_model), prompt_mixers)

    ck = jnp.transpose(cache_k[0], (0, 2, 3, 1)).reshape(n_phys, fw, page)
    cv = jnp.transpose(cache_v[0], (0, 2, 3, 1)).reshape(n_phys, fw, page)
    clf = jnp.transpose(cache_logf[0], (0, 2, 1))
    pt_flat = page_table.reshape(-1).astype(jnp.int32)

    def sample_mixers(gin, loga, qf, kf, vf, lft):
        chunk = math.gcd(n_new, GLA_CHUNK)
        assert chunk == n_new
        gs = _pick_tile(n_seq, 8)
        og, s_new = _gla(gin, loga, state_gla[0], nw_gla, n_outer=n_seq // gs, n_steps=1, n_seq=gs,
                         n_chunks=1, chunk=chunk, mx_dtype=F32, out_dtype=F32)
        lfnt = jnp.transpose(lft.reshape(f_heads, n_seq, n_new), (1, 0, 2))
        lfnt_pad = jnp.pad(lfnt, ((0, 0), (0, 0), (0, LANES - n_new)))
        of = _fox_paged(pt_flat, qf, kf, vf, lfnt_pad, ck, cv, clf, n_seq=n_seq, n_new=n_new,
                        n_pages=n_pages, heads=f_heads, dh=dh, ppb=_pick_tile(n_pages, PAGES_PER_STEP))
        return og, s_new, of

    y_s, s_s, k_s, v_s, lf_s = token_path(x_sample.reshape(n_seq * n_new, d_model), sample_mixers)

    return (y_p.reshape(batch, seq, d_model),
            y_s.reshape(n_seq, n_new, d_model),
            s_p[None],
            s_s[None],
            k_p.reshape(1, batch, seq, f_heads, dh),
            v_p.reshape(1, batch, seq, f_heads, dh),
            lf_p.reshape(1, batch, seq, f_heads),
            k_s.reshape(1, n_seq, n_new, f_heads, dh),
            v_s.reshape(1, n_seq, n_new, f_heads, dh),
            lf_s.reshape(1, n_seq, n_new, f_heads))
```

```python
import functools
import math

import jax
import jax.numpy as jnp
import numpy as np
from jax import lax
from jax.experimental import pallas as pl
from jax.experimental.pallas import tpu as pltpu

F32 = jnp.float32
BF16 = jnp.bfloat16

RMS_EPS = 1e-6
GLA_TAU = 16.0
GLA_CHUNK = 64
FLASH_HEADS_PER_STEP = 8
PAGES_PER_STEP = 32
MOE_CHUNK_ROWS = 288
MOE_EXPERTS_PER_STEP = 4
NEG = -1e30

V7X_VMEM_LIMIT_BYTES = 56 * 1024 * 1024
LANES = 128


def _cparams(sem):
    return pltpu.CompilerParams(dimension_semantics=sem, vmem_limit_bytes=V7X_VMEM_LIMIT_BYTES)


def _sigmoid(x):
    return 1.0 / (1.0 + jnp.exp(-x))


def _log_sigmoid(x):
    return jnp.minimum(x, 0.0) - jnp.log(1.0 + jnp.exp(-jnp.abs(x)))


def _rms(x, w):
    ms = jnp.mean(x * x, axis=-1, keepdims=True)
    return x * lax.rsqrt(ms + RMS_EPS) * w


def _split3(x):
    p1 = x.astype(BF16)
    r1 = x - p1.astype(F32)
    p2 = r1.astype(BF16)
    p3 = (r1 - p2.astype(F32)).astype(BF16)
    return p1, p2, p3


def _dot(a, b):
    return jnp.dot(a, b, preferred_element_type=F32)


def _dot_nt(a, b):
    return lax.dot_general(a, b, (((1,), (1,)), ((), ())), preferred_element_type=F32)


def _in_proj_kernel(x_ref, nw_ref, w_ref, w2_ref, b2_ref, bf_ref, gla_ref, loga_ref, qf_ref, *refs, dims, kv_t):
    n_gla, n_fox, d_model, n_lr, n_fh = dims
    sga_ref, sgb_ref, lft_ref = refs[-3:]
    xn = _rms(x_ref[...], nw_ref[...]).astype(BF16)

    def mm(lo, hi):
        return _dot(xn, w_ref[:, lo:hi])

    o = 0
    gla_ref[...] = mm(o, o + n_gla)
    o += n_gla
    fw = n_fox // 3
    qf_ref[...] = mm(o, o + fw)
    kf = mm(o + fw, o + 2 * fw)
    vf = mm(o + 2 * fw, o + 3 * fw)
    if kv_t:
        kb_ref, kt_ref, vt_ref, vtb_ref = refs[:4]
        kb_ref[...] = kf.astype(BF16)
        kt_ref[...] = kf.T
        v_t = vf.T
        vt_ref[...] = v_t
        v_tb = v_t.astype(BF16)
        for p in range(fw // LANES):
            vtb_ref[p] = v_tb[p * LANES:(p + 1) * LANES, :]
    else:
        kf_ref, vf_ref = refs[:2]
        kf_ref[...] = kf
        vf_ref[...] = vf
    o += n_fox
    sga_ref[...] = _sigmoid(mm(o, o + d_model))
    sgb_ref[...] = _sigmoid(mm(o + d_model, o + 2 * d_model))
    o += 2 * d_model
    small = mm(o, o + LANES)
    pre = _dot(small.astype(BF16), w2_ref[...]) + b2_ref[...]
    loga_ref[...] = _log_sigmoid(pre) * (1.0 / GLA_TAU)
    small_t = small.T
    lft_ref[...] = _log_sigmoid(small_t[n_lr:n_lr + n_fh, :] + bf_ref[...])


def _in_proj(x2, nw, w_all, w2p, b2, bfc, dims, tm, seq=None):
    n_gla, n_fox, d_model, n_lr, n_fh = dims
    t = x2.shape[0]
    fw = n_fox // 3
    n_qk = w2p.shape[1]
    row = lambda i: (i, 0)
    const = lambda i: (0, 0)
    if seq is None:
        kv_shapes = (jax.ShapeDtypeStruct((t, fw), F32),) * 2
        kv_specs = (pl.BlockSpec((tm, fw), row),) * 2
    else:
        nsb = seq // tm
        batch = t // seq
        kv_shapes = (
            jax.ShapeDtypeStruct((t, fw), BF16),
            jax.ShapeDtypeStruct((batch, fw, seq), F32),
            jax.ShapeDtypeStruct((batch, fw, seq), F32),
            jax.ShapeDtypeStruct((batch, fw // LANES, LANES, seq), BF16),
        )
        kv_specs = (
            pl.BlockSpec((tm, fw), row),
            pl.BlockSpec((None, fw, tm), lambda i: (i // nsb, 0, i % nsb)),
            pl.BlockSpec((None, fw, tm), lambda i: (i // nsb, 0, i % nsb)),
            pl.BlockSpec((None, fw // LANES, LANES, tm), lambda i: (i // nsb, 0, 0, i % nsb)),
        )
    out_shape = (
        jax.ShapeDtypeStruct((t, n_gla), F32),
        jax.ShapeDtypeStruct((t, n_qk), F32),
        jax.ShapeDtypeStruct((t, fw), F32),
        *kv_shapes,
        jax.ShapeDtypeStruct((t, d_model), F32),
        jax.ShapeDtypeStruct((t, d_model), F32),
        jax.ShapeDtypeStruct((n_fh, t), F32),
    )
    return pl.pallas_call(
        functools.partial(_in_proj_kernel, dims=dims, kv_t=seq is not None),
        grid=(t // tm,),
        in_specs=[
            pl.BlockSpec((tm, d_model), row),
            pl.BlockSpec((1, d_model), const),
            pl.BlockSpec(w_all.shape, const, pipeline_mode=pl.Buffered(1)),
            pl.BlockSpec(w2p.shape, const),
            pl.BlockSpec((1, n_qk), const),
            pl.BlockSpec((n_fh, 1), const),
        ],
        out_specs=(
            pl.BlockSpec((tm, n_gla), row),
            pl.BlockSpec((tm, n_qk), row),
            pl.BlockSpec((tm, fw), row),
            *kv_specs,
            pl.BlockSpec((tm, d_model), row),
            pl.BlockSpec((tm, d_model), row),
            pl.BlockSpec((n_fh, tm), lambda i: (0, i)),
        ),
        out_shape=out_shape,
        compiler_params=_cparams(("parallel",)),
        name="in_proj",
    )(x2, nw, w_all, w2p, b2, bfc)


def _gla_kernel(gin_ref, loga_ref, s0_ref, nw_ref, og_ref, sout_ref, s_scr, *,
                chunk, n_chunks, n_seq, heads, dk, dv, mx_dtype):
    j = pl.program_id(1)
    qk = heads * dk
    vw = heads * dv
    kpad = max(chunk, LANES)

    @pl.when(j == 0)
    def _():
        s_scr[...] = s0_ref[...]

    rows = n_seq * n_chunks * chunk
    shift = chunk.bit_length() - 1
    assert chunk == 1 << shift
    nw = nw_ref[...]

    rr = lax.broadcasted_iota(jnp.int32, (rows, rows), 0)
    cc = lax.broadcasted_iota(jnp.int32, (rows, rows), 1)
    same_chunk = (rr >> shift) == (cc >> shift)
    tri = jnp.where(same_chunk, jnp.where(cc <= rr, 1.0, 0.0), 0.0).astype(mx_dtype)
    gs = jnp.concatenate([p.astype(mx_dtype) for p in _split3(loga_ref[...])], axis=1)
    bs = _dot(tri, gs)
    b_all = bs[:, :qk] + bs[:, qk:2 * qk] + bs[:, 2 * qk:]
    k_all = gin_ref[:, qk:2 * qk]
    q_i_all = (gin_ref[:, 0:qk] * jnp.exp(b_all) * dk ** -0.5).astype(mx_dtype)
    k_i_all = k_all * jnp.exp(-b_all)
    vb_all = gin_ref[:, 2 * qk:2 * qk + vw].astype(mx_dtype)
    r_all = gin_ref[:, 2 * qk + vw:2 * qk + 2 * vw]
    silu_r = r_all * _sigmoid(r_all)

    rio = lax.broadcasted_iota(jnp.int32, (chunk, kpad), 0)
    cio = lax.broadcasted_iota(jnp.int32, (chunk, kpad), 1)
    tril = cio <= rio

    def pad_rows(a):
        if kpad == chunk:
            return a
        return jnp.concatenate([a, jnp.zeros((kpad - chunk, a.shape[1]), a.dtype)], axis=0)

    for idx in range(n_seq * n_chunks):
        seq = idx // n_chunks
        sl = slice(idx * chunk, (idx + 1) * chunk)
        b = b_all[sl]
        b_last = b[chunk - 1:chunk, :]
        q_i = q_i_all[sl]
        k_i = pad_rows(k_i_all[sl]).astype(mx_dtype)
        k_end_t = pad_rows(k_all[sl] * jnp.exp(b_last - b)).T.astype(mx_dtype)
        decay = jnp.exp(pad_rows(b).T[:, chunk - 1:chunk])
        vb = pad_rows(vb_all[sl])
        for h in range(heads):
            qh = q_i[:, h * dk:(h + 1) * dk]
            kh = k_i[:, h * dk:(h + 1) * dk]
            vh = vb[:, h * dv:(h + 1) * dv]
            a = jnp.where(tril, _dot_nt(qh, kh), 0.0).astype(mx_dtype)
            s_old = s_scr[seq, h]
            o = _dot(a, vh) + _dot(qh, s_old.astype(mx_dtype))
            kv = _dot(k_end_t[h * dk:(h + 1) * dk, :], vh)
            s_scr[seq, h] = decay[h * dk:(h + 1) * dk, :] * s_old + kv
            og_ref[sl, h * dv:(h + 1) * dv] = (_rms(o, nw) * silu_r[sl, h * dv:(h + 1) * dv]).astype(og_ref.dtype)

    @pl.when(j == pl.num_programs(1) - 1)
    def _():
        sout_ref[...] = s_scr[...]


def _gla(gin, loga, s0, nw, *, n_outer, n_steps, n_seq, n_chunks, chunk, mx_dtype, out_dtype):
    t = gin.shape[0]
    _, heads, dk, dv = s0.shape
    rows = n_seq * n_chunks * chunk
    assert t == n_outer * n_steps * rows
    kern = functools.partial(_gla_kernel, chunk=chunk, n_chunks=n_chunks, n_seq=n_seq, heads=heads,
                             dk=dk, dv=dv, mx_dtype=mx_dtype)
    tok = lambda i, j: (i * n_steps + j, 0)
    st = lambda i, j: (i, 0, 0, 0)
    return pl.pallas_call(
        kern,
        grid=(n_outer, n_steps),
        in_specs=[
            pl.BlockSpec((rows, gin.shape[1]), tok),
            pl.BlockSpec((rows, loga.shape[1]), tok),
            pl.BlockSpec((n_seq, heads, dk, dv), st),
            pl.BlockSpec((1, dv), lambda i, j: (0, 0)),
        ],
        out_specs=(
            pl.BlockSpec((rows, heads * dv), tok),
            pl.BlockSpec((n_seq, heads, dk, dv), st),
        ),
        out_shape=(
            jax.ShapeDtypeStruct((t, heads * dv), out_dtype),
            jax.ShapeDtypeStruct(s0.shape, F32),
        ),
        scratch_shapes=[pltpu.VMEM((n_seq, heads, dk, dv), F32)],
        compiler_params=_cparams(("parallel", "arbitrary")),
        name="gla",
    )(gin, loga, s0, nw)


def _fox_prep_kernel(qf_ref, kf_ref, lft_ref, rq_ref, rk_ref, qat_ref, ka_ref, carry_scr, *, heads, dh):
    j = pl.program_id(1)
    tp = qf_ref.shape[0]

    @pl.when(j == 0)
    def _():
        carry_scr[...] = jnp.zeros_like(carry_scr)

    rio = lax.broadcasted_iota(jnp.int32, (tp, tp), 0)
    cio = lax.broadcasted_iota(jnp.int32, (tp, tp), 1)
    upper = jnp.where(rio <= cio, 1.0, 0.0).astype(BF16)
    l1, l2, l3 = _split3(lft_ref[...])
    cs = _dot(jnp.concatenate([l1, l2, l3, jnp.zeros_like(l1)], axis=0), upper)
    c_t = cs[0:heads] + cs[heads:2 * heads] + cs[2 * heads:3 * heads] + carry_scr[:, 0:1]
    carry_scr[...] = jnp.broadcast_to(c_t[:, tp - 1:tp], carry_scr.shape)
    c1, c2, c3 = _split3(c_t)
    pad_rows = LANES - 3 * heads - 8
    aug_t = jnp.concatenate([c1.astype(F32), c2.astype(F32), c3.astype(F32),
                             jnp.ones((8, tp), F32), jnp.zeros((pad_rows, tp), F32)], axis=0)
    aug = aug_t.T.astype(BF16)
    lq_t = jnp.concatenate([(qf_ref[...].T * dh ** -0.5).astype(BF16), aug_t.astype(BF16)], axis=0)
    lk = jnp.concatenate([kf_ref[...].astype(BF16), aug], axis=1)
    qa_t = _dot(rq_ref[...], lq_t).astype(BF16)
    ka = _dot(lk, rk_ref[...]).astype(BF16)
    for h in range(heads):
        qat_ref[h] = qa_t[h * LANES:(h + 1) * LANES, :]
        ka_ref[h] = ka[:, h * LANES:(h + 1) * LANES]


def _placement_matrices(heads, dh):
    kin = heads * dh + LANES
    rq = np.zeros((kin, heads * LANES), np.float32)
    rk = np.zeros((kin, heads * LANES), np.float32)
    base = heads * dh
    ones_row = base + 3 * heads
    for h in range(heads):
        for d in range(dh):
            rq[h * dh + d, h * LANES + d] = 1.0
            rk[h * dh + d, h * LANES + d] = 1.0
        for p in range(3):
            rq[base + p * heads + h, h * LANES + dh + p] = 1.0
            rq[ones_row, h * LANES + dh + 3 + p] = 1.0
            rk[ones_row, h * LANES + dh + p] = 1.0
            rk[base + p * heads + h, h * LANES + dh + 3 + p] = -1.0
    return jnp.asarray(rq.T, BF16), jnp.asarray(rk, BF16)


def _fox_prep(qf, kf, lft, batch, seq, heads, dh, tp):
    rq, rk = _placement_matrices(heads, dh)
    n_steps = seq // tp
    tok = lambda b, j: (b * n_steps + j, 0)
    const = lambda b, j: (0, 0)
    return pl.pallas_call(
        functools.partial(_fox_prep_kernel, heads=heads, dh=dh),
        grid=(batch, n_steps),
        in_specs=[
            pl.BlockSpec((tp, heads * dh), tok),
            pl.BlockSpec((tp, heads * dh), tok),
            pl.BlockSpec((heads, tp), lambda b, j: (0, b * n_steps + j)),
            pl.BlockSpec(rq.shape, const),
            pl.BlockSpec(rk.shape, const),
        ],
        out_specs=(
            pl.BlockSpec((None, heads, LANES, tp), lambda b, j: (b, 0, 0, j)),
            pl.BlockSpec((None, heads, tp, LANES), lambda b, j: (b, 0, j, 0)),
        ),
        out_shape=(
            jax.ShapeDtypeStruct((batch, heads, LANES, seq), BF16),
            jax.ShapeDtypeStruct((batch, heads, seq, LANES), BF16),
        ),
        scratch_shapes=[pltpu.VMEM((heads, LANES), F32)],
        compiler_params=_cparams(("parallel", "arbitrary")),
        name="fox_prep",
    )(qf, kf, lft, rq, rk)


def _fox_flash_kernel(qi_ref, ki_ref, qat_ref, ka_ref, vt_ref, o_ref, m_scr, l_scr, acc_scr, *, dh, hps):
    step = pl.program_id(2)
    qi = qi_ref[step]
    ki = ki_ref[step]
    tq = qat_ref.shape[2]
    tk = ka_ref.shape[1]
    hpl = LANES // dh

    @pl.when(ki == 0)
    def _():
        m_scr[...] = jnp.full_like(m_scr, NEG)
        l_scr[...] = jnp.zeros_like(l_scr)
        acc_scr[...] = jnp.zeros_like(acc_scr)

    def update(masked):
        for h in range(hps):
            s = _dot(ka_ref[h], qat_ref[h])
            if masked:
                rio = lax.broadcasted_iota(jnp.int32, (tk, tq), 0)
                cio = lax.broadcasted_iota(jnp.int32, (tk, tq), 1)
                s = jnp.where(rio <= cio, s, NEG)
            m_prev = m_scr[h]
            m_new = jnp.maximum(m_prev, jnp.max(s, axis=0, keepdims=True))
            a = jnp.exp(m_prev - m_new)
            p = jnp.exp(s - m_new)
            l_scr[h] = a * l_scr[h] + jnp.sum(p, axis=0, keepdims=True)
            v_t = vt_ref[h // hpl, (h % hpl) * dh:(h % hpl + 1) * dh, :]
            acc_scr[h] = a * acc_scr[h] + _dot(v_t, p.astype(BF16))
            m_scr[h] = m_new

    @pl.when(ki < qi)
    def _():
        update(False)

    @pl.when(ki == qi)
    def _():
        update(True)
        o_t = jnp.concatenate([acc_scr[h] / l_scr[h] for h in range(hps)], axis=0)
        o_ref[...] = o_t.T.astype(o_ref.dtype)


def _fox_flash(qat, ka, vt, batch, seq, heads, dh, tq, hps):
    nq = seq // tq
    qi_l, ki_l = [], []
    for q in range(nq):
        for k in range(q + 1):
            qi_l.append(q)
            ki_l.append(k)
    qi_a = jnp.asarray(np.array(qi_l, np.int32))
    ki_a = jnp.asarray(np.array(ki_l, np.int32))
    n_tri = len(qi_l)
    hpl = LANES // dh
    grid_spec = pltpu.PrefetchScalarGridSpec(
        num_scalar_prefetch=2,
        grid=(batch, heads // hps, n_tri),
        in_specs=[
            pl.BlockSpec((None, hps, LANES, tq), lambda b, g, s, qi, ki: (b, g, 0, qi[s])),
            pl.BlockSpec((None, hps, tq, LANES), lambda b, g, s, qi, ki: (b, g, ki[s], 0)),
            pl.BlockSpec((None, hps // hpl, LANES, tq), lambda b, g, s, qi, ki: (b, g, 0, ki[s])),
        ],
        out_specs=pl.BlockSpec((tq, hps * dh), lambda b, g, s, qi, ki: (b * nq + qi[s], g)),
        scratch_shapes=[
            pltpu.VMEM((hps, 1, tq), F32),
            pltpu.VMEM((hps, 1, tq), F32),
            pltpu.VMEM((hps, dh, tq), F32),
        ],
    )
    return pl.pallas_call(
        functools.partial(_fox_flash_kernel, dh=dh, hps=hps),
        grid_spec=grid_spec,
        out_shape=jax.ShapeDtypeStruct((batch * seq, heads * dh), BF16),
        compiler_params=_cparams(("parallel", "parallel", "arbitrary")),
        name="fox_flash",
    )(qi_a, ki_a, qat, ka, vt)


def _fox_paged_kernel(pt_ref, q_ref, kn_ref, vn_ref, lfnt_ref, *refs, heads, dh, n_new, ppb, page):
    k_refs = refs[0:ppb]
    v_refs = refs[ppb:2 * ppb]
    lf_refs = refs[2 * ppb:3 * ppb]
    o_ref = refs[3 * ppb]
    qbd_scr, cncol_scr, cns_scr, base_scr, m_scr, l_scr, acc_scr = refs[3 * ppb + 1:]
    j = pl.program_id(1)
    rows = heads * n_new
    width = heads * dh

    rr = lax.broadcasted_iota(jnp.int32, (rows, width), 0)
    cc = lax.broadcasted_iota(jnp.int32, (rows, width), 1)
    bd_mask = (rr // n_new) == (cc // dh)

    @pl.when(j == 0)
    def _():
        q = q_ref[...] * dh ** -0.5
        q_rep = jnp.broadcast_to(q[None], (heads, n_new, width)).reshape(rows, width)
        qbd_scr[...] = jnp.where(bd_mask, q_rep, 0.0).astype(BF16)
        lfnt = lfnt_ref[...]
        lane = lax.broadcasted_iota(jnp.int32, lfnt.shape, 1)
        cnt = jnp.zeros_like(lfnt)
        for s in range(n_new):
            col = jnp.sum(jnp.where(lane <= s, lfnt, 0.0), axis=1, keepdims=True)
            cnt = cnt + jnp.where(lane == s, col, 0.0)
        cns = jnp.broadcast_to(cnt[:, None, :], (heads, n_new, LANES)).reshape(rows, LANES)
        cns_scr[...] = cns
        r2 = lax.broadcasted_iota(jnp.int32, (rows, LANES), 0)
        l2 = lax.broadcasted_iota(jnp.int32, (rows, LANES), 1)
        cncol_scr[...] = jnp.sum(jnp.where(l2 == r2 % n_new, cns, 0.0), axis=1, keepdims=True)
        base_scr[...] = jnp.zeros_like(base_scr)
        m_scr[...] = jnp.full_like(m_scr, NEG)
        l_scr[...] = jnp.zeros_like(l_scr)
        acc_scr[...] = jnp.zeros_like(acc_scr)

    rio = lax.broadcasted_iota(jnp.int32, (page, page), 0)
    cio = lax.broadcasted_iota(jnp.int32, (page, page), 1)
    after = jnp.where(rio > cio, 1.0, 0.0).astype(BF16)
    lf_all = jnp.concatenate([lf_refs[i][...] for i in range(ppb)], axis=0)
    ss = _dot(jnp.concatenate(_split3(lf_all), axis=0), after)
    nr = ppb * heads
    suf_all = ss[0:nr] + ss[nr:2 * nr] + ss[2 * nr:3 * nr]
    tot_all = suf_all[:, 0:1] + lf_all[:, 0:1]
    base = base_scr[:, 0:1]
    biases = []
    for i in range(ppb):
        biases.append(base + suf_all[i * heads:(i + 1) * heads])
        base = base + tot_all[i * heads:(i + 1) * heads]
    base_scr[...] = jnp.broadcast_to(base, base_scr.shape)
    bias = jnp.concatenate(biases, axis=1)
    bias_rows = jnp.broadcast_to(bias[:, None, :], (heads, n_new, ppb * page)).reshape(rows, ppb * page)

    kt = jnp.concatenate([k_refs[i][...].astype(BF16) for i in range(ppb)], axis=1)
    vt = jnp.concatenate([v_refs[i][...].astype(BF16) for i in range(ppb)], axis=1)
    qbd = qbd_scr[...]
    s = _dot(qbd, kt) + cncol_scr[...] + bias_rows
    m_prev = m_scr[...]
    m_new = jnp.maximum(m_prev, jnp.max(s, axis=1, keepdims=True))
    a = jnp.exp(m_prev - m_new)
    p = jnp.exp(s - m_new)
    l_scr[...] = a * l_scr[...] + jnp.sum(p, axis=1, keepdims=True)
    acc_scr[...] = a * acc_scr[...] + _dot_nt(p.astype(BF16), vt)
    m_scr[...] = m_new

    @pl.when(j == pl.num_programs(1) - 1)
    def _():
        pad = jnp.zeros((LANES - n_new, width), F32)
        kn = jnp.concatenate([kn_ref[...], pad], axis=0).astype(BF16)
        vn = jnp.concatenate([vn_ref[...], pad], axis=0).astype(BF16)
        r2 = lax.broadcasted_iota(jnp.int32, (rows, LANES), 0)
        l2 = lax.broadcasted_iota(jnp.int32, (rows, LANES), 1)
        sn = _dot_nt(qbd, kn) + cncol_scr[...] - cns_scr[...]
        sn = jnp.where(l2 <= r2 % n_new, sn, NEG)
        m_p = m_scr[...]
        m_n = jnp.maximum(m_p, jnp.max(sn, axis=1, keepdims=True))
        a2 = jnp.exp(m_p - m_n)
        pn = jnp.exp(sn - m_n)
        l_f = a2 * l_scr[...] + jnp.sum(pn, axis=1, keepdims=True)
        acc_f = a2 * acc_scr[...] + _dot(pn.astype(BF16), vn)
        o = jnp.where(bd_mask, acc_f / l_f, 0.0)
        o_ref[...] = jnp.sum(o.reshape(heads, n_new, width), axis=0)


def _fox_paged(pt_flat, qf, kf, vf, lfnt_pad, ck, cv, clf, *, n_seq, n_new, n_pages, heads, dh, ppb):
    page = ck.shape[-1]
    width = heads * dh
    rows = heads * n_new
    n_steps = n_pages // ppb

    def page_map(i):
        return lambda b, j, pt: (pt[b * n_pages + (n_pages - 1 - (j * ppb + i))], 0, 0)

    tok = pl.BlockSpec((n_new, width), lambda b, j, pt: (b, 0))
    in_specs = [tok, tok, tok, pl.BlockSpec((None, heads, LANES), lambda b, j, pt: (b, 0, 0))]
    in_specs += [pl.BlockSpec((None, width, page), page_map(i)) for i in range(ppb)]
    in_specs += [pl.BlockSpec((None, width, page), page_map(i)) for i in range(ppb)]
    in_specs += [pl.BlockSpec((None, heads, page), page_map(i)) for i in range(ppb)]
    grid_spec = pltpu.PrefetchScalarGridSpec(
        num_scalar_prefetch=1,
        grid=(n_seq, n_steps),
        in_specs=in_specs,
        out_specs=pl.BlockSpec((n_new, width), lambda b, j, pt: (b, 0)),
        scratch_shapes=[
            pltpu.VMEM((rows, width), BF16),
            pltpu.VMEM((rows, 1), F32),
            pltpu.VMEM((rows, LANES), F32),
            pltpu.VMEM((heads, LANES), F32),
            pltpu.VMEM((rows, 1), F32),
            pltpu.VMEM((rows, 1), F32),
            pltpu.VMEM((rows, width), F32),
        ],
    )
    kern = functools.partial(_fox_paged_kernel, heads=heads, dh=dh, n_new=n_new, ppb=ppb, page=page)
    return pl.pallas_call(
        kern,
        grid_spec=grid_spec,
        out_shape=jax.ShapeDtypeStruct((n_seq * n_new, width), F32),
        compiler_params=_cparams(("parallel", "arbitrary")),
        name="fox_paged",
    )(pt_flat, qf, kf, vf, lfnt_pad, *([ck] * ppb), *([cv] * ppb), *([clf] * ppb))


def _out_proj_kernel(x_ref, og_ref, of_ref, sga_ref, sgb_ref, wug_ref, wuf_ref, wo_ref, nfw_ref, wr_ref, br_ref,
                     h_ref, xn_ref, gates_ref, *, n_experts, n_groups):
    up_a = _dot(og_ref[...].astype(BF16), wug_ref[...])
    up_b = _dot(of_ref[...].astype(BF16), wuf_ref[...])
    merged = sga_ref[...] * up_a + sgb_ref[...] * up_b
    h = x_ref[...] + _dot(merged.astype(BF16), wo_ref[...])
    h_ref[...] = h
    xn = _rms(h, nfw_ref[...])
    xn_ref[...] = xn.astype(BF16)

    x_hi = xn.astype(BF16)
    x_lo = (xn - x_hi.astype(F32)).astype(BF16)
    l_hi = _dot(x_hi, wr_ref[...])
    logits = l_hi[:, :LANES] + l_hi[:, LANES:] + _dot(x_lo, wr_ref[:, :LANES]) + br_ref[...]
    lane = lax.broadcasted_iota(jnp.int32, logits.shape, 1).astype(F32)
    epg = n_experts // n_groups
    big = 4.0 * LANES
    is_g = (lane >= n_experts) & (lane < n_experts + n_groups)
    gl = jnp.where(is_g, logits, NEG)
    gmax = jnp.max(gl, axis=1, keepdims=True)
    gsum = jnp.sum(jnp.exp(gl - gmax), axis=1, keepdims=True)
    p_g = 1.0 / gsum
    g_sel = jnp.min(jnp.where(gl == gmax, lane, big), axis=1, keepdims=True) - n_experts
    e_lo = g_sel * epg
    in_grp = (lane >= e_lo) & (lane < e_lo + epg)
    el = jnp.where(in_grp, logits, NEG)
    emax = jnp.max(el, axis=1, keepdims=True)
    ee = jnp.exp(el - emax)
    pe = ee / jnp.sum(ee, axis=1, keepdims=True)
    cand = jnp.where(in_grp, pe, -1.0)
    v1 = jnp.max(cand, axis=1, keepdims=True)
    i1 = jnp.min(jnp.where(cand == v1, lane, big), axis=1, keepdims=True)
    cand2 = jnp.where(lane == i1, -1.0, cand)
    v2 = jnp.max(cand2, axis=1, keepdims=True)
    i2 = jnp.min(jnp.where(cand2 == v2, lane, big), axis=1, keepdims=True)
    tot = v1 + v2
    gates_ref[...] = (jnp.where(lane == i1, p_g * (v1 / tot), 0.0) + jnp.where(lane == i2, p_g * (v2 / tot), 0.0)
                      + jnp.where(lane == n_experts, g_sel, 0.0))


def _out_proj(x2, og, of, sga, sgb, wug, wuf, wo, nfw, wr, br, *, n_experts, n_groups, tm):
    t, d_model = x2.shape
    row = lambda i: (i, 0)
    const = lambda i: (0, 0)
    return pl.pallas_call(
        functools.partial(_out_proj_kernel, n_experts=n_experts, n_groups=n_groups),
        grid=(t // tm,),
        in_specs=[
            pl.BlockSpec((tm, d_model), row),
            pl.BlockSpec((tm, og.shape[1]), row),
            pl.BlockSpec((tm, of.shape[1]), row),
            pl.BlockSpec((tm, d_model), row),
            pl.BlockSpec((tm, d_model), row),
            pl.BlockSpec(wug.shape, const, pipeline_mode=pl.Buffered(1)),
            pl.BlockSpec(wuf.shape, const, pipeline_mode=pl.Buffered(1)),
            pl.BlockSpec(wo.shape, const, pipeline_mode=pl.Buffered(1)),
            pl.BlockSpec((1, d_model), const),
            pl.BlockSpec(wr.shape, const),
            pl.BlockSpec((1, LANES), const),
        ],
        out_specs=(
            pl.BlockSpec((tm, d_model), row),
            pl.BlockSpec((tm, d_model), row),
            pl.BlockSpec((tm, LANES), row),
        ),
        out_shape=(
            jax.ShapeDtypeStruct((t, d_model), F32),
            jax.ShapeDtypeStruct((t, d_model), BF16),
            jax.ShapeDtypeStruct((t, LANES), F32),
        ),
        compiler_params=_cparams(("parallel",)),
        name="out_proj",
    )(x2, og, of, sga, sgb, wug, wuf, wo, nfw, wr, br)


def _moe_kernel(xn_ref, gates_ref, h_ref, wg_ref, wu_ref, wd_ref, fnw_ref, y_ref,
                xs_scr, ys_scr, gs_scr, selt_scr, off_smem, *, n_experts, n_groups, chunk, eps):
    e = pl.program_id(1)
    tm = xn_ref.shape[0]
    rows = selt_scr.shape[1]
    rows_p = xs_scr.shape[0]
    epg = n_experts // n_groups

    @pl.when(e == 0)
    def _():
        gates = gates_ref[...]
        lane = lax.broadcasted_iota(jnp.int32, (tm, LANES), 1).astype(F32)
        g_sel = gates[:, n_experts:n_experts + 1]
        in_g = lane == g_sel
        onehot = jnp.where(in_g, 1.0, 0.0).astype(BF16)
        rio = lax.broadcasted_iota(jnp.int32, (tm, tm), 0)
        cio = lax.broadcasted_iota(jnp.int32, (tm, tm), 1)
        tril = jnp.where(cio <= rio, 1.0, 0.0).astype(BF16)
        cum = _dot(tril, onehot)
        cnt = cum[tm - 1:tm, :].astype(jnp.int32)
        lane1 = lax.broadcasted_iota(jnp.int32, (1, LANES), 1)
        off = jnp.int32(0)
        off_vec = jnp.zeros((1, LANES), F32)
        for g in range(n_groups):
            off_smem[g] = off
            off_vec = off_vec + jnp.where(lane1 == g, off.astype(F32), 0.0)
            off = off + ((cnt[0, g] + 7) // 8) * 8
        off_smem[n_groups] = off
        pos = jnp.sum(jnp.where(in_g, off_vec + cum - 1.0, 0.0), axis=1, keepdims=True)
        r_lane = lax.broadcasted_iota(jnp.int32, (tm, rows), 1).astype(F32)
        selt_scr[...] = jnp.where(r_lane == pos, 1.0, 0.0).astype(BF16)
        pos_t = jnp.broadcast_to(pos, (tm, LANES)).T[0:1, :]
        r_sub = lax.broadcasted_iota(jnp.int32, (rows, tm), 0).astype(F32)
        sel = jnp.where(r_sub == pos_t, 1.0, 0.0).astype(BF16)
        xs_scr[0:rows, :] = _dot(sel, xn_ref[...])
        g1, g2, g3 = _split3(gates)
        gs = _dot(sel, jnp.concatenate([g1, g2, g3], axis=1))
        gs_scr[0:rows, :] = gs[:, :LANES] + gs[:, LANES:2 * LANES] + gs[:, 2 * LANES:]
        xs_scr[rows:rows_p, :] = jnp.zeros((rows_p - rows, xs_scr.shape[1]), F32)
        gs_scr[rows:rows_p, :] = jnp.zeros((rows_p - rows, LANES), F32)
        ys_scr[...] = jnp.zeros_like(ys_scr)

    g = (e * eps) // epg
    start = off_smem[g]
    n_rows = off_smem[g + 1] - start
    lane_c = lax.broadcasted_iota(jnp.int32, (chunk, LANES), 1)

    def body(c, carry):
        r0 = pl.multiple_of(start + c * chunk, 8)
        x = xs_scr[pl.ds(r0, chunk), :].astype(BF16)
        gs = gs_scr[pl.ds(r0, chunk), :]
        acc = None
        for i in range(eps):
            gg = _dot(x, wg_ref[i])
            uu = _dot(x, wu_ref[i])
            hid = (gg * _sigmoid(gg)) * uu
            out = _dot(hid.astype(BF16), wd_ref[i])
            gate = jnp.sum(jnp.where(lane_c == e * eps + i, gs, 0.0), axis=1, keepdims=True)
            acc = gate * out if acc is None else acc + gate * out
        ys_scr[pl.ds(r0, chunk), :] += acc
        return carry

    lax.fori_loop(0, (n_rows + chunk - 1) // chunk, body, 0)

    @pl.when(e == pl.num_programs(1) - 1)
    def _():
        ys = ys_scr[0:rows, :]
        hi = ys.astype(BF16)
        lo = (ys - hi.astype(F32)).astype(BF16)
        selt = selt_scr[...]
        moe = _dot(selt, hi) + _dot(selt, lo)
        y_ref[...] = _rms(h_ref[...] + moe, fnw_ref[...])


def _moe(xn, gates, h, wg, wu, wd, fnw, *, tm, n_groups, chunk=MOE_CHUNK_ROWS, eps=MOE_EXPERTS_PER_STEP):
    t, d_model = h.shape
    n_experts, _, d_exp = wg.shape
    rows = tm + LANES
    rows_p = rows + chunk
    row = lambda i, e: (i, 0)
    once = pl.Buffered(1)
    assert (n_experts // n_groups) % eps == 0
    kern = functools.partial(_moe_kernel, n_experts=n_experts, n_groups=n_groups, chunk=chunk, eps=eps)
    return pl.pallas_call(
        kern,
        grid=(t // tm, n_experts // eps),
        in_specs=[
            pl.BlockSpec((tm, d_model), row, pipeline_mode=once),
            pl.BlockSpec((tm, LANES), row, pipeline_mode=once),
            pl.BlockSpec((tm, d_model), row, pipeline_mode=once),
            pl.BlockSpec((eps, d_model, d_exp), lambda i, e: (e, 0, 0)),
            pl.BlockSpec((eps, d_model, d_exp), lambda i, e: (e, 0, 0)),
            pl.BlockSpec((eps, d_exp, d_model), lambda i, e: (e, 0, 0)),
            pl.BlockSpec((1, d_model), lambda i, e: (0, 0)),
        ],
        out_specs=pl.BlockSpec((tm, d_model), row),
        out_shape=jax.ShapeDtypeStruct((t, d_model), F32),
        scratch_shapes=[
            pltpu.VMEM((rows_p, d_model), F32),
            pltpu.VMEM((rows_p, d_model), F32),
            pltpu.VMEM((rows_p, LANES), F32),
            pltpu.VMEM((tm, rows), BF16),
            pltpu.SMEM((n_groups + 1,), jnp.int32),
        ],
        compiler_params=_cparams(("parallel", "arbitrary")),
        name="moe",
    )(xn, gates, h, wg, wu, wd, fnw)


def _pick_tile(n, pref):
    t = min(n, pref)
    while n % t:
        t //= 2
    return t


def kernel(x_prompt, x_sample, state_gla, cache_k, cache_v, cache_logf, page_table, norm_mix_w, w_in, w_gla_a2,
           b_gla_a2, gla_norm_w, b_fox_f, w_up_gla, w_up_fox, w_out, norm_ffn_w, w_router_group, b_router_group,
           w_router_expert, b_router_expert, w_exp_gate, w_exp_up, w_exp_down, final_norm_w):
    depth = w_in.shape[0]
    assert depth == 1
    batch, seq, d_model = x_prompt.shape
    n_seq, n_new, _ = x_sample.shape
    _, _, g_heads, dk, dv = state_gla.shape
    _, n_phys, page, f_heads, dh = cache_k.shape
    n_pages = page_table.shape[1]
    n_lr = w_gla_a2.shape[1]
    n_groups = w_router_group.shape[2]
    n_experts = w_router_expert.shape[2]
    qk = g_heads * dk
    vw = g_heads * dv
    fw = f_heads * dh
    n_gla = 2 * qk + 2 * vw
    n_fox = 3 * fw
    dims = (n_gla, n_fox, d_model, n_lr, f_heads)

    wi = w_in[0]
    o_za = n_gla
    o_fox = o_za + n_lr
    o_fp = o_fox + n_fox
    o_gate = o_fp + f_heads
    w_small = jnp.concatenate([wi[:, o_za:o_fox], wi[:, o_fp:o_gate],
                               jnp.zeros((d_model, LANES - n_lr - f_heads), wi.dtype)], axis=1)
    w_all = jnp.concatenate([wi[:, :o_za], wi[:, o_fox:o_fp], wi[:, o_gate:], w_small], axis=1).astype(BF16)
    w2p = jnp.concatenate([w_gla_a2[0], jnp.zeros((LANES - n_lr, qk), F32)], axis=0).astype(BF16)
    b2 = b_gla_a2[0].reshape(1, qk)
    bfc = b_fox_f[0].reshape(f_heads, 1)
    nw_mix = norm_mix_w[0].reshape(1, d_model)
    nw_gla = gla_norm_w[0].reshape(1, dv)
    wug = w_up_gla[0].astype(BF16)
    wuf = w_up_fox[0].astype(BF16)
    wo = w_out[0].astype(BF16)
    nfw = norm_ffn_w[0].reshape(1, d_model)
    wr32 = jnp.concatenate([w_router_expert[0], w_router_group[0],
                            jnp.zeros((d_model, LANES - n_experts - n_groups), F32)], axis=1)
    wr_hi = wr32.astype(BF16)
    wr = jnp.concatenate([wr_hi, (wr32 - wr_hi.astype(F32)).astype(BF16)], axis=1)
    br = jnp.concatenate([b_router_expert[0], b_router_group[0],
                          jnp.zeros((LANES - n_experts - n_groups,), F32)]).reshape(1, LANES)
    wg = w_exp_gate[0].astype(BF16)
    wu = w_exp_up[0].astype(BF16)
    wd = w_exp_down[0].astype(BF16)
    fnw = final_norm_w.reshape(1, d_model)

    def token_path(x2, mixers, seq_len=None):
        tm = _pick_tile(x2.shape[0] if seq_len is None else seq_len, 512)
        gin, loga, qf, *kv, sga, sgb, lft = _in_proj(x2, nw_mix, w_all, w2p, b2, bfc, dims, tm, seq=seq_len)
        og, s_new, of = mixers(gin, loga, qf, kv, lft)
        h, xn, gates = _out_proj(x2, og, of, sga, sgb, wug, wuf, wo, nfw, wr, br,
                                 n_experts=n_experts, n_groups=n_groups, tm=tm)
        y = _moe(xn, gates, h, wg, wu, wd, fnw, tm=_pick_tile(x2.shape[0], 1024), n_groups=n_groups)
        return y, s_new, kv, lft.T

    def prompt_mixers(gin, loga, qf, kv, lft):
        kb, _, _, vtb = kv
        chunk = math.gcd(seq, GLA_CHUNK)
        n_chunks = _pick_tile(seq // chunk, 8)
        s0 = jnp.zeros((batch, g_heads, dk, dv), F32)
        og, s_new = _gla(gin, loga, s0, nw_gla, n_outer=batch, n_steps=seq // (chunk * n_chunks), n_seq=1,
                         n_chunks=n_chunks, chunk=chunk, mx_dtype=BF16, out_dtype=BF16)
        tp = _pick_tile(seq, 512)
        qat, ka = _fox_prep(qf, kb, lft, batch, seq, f_heads, dh, tp)
        of = _fox_flash(qat, ka, vtb, batch, seq, f_heads, dh, tp, hps=FLASH_HEADS_PER_STEP)
        return og, s_new, of

    y_p, s_p, (_, kt_p, vt_p, _), lf_p = token_path(x_prompt.reshape(batch * seq, d_model), prompt_mixers, seq)
    k_p = jnp.transpose(kt_p.reshape(batch, f_heads, dh, seq), (0, 3, 1, 2))
    v_p = jnp.transpose(vt_p.reshape(batch, f_heads, dh, seq), (0, 3, 1, 2))

    ck = jnp.transpose(cache_k[0], (0, 2, 3, 1)).reshape(n_phys, fw, page)
    cv = jnp.transpose(cache_v[0], (0, 2, 3, 1)).reshape(n_phys, fw, page)
    clf = jnp.transpose(cache_logf[0], (0, 2, 1))
    pt_flat = page_table.reshape(-1).astype(jnp.int32)

    def sample_mixers(gin, loga, qf, kv, lft):
        kf, vf = kv
        chunk = math.gcd(n_new, GLA_CHUNK)
        assert chunk == n_new
        gs = _pick_tile(n_seq, 8)
        og, s_new = _gla(gin, loga, state_gla[0], nw_gla, n_outer=n_seq // gs, n_steps=1, n_seq=gs,
                         n_chunks=1, chunk=chunk, mx_dtype=F32, out_dtype=F32)
        lfnt = jnp.transpose(lft.reshape(f_heads, n_seq, n_new), (1, 0, 2))
        lfnt_pad = jnp.pad(lfnt, ((0, 0), (0, 0), (0, LANES - n_new)))
        of = _fox_paged(pt_flat, qf, kf, vf, lfnt_pad, ck, cv, clf, n_seq=n_seq, n_new=n_new,
                        n_pages=n_pages, heads=f_heads, dh=dh, ppb=_pick_tile(n_pages, PAGES_PER_STEP))
        return og, s_new, of

    y_s, s_s, (k_s, v_s), lf_s = token_path(x_sample.reshape(n_seq * n_new, d_model), sample_mixers)

    return (y_p.reshape(batch, seq, d_model),
            y_s.reshape(n_seq, n_new, d_model),
            s_p[None],
            s_s[None],
            k_p.reshape(1, batch, seq, f_heads, dh),
            v_p.reshape(1, batch, seq, f_heads, dh),
            lf_p.reshape(1, batch, seq, f_heads),
            k_s.reshape(1, n_seq, n_new, f_heads, dh),
            v_s.reshape(1, n_seq, n_new, f_heads, dh),
            lf_s.reshape(1, n_seq, n_new, f_heads))
```

```python
import functools
import math

import jax
import jax.numpy as jnp
import numpy as np
from jax import lax
from jax.experimental import pallas as pl
from jax.experimental.pallas import tpu as pltpu

F32 = jnp.float32
BF16 = jnp.bfloat16

RMS_EPS = 1e-6
GLA_TAU = 16.0
GLA_CHUNK = 64
FLASH_HEADS_PER_STEP = 8
PAGES_PER_STEP = 32
MOE_CHUNK_ROWS = 288
MOE_EXPERTS_PER_STEP = 4
NEG = -1e30
LOG2E = 1.4426950408889634

V7X_VMEM_LIMIT_BYTES = 56 * 1024 * 1024
LANES = 128


def _cparams(sem):
    return pltpu.CompilerParams(dimension_semantics=sem, vmem_limit_bytes=V7X_VMEM_LIMIT_BYTES)


def _sigmoid(x):
    return 1.0 / (1.0 + jnp.exp(-x))


def _log_sigmoid(x):
    return jnp.minimum(x, 0.0) - jnp.log(1.0 + jnp.exp(-jnp.abs(x)))


def _rms(x, w):
    ms = jnp.mean(x * x, axis=-1, keepdims=True)
    return x * lax.rsqrt(ms + RMS_EPS) * w


def _split3(x):
    p1 = x.astype(BF16)
    r1 = x - p1.astype(F32)
    p2 = r1.astype(BF16)
    p3 = (r1 - p2.astype(F32)).astype(BF16)
    return p1, p2, p3


def _dot(a, b):
    return jnp.dot(a, b, preferred_element_type=F32)


def _dot_nt(a, b):
    return lax.dot_general(a, b, (((1,), (1,)), ((), ())), preferred_element_type=F32)


def _in_proj_kernel(x_ref, nw_ref, w_ref, w2_ref, b2_ref, bf_ref, gla_ref, loga_ref, qf_ref, *refs, dims, kv_t):
    n_gla, n_fox, d_model, n_lr, n_fh = dims
    sga_ref, sgb_ref, lft_ref = refs[-3:]
    xn = _rms(x_ref[...], nw_ref[...]).astype(BF16)

    def mm(lo, hi):
        return _dot(xn, w_ref[:, lo:hi])

    o = 0
    gla_ref[...] = mm(o, o + n_gla)
    o += n_gla
    fw = n_fox // 3
    qf_ref[...] = mm(o, o + fw)
    kf = mm(o + fw, o + 2 * fw)
    vf = mm(o + 2 * fw, o + 3 * fw)
    if kv_t:
        kb_ref, kt_ref, vt_ref, vtb_ref = refs[:4]
        kb_ref[...] = kf.astype(BF16)
        kt_ref[...] = kf.T
        v_t = vf.T
        vt_ref[...] = v_t
        v_tb = v_t.astype(BF16)
        for p in range(fw // LANES):
            vtb_ref[p] = v_tb[p * LANES:(p + 1) * LANES, :]
    else:
        kf_ref, vf_ref = refs[:2]
        kf_ref[...] = kf
        vf_ref[...] = vf
    o += n_fox
    sga_ref[...] = _sigmoid(mm(o, o + d_model))
    sgb_ref[...] = _sigmoid(mm(o + d_model, o + 2 * d_model))
    o += 2 * d_model
    small = mm(o, o + LANES)
    pre = _dot(small.astype(BF16), w2_ref[...]) + b2_ref[...]
    loga_ref[...] = _log_sigmoid(pre) * (1.0 / GLA_TAU)
    small_t = small.T
    lft_ref[...] = _log_sigmoid(small_t[n_lr:n_lr + n_fh, :] + bf_ref[...])


def _in_proj(x2, nw, w_all, w2p, b2, bfc, dims, tm, seq=None):
    n_gla, n_fox, d_model, n_lr, n_fh = dims
    t = x2.shape[0]
    fw = n_fox // 3
    n_qk = w2p.shape[1]
    row = lambda i: (i, 0)
    const = lambda i: (0, 0)
    if seq is None:
        kv_shapes = (jax.ShapeDtypeStruct((t, fw), F32),) * 2
        kv_specs = (pl.BlockSpec((tm, fw), row),) * 2
    else:
        nsb = seq // tm
        batch = t // seq
        kv_shapes = (
            jax.ShapeDtypeStruct((t, fw), BF16),
            jax.ShapeDtypeStruct((batch, fw, seq), F32),
            jax.ShapeDtypeStruct((batch, fw, seq), F32),
            jax.ShapeDtypeStruct((batch, fw // LANES, LANES, seq), BF16),
        )
        kv_specs = (
            pl.BlockSpec((tm, fw), row),
            pl.BlockSpec((None, fw, tm), lambda i: (i // nsb, 0, i % nsb)),
            pl.BlockSpec((None, fw, tm), lambda i: (i // nsb, 0, i % nsb)),
            pl.BlockSpec((None, fw // LANES, LANES, tm), lambda i: (i // nsb, 0, 0, i % nsb)),
        )
    out_shape = (
        jax.ShapeDtypeStruct((t, n_gla), F32),
        jax.ShapeDtypeStruct((t, n_qk), F32),
        jax.ShapeDtypeStruct((t, fw), F32),
        *kv_shapes,
        jax.ShapeDtypeStruct((t, d_model), F32),
        jax.ShapeDtypeStruct((t, d_model), F32),
        jax.ShapeDtypeStruct((n_fh, t), F32),
    )
    return pl.pallas_call(
        functools.partial(_in_proj_kernel, dims=dims, kv_t=seq is not None),
        grid=(t // tm,),
        in_specs=[
            pl.BlockSpec((tm, d_model), row),
            pl.BlockSpec((1, d_model), const),
            pl.BlockSpec(w_all.shape, const, pipeline_mode=pl.Buffered(1)),
            pl.BlockSpec(w2p.shape, const),
            pl.BlockSpec((1, n_qk), const),
            pl.BlockSpec((n_fh, 1), const),
        ],
        out_specs=(
            pl.BlockSpec((tm, n_gla), row),
            pl.BlockSpec((tm, n_qk), row),
            pl.BlockSpec((tm, fw), row),
            *kv_specs,
            pl.BlockSpec((tm, d_model), row),
            pl.BlockSpec((tm, d_model), row),
            pl.BlockSpec((n_fh, tm), lambda i: (0, i)),
        ),
        out_shape=out_shape,
        compiler_params=_cparams(("parallel",)),
        name="in_proj",
    )(x2, nw, w_all, w2p, b2, bfc)


def _gla_kernel(gin_ref, loga_ref, s0_ref, nw_ref, og_ref, sout_ref, s_scr, *,
                chunk, n_chunks, n_seq, heads, dk, dv, mx_dtype):
    j = pl.program_id(1)
    qk = heads * dk
    vw = heads * dv
    kpad = max(chunk, LANES)

    @pl.when(j == 0)
    def _():
        s_scr[...] = s0_ref[...]

    rows = n_seq * n_chunks * chunk
    shift = chunk.bit_length() - 1
    assert chunk == 1 << shift
    nw = nw_ref[...]

    rr = lax.broadcasted_iota(jnp.int32, (rows, rows), 0)
    cc = lax.broadcasted_iota(jnp.int32, (rows, rows), 1)
    same_chunk = (rr >> shift) == (cc >> shift)
    tri = jnp.where(same_chunk, jnp.where(cc <= rr, 1.0, 0.0), 0.0).astype(mx_dtype)
    gs = jnp.concatenate([p.astype(mx_dtype) for p in _split3(loga_ref[...])], axis=1)
    bs = _dot(tri, gs)
    b_all = bs[:, :qk] + bs[:, qk:2 * qk] + bs[:, 2 * qk:]
    k_all = gin_ref[:, qk:2 * qk]
    q_i_all = (gin_ref[:, 0:qk] * jnp.exp(b_all) * dk ** -0.5).astype(mx_dtype)
    k_i_all = k_all * jnp.exp(-b_all)
    vb_all = gin_ref[:, 2 * qk:2 * qk + vw].astype(mx_dtype)
    r_all = gin_ref[:, 2 * qk + vw:2 * qk + 2 * vw]
    silu_r = r_all * _sigmoid(r_all)

    rio = lax.broadcasted_iota(jnp.int32, (chunk, kpad), 0)
    cio = lax.broadcasted_iota(jnp.int32, (chunk, kpad), 1)
    tril = cio <= rio

    def pad_rows(a):
        if kpad == chunk:
            return a
        return jnp.concatenate([a, jnp.zeros((kpad - chunk, a.shape[1]), a.dtype)], axis=0)

    for idx in range(n_seq * n_chunks):
        seq = idx // n_chunks
        sl = slice(idx * chunk, (idx + 1) * chunk)
        b = b_all[sl]
        b_last = b[chunk - 1:chunk, :]
        q_i = q_i_all[sl]
        k_i = pad_rows(k_i_all[sl]).astype(mx_dtype)
        k_end_t = pad_rows(k_all[sl] * jnp.exp(b_last - b)).T.astype(mx_dtype)
        decay = jnp.exp(pad_rows(b).T[:, chunk - 1:chunk])
        vb = pad_rows(vb_all[sl])
        for h in range(heads):
            qh = q_i[:, h * dk:(h + 1) * dk]
            kh = k_i[:, h * dk:(h + 1) * dk]
            vh = vb[:, h * dv:(h + 1) * dv]
            a = jnp.where(tril, _dot_nt(qh, kh), 0.0).astype(mx_dtype)
            s_old = s_scr[seq, h]
            o = _dot(a, vh) + _dot(qh, s_old.astype(mx_dtype))
            kv = _dot(k_end_t[h * dk:(h + 1) * dk, :], vh)
            s_scr[seq, h] = decay[h * dk:(h + 1) * dk, :] * s_old + kv
            og_ref[sl, h * dv:(h + 1) * dv] = (_rms(o, nw) * silu_r[sl, h * dv:(h + 1) * dv]).astype(og_ref.dtype)

    @pl.when(j == pl.num_programs(1) - 1)
    def _():
        sout_ref[...] = s_scr[...]


def _gla(gin, loga, s0, nw, *, n_outer, n_steps, n_seq, n_chunks, chunk, mx_dtype, out_dtype):
    t = gin.shape[0]
    _, heads, dk, dv = s0.shape
    rows = n_seq * n_chunks * chunk
    assert t == n_outer * n_steps * rows
    kern = functools.partial(_gla_kernel, chunk=chunk, n_chunks=n_chunks, n_seq=n_seq, heads=heads,
                             dk=dk, dv=dv, mx_dtype=mx_dtype)
    tok = lambda i, j: (i * n_steps + j, 0)
    st = lambda i, j: (i, 0, 0, 0)
    return pl.pallas_call(
        kern,
        grid=(n_outer, n_steps),
        in_specs=[
            pl.BlockSpec((rows, gin.shape[1]), tok),
            pl.BlockSpec((rows, loga.shape[1]), tok),
            pl.BlockSpec((n_seq, heads, dk, dv), st),
            pl.BlockSpec((1, dv), lambda i, j: (0, 0)),
        ],
        out_specs=(
            pl.BlockSpec((rows, heads * dv), tok),
            pl.BlockSpec((n_seq, heads, dk, dv), st),
        ),
        out_shape=(
            jax.ShapeDtypeStruct((t, heads * dv), out_dtype),
            jax.ShapeDtypeStruct(s0.shape, F32),
        ),
        scratch_shapes=[pltpu.VMEM((n_seq, heads, dk, dv), F32)],
        compiler_params=_cparams(("parallel", "arbitrary")),
        name="gla",
    )(gin, loga, s0, nw)


def _fox_prep_kernel(qf_ref, kf_ref, lft_ref, rq_ref, rk_ref, qat_ref, ka_ref, carry_scr, *, heads, dh):
    j = pl.program_id(1)
    tp = qf_ref.shape[0]

    @pl.when(j == 0)
    def _():
        carry_scr[...] = jnp.zeros_like(carry_scr)

    rio = lax.broadcasted_iota(jnp.int32, (tp, tp), 0)
    cio = lax.broadcasted_iota(jnp.int32, (tp, tp), 1)
    upper = jnp.where(rio <= cio, 1.0, 0.0).astype(BF16)
    l1, l2, l3 = _split3(lft_ref[...])
    cs = _dot(jnp.concatenate([l1, l2, l3, jnp.zeros_like(l1)], axis=0), upper)
    c_t = cs[0:heads] + cs[heads:2 * heads] + cs[2 * heads:3 * heads] + carry_scr[:, 0:1]
    carry_scr[...] = jnp.broadcast_to(c_t[:, tp - 1:tp], carry_scr.shape)
    c1, c2, c3 = _split3(c_t * LOG2E)
    pad_rows = LANES - 3 * heads - 8
    aug_t = jnp.concatenate([c1.astype(F32), c2.astype(F32), c3.astype(F32),
                             jnp.ones((8, tp), F32), jnp.zeros((pad_rows, tp), F32)], axis=0)
    aug = aug_t.T.astype(BF16)
    lq_t = jnp.concatenate([(qf_ref[...].T * (dh ** -0.5 * LOG2E)).astype(BF16), aug_t.astype(BF16)], axis=0)
    lk = jnp.concatenate([kf_ref[...].astype(BF16), aug], axis=1)
    qa_t = _dot(rq_ref[...], lq_t).astype(BF16)
    ka = _dot(lk, rk_ref[...]).astype(BF16)
    for h in range(heads):
        qat_ref[h] = qa_t[h * LANES:(h + 1) * LANES, :]
        ka_ref[h] = ka[:, h * LANES:(h + 1) * LANES]


def _placement_matrices(heads, dh):
    kin = heads * dh + LANES
    rq = np.zeros((kin, heads * LANES), np.float32)
    rk = np.zeros((kin, heads * LANES), np.float32)
    base = heads * dh
    ones_row = base + 3 * heads
    for h in range(heads):
        for d in range(dh):
            rq[h * dh + d, h * LANES + d] = 1.0
            rk[h * dh + d, h * LANES + d] = 1.0
        for p in range(3):
            rq[base + p * heads + h, h * LANES + dh + p] = 1.0
            rq[ones_row, h * LANES + dh + 3 + p] = 1.0
            rk[ones_row, h * LANES + dh + p] = 1.0
            rk[base + p * heads + h, h * LANES + dh + 3 + p] = -1.0
    return jnp.asarray(rq.T, BF16), jnp.asarray(rk, BF16)


def _fox_prep(qf, kf, lft, batch, seq, heads, dh, tp):
    rq, rk = _placement_matrices(heads, dh)
    n_steps = seq // tp
    tok = lambda b, j: (b * n_steps + j, 0)
    const = lambda b, j: (0, 0)
    return pl.pallas_call(
        functools.partial(_fox_prep_kernel, heads=heads, dh=dh),
        grid=(batch, n_steps),
        in_specs=[
            pl.BlockSpec((tp, heads * dh), tok),
            pl.BlockSpec((tp, heads * dh), tok),
            pl.BlockSpec((heads, tp), lambda b, j: (0, b * n_steps + j)),
            pl.BlockSpec(rq.shape, const),
            pl.BlockSpec(rk.shape, const),
        ],
        out_specs=(
            pl.BlockSpec((None, heads, LANES, tp), lambda b, j: (b, 0, 0, j)),
            pl.BlockSpec((None, heads, tp, LANES), lambda b, j: (b, 0, j, 0)),
        ),
        out_shape=(
            jax.ShapeDtypeStruct((batch, heads, LANES, seq), BF16),
            jax.ShapeDtypeStruct((batch, heads, seq, LANES), BF16),
        ),
        scratch_shapes=[pltpu.VMEM((heads, LANES), F32)],
        compiler_params=_cparams(("parallel", "arbitrary")),
        name="fox_prep",
    )(qf, kf, lft, rq, rk)


def _fox_flash_kernel(qi_ref, ki_ref, qat_ref, ka_ref, vt_ref, o_ref, m_scr, l_scr, acc_scr, s0_scr, s1_scr, *,
                      dh, hps):
    step = pl.program_id(2)
    qi = qi_ref[step]
    ki = ki_ref[step]
    tq = qat_ref.shape[2]
    tk = ka_ref.shape[1]
    hpl = LANES // dh

    @pl.when(ki == 0)
    def _():
        m_scr[...] = jnp.full_like(m_scr, NEG)
        l_scr[...] = jnp.zeros_like(l_scr)
        acc_scr[...] = jnp.zeros_like(acc_scr)

    def update(masked):
        def scores(h, buf):
            buf[...] = _dot(ka_ref[h], qat_ref[h])

        def soft_pv(h, buf):
            s = buf[...]
            if masked:
                rio = lax.broadcasted_iota(jnp.int32, (tk, tq), 0)
                cio = lax.broadcasted_iota(jnp.int32, (tk, tq), 1)
                s = jnp.where(rio <= cio, s, NEG)
            m_prev = m_scr[h]
            m_new = jnp.maximum(m_prev, jnp.max(s, axis=0, keepdims=True))
            a = jnp.exp2(m_prev - m_new)
            p = jnp.exp2(s - m_new)
            l_scr[h] = a * l_scr[h] + jnp.sum(p, axis=0, keepdims=True)
            v_t = vt_ref[h // hpl, pl.ds(pl.multiple_of((h % hpl) * dh, dh), dh), :]
            acc_scr[h] = a * acc_scr[h] + _dot(v_t, p.astype(BF16))
            m_scr[h] = m_new

        scores(0, s0_scr)

        def pair(i, carry):
            scores(2 * i + 1, s1_scr)
            soft_pv(2 * i, s0_scr)
            scores(jnp.minimum(2 * i + 2, hps - 1), s0_scr)
            soft_pv(2 * i + 1, s1_scr)
            return carry

        lax.fori_loop(0, hps // 2, pair, 0)

    @pl.when(ki < qi)
    def _():
        update(False)

    @pl.when(ki == qi)
    def _():
        update(True)
        o_t = jnp.concatenate([acc_scr[h] / l_scr[h] for h in range(hps)], axis=0)
        o_ref[...] = o_t.T.astype(o_ref.dtype)


def _fox_flash(qat, ka, vt, batch, seq, heads, dh, tq, hps):
    nq = seq // tq
    qi_l, ki_l = [], []
    for q in range(nq):
        for k in range(q + 1):
            qi_l.append(q)
            ki_l.append(k)
    qi_a = jnp.asarray(np.array(qi_l, np.int32))
    ki_a = jnp.asarray(np.array(ki_l, np.int32))
    n_tri = len(qi_l)
    hpl = LANES // dh
    grid_spec = pltpu.PrefetchScalarGridSpec(
        num_scalar_prefetch=2,
        grid=(batch, heads // hps, n_tri),
        in_specs=[
            pl.BlockSpec((None, hps, LANES, tq), lambda b, g, s, qi, ki: (b, g, 0, qi[s])),
            pl.BlockSpec((None, hps, tq, LANES), lambda b, g, s, qi, ki: (b, g, ki[s], 0)),
            pl.BlockSpec((None, hps // hpl, LANES, tq), lambda b, g, s, qi, ki: (b, g, 0, ki[s])),
        ],
        out_specs=pl.BlockSpec((tq, hps * dh), lambda b, g, s, qi, ki: (b * nq + qi[s], g)),
        scratch_shapes=[
            pltpu.VMEM((hps, 1, tq), F32),
            pltpu.VMEM((hps, 1, tq), F32),
            pltpu.VMEM((hps, dh, tq), F32),
            pltpu.VMEM((tq, tq), F32),
            pltpu.VMEM((tq, tq), F32),
        ],
    )
    return pl.pallas_call(
        functools.partial(_fox_flash_kernel, dh=dh, hps=hps),
        grid_spec=grid_spec,
        out_shape=jax.ShapeDtypeStruct((batch * seq, heads * dh), BF16),
        compiler_params=_cparams(("parallel", "parallel", "arbitrary")),
        name="fox_flash",
    )(qi_a, ki_a, qat, ka, vt)


def _fox_paged_kernel(pt_ref, q_ref, kn_ref, vn_ref, lfnt_ref, *refs, heads, dh, n_new, ppb, page):
    k_refs = refs[0:ppb]
    v_refs = refs[ppb:2 * ppb]
    lf_refs = refs[2 * ppb:3 * ppb]
    o_ref = refs[3 * ppb]
    qbd_scr, cncol_scr, cns_scr, base_scr, m_scr, l_scr, acc_scr = refs[3 * ppb + 1:]
    j = pl.program_id(1)
    rows = heads * n_new
    width = heads * dh

    rr = lax.broadcasted_iota(jnp.int32, (rows, width), 0)
    cc = lax.broadcasted_iota(jnp.int32, (rows, width), 1)
    bd_mask = (rr // n_new) == (cc // dh)

    @pl.when(j == 0)
    def _():
        q = q_ref[...] * dh ** -0.5
        q_rep = jnp.broadcast_to(q[None], (heads, n_new, width)).reshape(rows, width)
        qbd_scr[...] = jnp.where(bd_mask, q_rep, 0.0).astype(BF16)
        lfnt = lfnt_ref[...]
        lane = lax.broadcasted_iota(jnp.int32, lfnt.shape, 1)
        cnt = jnp.zeros_like(lfnt)
        for s in range(n_new):
            col = jnp.sum(jnp.where(lane <= s, lfnt, 0.0), axis=1, keepdims=True)
            cnt = cnt + jnp.where(lane == s, col, 0.0)
        cns = jnp.broadcast_to(cnt[:, None, :], (heads, n_new, LANES)).reshape(rows, LANES)
        cns_scr[...] = cns
        r2 = lax.broadcasted_iota(jnp.int32, (rows, LANES), 0)
        l2 = lax.broadcasted_iota(jnp.int32, (rows, LANES), 1)
        cncol_scr[...] = jnp.sum(jnp.where(l2 == r2 % n_new, cns, 0.0), axis=1, keepdims=True)
        base_scr[...] = jnp.zeros_like(base_scr)
        m_scr[...] = jnp.full_like(m_scr, NEG)
        l_scr[...] = jnp.zeros_like(l_scr)
        acc_scr[...] = jnp.zeros_like(acc_scr)

    rio = lax.broadcasted_iota(jnp.int32, (page, page), 0)
    cio = lax.broadcasted_iota(jnp.int32, (page, page), 1)
    after = jnp.where(rio > cio, 1.0, 0.0).astype(BF16)
    lf_all = jnp.concatenate([lf_refs[i][...] for i in range(ppb)], axis=0)
    ss = _dot(jnp.concatenate(_split3(lf_all), axis=0), after)
    nr = ppb * heads
    suf_all = ss[0:nr] + ss[nr:2 * nr] + ss[2 * nr:3 * nr]
    tot_all = suf_all[:, 0:1] + lf_all[:, 0:1]
    base = base_scr[:, 0:1]
    biases = []
    for i in range(ppb):
        biases.append(base + suf_all[i * heads:(i + 1) * heads])
        base = base + tot_all[i * heads:(i + 1) * heads]
    base_scr[...] = jnp.broadcast_to(base, base_scr.shape)
    bias = jnp.concatenate(biases, axis=1)
    bias_rows = jnp.broadcast_to(bias[:, None, :], (heads, n_new, ppb * page)).reshape(rows, ppb * page)

    kt = jnp.concatenate([k_refs[i][...].astype(BF16) for i in range(ppb)], axis=1)
    vt = jnp.concatenate([v_refs[i][...].astype(BF16) for i in range(ppb)], axis=1)
    qbd = qbd_scr[...]
    s = _dot(qbd, kt) + cncol_scr[...] + bias_rows
    m_prev = m_scr[...]
    m_new = jnp.maximum(m_prev, jnp.max(s, axis=1, keepdims=True))
    a = jnp.exp(m_prev - m_new)
    p = jnp.exp(s - m_new)
    l_scr[...] = a * l_scr[...] + jnp.sum(p, axis=1, keepdims=True)
    acc_scr[...] = a * acc_scr[...] + _dot_nt(p.astype(BF16), vt)
    m_scr[...] = m_new

    @pl.when(j == pl.num_programs(1) - 1)
    def _():
        pad = jnp.zeros((LANES - n_new, width), F32)
        kn = jnp.concatenate([kn_ref[...], pad], axis=0).astype(BF16)
        vn = jnp.concatenate([vn_ref[...], pad], axis=0).astype(BF16)
        r2 = lax.broadcasted_iota(jnp.int32, (rows, LANES), 0)
        l2 = lax.broadcasted_iota(jnp.int32, (rows, LANES), 1)
        sn = _dot_nt(qbd, kn) + cncol_scr[...] - cns_scr[...]
        sn = jnp.where(l2 <= r2 % n_new, sn, NEG)
        m_p = m_scr[...]
        m_n = jnp.maximum(m_p, jnp.max(sn, axis=1, keepdims=True))
        a2 = jnp.exp(m_p - m_n)
        pn = jnp.exp(sn - m_n)
        l_f = a2 * l_scr[...] + jnp.sum(pn, axis=1, keepdims=True)
        acc_f = a2 * acc_scr[...] + _dot(pn.astype(BF16), vn)
        o = jnp.where(bd_mask, acc_f / l_f, 0.0)
        o_ref[...] = jnp.sum(o.reshape(heads, n_new, width), axis=0)


def _fox_paged(pt_flat, qf, kf, vf, lfnt_pad, ck, cv, clf, *, n_seq, n_new, n_pages, heads, dh, ppb):
    page = ck.shape[-1]
    width = heads * dh
    rows = heads * n_new
    n_steps = n_pages // ppb

    def page_map(i):
        return lambda b, j, pt: (pt[b * n_pages + (n_pages - 1 - (j * ppb + i))], 0, 0)

    tok = pl.BlockSpec((n_new, width), lambda b, j, pt: (b, 0))
    in_specs = [tok, tok, tok, pl.BlockSpec((None, heads, LANES), lambda b, j, pt: (b, 0, 0))]
    in_specs += [pl.BlockSpec((None, width, page), page_map(i)) for i in range(ppb)]
    in_specs += [pl.BlockSpec((None, width, page), page_map(i)) for i in range(ppb)]
    in_specs += [pl.BlockSpec((None, heads, page), page_map(i)) for i in range(ppb)]
    grid_spec = pltpu.PrefetchScalarGridSpec(
        num_scalar_prefetch=1,
        grid=(n_seq, n_steps),
        in_specs=in_specs,
        out_specs=pl.BlockSpec((n_new, width), lambda b, j, pt: (b, 0)),
        scratch_shapes=[
            pltpu.VMEM((rows, width), BF16),
            pltpu.VMEM((rows, 1), F32),
            pltpu.VMEM((rows, LANES), F32),
            pltpu.VMEM((heads, LANES), F32),
            pltpu.VMEM((rows, 1), F32),
            pltpu.VMEM((rows, 1), F32),
            pltpu.VMEM((rows, width), F32),
        ],
    )
    kern = functools.partial(_fox_paged_kernel, heads=heads, dh=dh, n_new=n_new, ppb=ppb, page=page)
    return pl.pallas_call(
        kern,
        grid_spec=grid_spec,
        out_shape=jax.ShapeDtypeStruct((n_seq * n_new, width), F32),
        compiler_params=_cparams(("parallel", "arbitrary")),
        name="fox_paged",
    )(pt_flat, qf, kf, vf, lfnt_pad, *([ck] * ppb), *([cv] * ppb), *([clf] * ppb))


def _out_proj_kernel(x_ref, og_ref, of_ref, sga_ref, sgb_ref, wug_ref, wuf_ref, wo_ref, nfw_ref, wr_ref, br_ref,
                     h_ref, xn_ref, gates_ref, *, n_experts, n_groups):
    up_a = _dot(og_ref[...].astype(BF16), wug_ref[...])
    up_b = _dot(of_ref[...].astype(BF16), wuf_ref[...])
    merged = sga_ref[...] * up_a + sgb_ref[...] * up_b
    h = x_ref[...] + _dot(merged.astype(BF16), wo_ref[...])
    h_ref[...] = h
    xn = _rms(h, nfw_ref[...])
    xn_ref[...] = xn.astype(BF16)

    x_hi = xn.astype(BF16)
    x_lo = (xn - x_hi.astype(F32)).astype(BF16)
    l_hi = _dot(x_hi, wr_ref[...])
    logits = l_hi[:, :LANES] + l_hi[:, LANES:] + _dot(x_lo, wr_ref[:, :LANES]) + br_ref[...]
    lane = lax.broadcasted_iota(jnp.int32, logits.shape, 1).astype(F32)
    epg = n_experts // n_groups
    big = 4.0 * LANES
    is_g = (lane >= n_experts) & (lane < n_experts + n_groups)
    gl = jnp.where(is_g, logits, NEG)
    gmax = jnp.max(gl, axis=1, keepdims=True)
    gsum = jnp.sum(jnp.exp(gl - gmax), axis=1, keepdims=True)
    p_g = 1.0 / gsum
    g_sel = jnp.min(jnp.where(gl == gmax, lane, big), axis=1, keepdims=True) - n_experts
    e_lo = g_sel * epg
    in_grp = (lane >= e_lo) & (lane < e_lo + epg)
    el = jnp.where(in_grp, logits, NEG)
    emax = jnp.max(el, axis=1, keepdims=True)
    ee = jnp.exp(el - emax)
    pe = ee / jnp.sum(ee, axis=1, keepdims=True)
    cand = jnp.where(in_grp, pe, -1.0)
    v1 = jnp.max(cand, axis=1, keepdims=True)
    i1 = jnp.min(jnp.where(cand == v1, lane, big), axis=1, keepdims=True)
    cand2 = jnp.where(lane == i1, -1.0, cand)
    v2 = jnp.max(cand2, axis=1, keepdims=True)
    i2 = jnp.min(jnp.where(cand2 == v2, lane, big), axis=1, keepdims=True)
    tot = v1 + v2
    gates_ref[...] = (jnp.where(lane == i1, p_g * (v1 / tot), 0.0) + jnp.where(lane == i2, p_g * (v2 / tot), 0.0)
                      + jnp.where(lane == n_experts, g_sel, 0.0))


def _out_proj(x2, og, of, sga, sgb, wug, wuf, wo, nfw, wr, br, *, n_experts, n_groups, tm):
    t, d_model = x2.shape
    row = lambda i: (i, 0)
    const = lambda i: (0, 0)
    return pl.pallas_call(
        functools.partial(_out_proj_kernel, n_experts=n_experts, n_groups=n_groups),
        grid=(t // tm,),
        in_specs=[
            pl.BlockSpec((tm, d_model), row),
            pl.BlockSpec((tm, og.shape[1]), row),
            pl.BlockSpec((tm, of.shape[1]), row),
            pl.BlockSpec((tm, d_model), row),
            pl.BlockSpec((tm, d_model), row),
            pl.BlockSpec(wug.shape, const, pipeline_mode=pl.Buffered(1)),
            pl.BlockSpec(wuf.shape, const, pipeline_mode=pl.Buffered(1)),
            pl.BlockSpec(wo.shape, const, pipeline_mode=pl.Buffered(1)),
            pl.BlockSpec((1, d_model), const),
            pl.BlockSpec(wr.shape, const),
            pl.BlockSpec((1, LANES), const),
        ],
        out_specs=(
            pl.BlockSpec((tm, d_model), row),
            pl.BlockSpec((tm, d_model), row),
            pl.BlockSpec((tm, LANES), row),
        ),
        out_shape=(
            jax.ShapeDtypeStruct((t, d_model), F32),
            jax.ShapeDtypeStruct((t, d_model), BF16),
            jax.ShapeDtypeStruct((t, LANES), F32),
        ),
        compiler_params=_cparams(("parallel",)),
        name="out_proj",
    )(x2, og, of, sga, sgb, wug, wuf, wo, nfw, wr, br)


def _moe_kernel(xn_ref, gates_ref, h_ref, wg_ref, wu_ref, wd_ref, fnw_ref, y_ref,
                xs_scr, ys_scr, gs_scr, selt_scr, off_smem, *, n_experts, n_groups, chunk, eps):
    e = pl.program_id(1)
    tm = xn_ref.shape[0]
    rows = selt_scr.shape[1]
    rows_p = xs_scr.shape[0]
    epg = n_experts // n_groups

    @pl.when(e == 0)
    def _():
        gates = gates_ref[...]
        lane = lax.broadcasted_iota(jnp.int32, (tm, LANES), 1).astype(F32)
        g_sel = gates[:, n_experts:n_experts + 1]
        in_g = lane == g_sel
        onehot = jnp.where(in_g, 1.0, 0.0).astype(BF16)
        rio = lax.broadcasted_iota(jnp.int32, (tm, tm), 0)
        cio = lax.broadcasted_iota(jnp.int32, (tm, tm), 1)
        tril = jnp.where(cio <= rio, 1.0, 0.0).astype(BF16)
        cum = _dot(tril, onehot)
        cnt = cum[tm - 1:tm, :].astype(jnp.int32)
        lane1 = lax.broadcasted_iota(jnp.int32, (1, LANES), 1)
        off = jnp.int32(0)
        off_vec = jnp.zeros((1, LANES), F32)
        for g in range(n_groups):
            off_smem[g] = off
            off_vec = off_vec + jnp.where(lane1 == g, off.astype(F32), 0.0)
            off = off + ((cnt[0, g] + 7) // 8) * 8
        off_smem[n_groups] = off
        pos = jnp.sum(jnp.where(in_g, off_vec + cum - 1.0, 0.0), axis=1, keepdims=True)
        r_lane = lax.broadcasted_iota(jnp.int32, (tm, rows), 1).astype(F32)
        selt_scr[...] = jnp.where(r_lane == pos, 1.0, 0.0).astype(BF16)
        pos_t = jnp.broadcast_to(pos, (tm, LANES)).T[0:1, :]
        r_sub = lax.broadcasted_iota(jnp.int32, (rows, tm), 0).astype(F32)
        sel = jnp.where(r_sub == pos_t, 1.0, 0.0).astype(BF16)
        xs_scr[0:rows, :] = _dot(sel, xn_ref[...])
        g1, g2, g3 = _split3(gates)
        gs = _dot(sel, jnp.concatenate([g1, g2, g3], axis=1))
        gs_scr[0:rows, :] = gs[:, :LANES] + gs[:, LANES:2 * LANES] + gs[:, 2 * LANES:]
        xs_scr[rows:rows_p, :] = jnp.zeros((rows_p - rows, xs_scr.shape[1]), F32)
        gs_scr[rows:rows_p, :] = jnp.zeros((rows_p - rows, LANES), F32)
        ys_scr[...] = jnp.zeros_like(ys_scr)

    g = (e * eps) // epg
    start = off_smem[g]
    n_rows = off_smem[g + 1] - start
    lane_c = lax.broadcasted_iota(jnp.int32, (chunk, LANES), 1)

    def body(c, carry):
        r0 = pl.multiple_of(start + c * chunk, 8)
        x = xs_scr[pl.ds(r0, chunk), :].astype(BF16)
        gs = gs_scr[pl.ds(r0, chunk), :]
        acc = None
        for i in range(eps):
            gg = _dot(x, wg_ref[i])
            uu = _dot(x, wu_ref[i])
            hid = (gg * _sigmoid(gg)) * uu
            out = _dot(hid.astype(BF16), wd_ref[i])
            gate = jnp.sum(jnp.where(lane_c == e * eps + i, gs, 0.0), axis=1, keepdims=True)
            acc = gate * out if acc is None else acc + gate * out
        ys_scr[pl.ds(r0, chunk), :] += acc
        return carry

    lax.fori_loop(0, (n_rows + chunk - 1) // chunk, body, 0)

    @pl.when(e == pl.num_programs(1) - 1)
    def _():
        ys = ys_scr[0:rows, :]
        hi = ys.astype(BF16)
        lo = (ys - hi.astype(F32)).astype(BF16)
        selt = selt_scr[...]
        moe = _dot(selt, hi) + _dot(selt, lo)
        y_ref[...] = _rms(h_ref[...] + moe, fnw_ref[...])


def _moe(xn, gates, h, wg, wu, wd, fnw, *, tm, n_groups, chunk=MOE_CHUNK_ROWS, eps=MOE_EXPERTS_PER_STEP):
    t, d_model = h.shape
    n_experts, _, d_exp = wg.shape
    rows = tm + LANES
    rows_p = rows + chunk
    row = lambda i, e: (i, 0)
    once = pl.Buffered(1)
    assert (n_experts // n_groups) % eps == 0
    kern = functools.partial(_moe_kernel, n_experts=n_experts, n_groups=n_groups, chunk=chunk, eps=eps)
    return pl.pallas_call(
        kern,
        grid=(t // tm, n_experts // eps),
        in_specs=[
            pl.BlockSpec((tm, d_model), row, pipeline_mode=once),
            pl.BlockSpec((tm, LANES), row, pipeline_mode=once),
            pl.BlockSpec((tm, d_model), row, pipeline_mode=once),
            pl.BlockSpec((eps, d_model, d_exp), lambda i, e: (e, 0, 0)),
            pl.BlockSpec((eps, d_model, d_exp), lambda i, e: (e, 0, 0)),
            pl.BlockSpec((eps, d_exp, d_model), lambda i, e: (e, 0, 0)),
            pl.BlockSpec((1, d_model), lambda i, e: (0, 0)),
        ],
        out_specs=pl.BlockSpec((tm, d_model), row),
        out_shape=jax.ShapeDtypeStruct((t, d_model), F32),
        scratch_shapes=[
            pltpu.VMEM((rows_p, d_model), F32),
            pltpu.VMEM((rows_p, d_model), F32),
            pltpu.VMEM((rows_p, LANES), F32),
            pltpu.VMEM((tm, rows), BF16),
            pltpu.SMEM((n_groups + 1,), jnp.int32),
        ],
        compiler_params=_cparams(("parallel", "arbitrary")),
        name="moe",
    )(xn, gates, h, wg, wu, wd, fnw)


def _pick_tile(n, pref):
    t = min(n, pref)
    while n % t:
        t //= 2
    return t


def kernel(x_prompt, x_sample, state_gla, cache_k, cache_v, cache_logf, page_table, norm_mix_w, w_in, w_gla_a2,
           b_gla_a2, gla_norm_w, b_fox_f, w_up_gla, w_up_fox, w_out, norm_ffn_w, w_router_group, b_router_group,
           w_router_expert, b_router_expert, w_exp_gate, w_exp_up, w_exp_down, final_norm_w):
    depth = w_in.shape[0]
    assert depth == 1
    batch, seq, d_model = x_prompt.shape
    n_seq, n_new, _ = x_sample.shape
    _, _, g_heads, dk, dv = state_gla.shape
    _, n_phys, page, f_heads, dh = cache_k.shape
    n_pages = page_table.shape[1]
    n_lr = w_gla_a2.shape[1]
    n_groups = w_router_group.shape[2]
    n_experts = w_router_expert.shape[2]
    qk = g_heads * dk
    vw = g_heads * dv
    fw = f_heads * dh
    n_gla = 2 * qk + 2 * vw
    n_fox = 3 * fw
    dims = (n_gla, n_fox, d_model, n_lr, f_heads)

    wi = w_in[0]
    o_za = n_gla
    o_fox = o_za + n_lr
    o_fp = o_fox + n_fox
    o_gate = o_fp + f_heads
    w_small = jnp.concatenate([wi[:, o_za:o_fox], wi[:, o_fp:o_gate],
                               jnp.zeros((d_model, LANES - n_lr - f_heads), wi.dtype)], axis=1)
    w_all = jnp.concatenate([wi[:, :o_za], wi[:, o_fox:o_fp], wi[:, o_gate:], w_small], axis=1).astype(BF16)
    w2p = jnp.concatenate([w_gla_a2[0], jnp.zeros((LANES - n_lr, qk), F32)], axis=0).astype(BF16)
    b2 = b_gla_a2[0].reshape(1, qk)
    bfc = b_fox_f[0].reshape(f_heads, 1)
    nw_mix = norm_mix_w[0].reshape(1, d_model)
    nw_gla = gla_norm_w[0].reshape(1, dv)
    wug = w_up_gla[0].astype(BF16)
    wuf = w_up_fox[0].astype(BF16)
    wo = w_out[0].astype(BF16)
    nfw = norm_ffn_w[0].reshape(1, d_model)
    wr32 = jnp.concatenate([w_router_expert[0], w_router_group[0],
                            jnp.zeros((d_model, LANES - n_experts - n_groups), F32)], axis=1)
    wr_hi = wr32.astype(BF16)
    wr = jnp.concatenate([wr_hi, (wr32 - wr_hi.astype(F32)).astype(BF16)], axis=1)
    br = jnp.concatenate([b_router_expert[0], b_router_group[0],
                          jnp.zeros((LANES - n_experts - n_groups,), F32)]).reshape(1, LANES)
    wg = w_exp_gate[0].astype(BF16)
    wu = w_exp_up[0].astype(BF16)
    wd = w_exp_down[0].astype(BF16)
    fnw = final_norm_w.reshape(1, d_model)

    def token_path(x2, mixers, seq_len=None):
        tm = _pick_tile(x2.shape[0] if seq_len is None else seq_len, 512)
        gin, loga, qf, *kv, sga, sgb, lft = _in_proj(x2, nw_mix, w_all, w2p, b2, bfc, dims, tm, seq=seq_len)
        og, s_new, of = mixers(gin, loga, qf, kv, lft)
        h, xn, gates = _out_proj(x2, og, of, sga, sgb, wug, wuf, wo, nfw, wr, br,
                                 n_experts=n_experts, n_groups=n_groups, tm=tm)
        y = _moe(xn, gates, h, wg, wu, wd, fnw, tm=_pick_tile(x2.shape[0], 1024), n_groups=n_groups)
        return y, s_new, kv, lft.T

    def prompt_mixers(gin, loga, qf, kv, lft):
        kb, _, _, vtb = kv
        chunk = math.gcd(seq, GLA_CHUNK)
        n_chunks = _pick_tile(seq // chunk, 8)
        s0 = jnp.zeros((batch, g_heads, dk, dv), F32)
        og, s_new = _gla(gin, loga, s0, nw_gla, n_outer=batch, n_steps=seq // (chunk * n_chunks), n_seq=1,
                         n_chunks=n_chunks, chunk=chunk, mx_dtype=BF16, out_dtype=BF16)
        tp = _pick_tile(seq, 512)
        qat, ka = _fox_prep(qf, kb, lft, batch, seq, f_heads, dh, tp)
        of = _fox_flash(qat, ka, vtb, batch, seq, f_heads, dh, tp, hps=FLASH_HEADS_PER_STEP)
        return og, s_new, of

    y_p, s_p, (_, kt_p, vt_p, _), lf_p = token_path(x_prompt.reshape(batch * seq, d_model), prompt_mixers, seq)
    k_p = jnp.transpose(kt_p.reshape(batch, f_heads, dh, seq), (0, 3, 1, 2))
    v_p = jnp.transpose(vt_p.reshape(batch, f_heads, dh, seq), (0, 3, 1, 2))

    ck = jnp.transpose(cache_k[0], (0, 2, 3, 1)).reshape(n_phys, fw, page)
    cv = jnp.transpose(cache_v[0], (0, 2, 3, 1)).reshape(n_phys, fw, page)
    clf = jnp.transpose(cache_logf[0], (0, 2, 1))
    pt_flat = page_table.reshape(-1).astype(jnp.int32)

    def sample_mixers(gin, loga, qf, kv, lft):
        kf, vf = kv
        chunk = math.gcd(n_new, GLA_CHUNK)
        assert chunk == n_new
        gs = _pick_tile(n_seq, 8)
        og, s_new = _gla(gin, loga, state_gla[0], nw_gla, n_outer=n_seq // gs, n_steps=1, n_seq=gs,
                         n_chunks=1, chunk=chunk, mx_dtype=F32, out_dtype=F32)
        lfnt = jnp.transpose(lft.reshape(f_heads, n_seq, n_new), (1, 0, 2))
        lfnt_pad = jnp.pad(lfnt, ((0, 0), (0, 0), (0, LANES - n_new)))
        of = _fox_paged(pt_flat, qf, kf, vf, lfnt_pad, ck, cv, clf, n_seq=n_seq, n_new=n_new,
                        n_pages=n_pages, heads=f_heads, dh=dh, ppb=_pick_tile(n_pages, PAGES_PER_STEP))
        return og, s_new, of

    y_s, s_s, (k_s, v_s), lf_s = token_path(x_sample.reshape(n_seq * n_new, d_model), sample_mixers)

    return (y_p.reshape(batch, seq, d_model),
            y_s.reshape(n_seq, n_new, d_model),
            s_p[None],
            s_s[None],
            k_p.reshape(1, batch, seq, f_heads, dh),
            v_p.reshape(1, batch, seq, f_heads, dh),
            lf_p.reshape(1, batch, seq, f_heads),
            k_s.reshape(1, n_seq, n_new, f_heads, dh),
            v_s.reshape(1, n_seq, n_new, f_heads, dh),
            lf_s.reshape(1, n_seq, n_new, f_heads))
```

```python
import functools
import math

import jax
import jax.numpy as jnp
import numpy as np
from jax import lax
from jax.experimental import pallas as pl
from jax.experimental.pallas import tpu as pltpu

F32 = jnp.float32
BF16 = jnp.bfloat16

RMS_EPS = 1e-6
GLA_TAU = 16.0
GLA_CHUNK = 64
PAGES_PER_STEP = 32
MOE_CHUNK_ROWS = 288
MOE_EXPERTS_PER_STEP = 4
NEG = -1e30
LOG2E = 1.4426950408889634

V7X_VMEM_LIMIT_BYTES = 56 * 1024 * 1024
LANES = 128


def _cparams(sem):
    return pltpu.CompilerParams(dimension_semantics=sem, vmem_limit_bytes=V7X_VMEM_LIMIT_BYTES)


def _sigmoid(x):
    return 1.0 / (1.0 + jnp.exp(-x))


def _log_sigmoid(x):
    return jnp.minimum(x, 0.0) - jnp.log(1.0 + jnp.exp(-jnp.abs(x)))


def _rms(x, w):
    ms = jnp.mean(x * x, axis=-1, keepdims=True)
    return x * lax.rsqrt(ms + RMS_EPS) * w


def _split3(x):
    p1 = x.astype(BF16)
    r1 = x - p1.astype(F32)
    p2 = r1.astype(BF16)
    p3 = (r1 - p2.astype(F32)).astype(BF16)
    return p1, p2, p3


def _dot(a, b):
    return jnp.dot(a, b, preferred_element_type=F32)


def _dot_nt(a, b):
    return lax.dot_general(a, b, (((1,), (1,)), ((), ())), preferred_element_type=F32)


def _in_proj_kernel(x_ref, nw_ref, w_ref, w2_ref, b2_ref, bf_ref, gla_ref, loga_ref, qf_ref, *refs, dims, kv_t):
    n_gla, n_fox, d_model, n_lr, n_fh = dims
    sga_ref, sgb_ref, lft_ref = refs[-3:]
    xn = _rms(x_ref[...], nw_ref[...]).astype(BF16)

    def mm(lo, hi):
        return _dot(xn, w_ref[:, lo:hi])

    o = 0
    gla_ref[...] = mm(o, o + n_gla)
    o += n_gla
    fw = n_fox // 3
    qf_ref[...] = mm(o, o + fw)
    kf = mm(o + fw, o + 2 * fw)
    vf = mm(o + 2 * fw, o + 3 * fw)
    if kv_t:
        kb_ref, kt_ref, vt_ref, vtb_ref = refs[:4]
        kb_ref[...] = kf.astype(BF16)
        kt_ref[...] = kf.T
        v_t = vf.T
        vt_ref[...] = v_t
        v_tb = v_t.astype(BF16)
        for p in range(fw // LANES):
            vtb_ref[p] = v_tb[p * LANES:(p + 1) * LANES, :]
    else:
        kf_ref, vf_ref = refs[:2]
        kf_ref[...] = kf
        vf_ref[...] = vf
    o += n_fox
    sga_ref[...] = _sigmoid(mm(o, o + d_model))
    sgb_ref[...] = _sigmoid(mm(o + d_model, o + 2 * d_model))
    o += 2 * d_model
    small = mm(o, o + LANES)
    pre = _dot(small.astype(BF16), w2_ref[...]) + b2_ref[...]
    loga_ref[...] = _log_sigmoid(pre) * (1.0 / GLA_TAU)
    small_t = small.T
    lft_ref[...] = _log_sigmoid(small_t[n_lr:n_lr + n_fh, :] + bf_ref[...])


def _in_proj(x2, nw, w_all, w2p, b2, bfc, dims, tm, seq=None):
    n_gla, n_fox, d_model, n_lr, n_fh = dims
    t = x2.shape[0]
    fw = n_fox // 3
    n_qk = w2p.shape[1]
    row = lambda i: (i, 0)
    const = lambda i: (0, 0)
    if seq is None:
        kv_shapes = (jax.ShapeDtypeStruct((t, fw), F32),) * 2
        kv_specs = (pl.BlockSpec((tm, fw), row),) * 2
    else:
        nsb = seq // tm
        batch = t // seq
        kv_shapes = (
            jax.ShapeDtypeStruct((t, fw), BF16),
            jax.ShapeDtypeStruct((batch, fw, seq), F32),
            jax.ShapeDtypeStruct((batch, fw, seq), F32),
            jax.ShapeDtypeStruct((batch, fw // LANES, LANES, seq), BF16),
        )
        kv_specs = (
            pl.BlockSpec((tm, fw), row),
            pl.BlockSpec((None, fw, tm), lambda i: (i // nsb, 0, i % nsb)),
            pl.BlockSpec((None, fw, tm), lambda i: (i // nsb, 0, i % nsb)),
            pl.BlockSpec((None, fw // LANES, LANES, tm), lambda i: (i // nsb, 0, 0, i % nsb)),
        )
    out_shape = (
        jax.ShapeDtypeStruct((t, n_gla), F32),
        jax.ShapeDtypeStruct((t, n_qk), F32),
        jax.ShapeDtypeStruct((t, fw), F32),
        *kv_shapes,
        jax.ShapeDtypeStruct((t, d_model), F32),
        jax.ShapeDtypeStruct((t, d_model), F32),
        jax.ShapeDtypeStruct((n_fh, t), F32),
    )
    return pl.pallas_call(
        functools.partial(_in_proj_kernel, dims=dims, kv_t=seq is not None),
        grid=(t // tm,),
        in_specs=[
            pl.BlockSpec((tm, d_model), row),
            pl.BlockSpec((1, d_model), const),
            pl.BlockSpec(w_all.shape, const, pipeline_mode=pl.Buffered(1)),
            pl.BlockSpec(w2p.shape, const),
            pl.BlockSpec((1, n_qk), const),
            pl.BlockSpec((n_fh, 1), const),
        ],
        out_specs=(
            pl.BlockSpec((tm, n_gla), row),
            pl.BlockSpec((tm, n_qk), row),
            pl.BlockSpec((tm, fw), row),
            *kv_specs,
            pl.BlockSpec((tm, d_model), row),
            pl.BlockSpec((tm, d_model), row),
            pl.BlockSpec((n_fh, tm), lambda i: (0, i)),
        ),
        out_shape=out_shape,
        compiler_params=_cparams(("parallel",)),
        name="in_proj",
    )(x2, nw, w_all, w2p, b2, bfc)


def _gla_kernel(gin_ref, loga_ref, s0_ref, nw_ref, og_ref, sout_ref, s_scr, *,
                chunk, n_chunks, n_seq, heads, dk, dv, mx_dtype):
    j = pl.program_id(1)
    qk = heads * dk
    vw = heads * dv
    kpad = max(chunk, LANES)

    @pl.when(j == 0)
    def _():
        s_scr[...] = s0_ref[...]

    rows = n_seq * n_chunks * chunk
    shift = chunk.bit_length() - 1
    assert chunk == 1 << shift
    nw = nw_ref[...]

    rr = lax.broadcasted_iota(jnp.int32, (rows, rows), 0)
    cc = lax.broadcasted_iota(jnp.int32, (rows, rows), 1)
    same_chunk = (rr >> shift) == (cc >> shift)
    tri = jnp.where(same_chunk, jnp.where(cc <= rr, 1.0, 0.0), 0.0).astype(mx_dtype)
    gs = jnp.concatenate([p.astype(mx_dtype) for p in _split3(loga_ref[...])], axis=1)
    bs = _dot(tri, gs)
    b_all = bs[:, :qk] + bs[:, qk:2 * qk] + bs[:, 2 * qk:]
    k_all = gin_ref[:, qk:2 * qk]
    q_i_all = (gin_ref[:, 0:qk] * jnp.exp(b_all) * dk ** -0.5).astype(mx_dtype)
    k_i_all = k_all * jnp.exp(-b_all)
    vb_all = gin_ref[:, 2 * qk:2 * qk + vw].astype(mx_dtype)
    r_all = gin_ref[:, 2 * qk + vw:2 * qk + 2 * vw]
    silu_r = r_all * _sigmoid(r_all)

    rio = lax.broadcasted_iota(jnp.int32, (chunk, kpad), 0)
    cio = lax.broadcasted_iota(jnp.int32, (chunk, kpad), 1)
    tril = cio <= rio

    def pad_rows(a):
        if kpad == chunk:
            return a
        return jnp.concatenate([a, jnp.zeros((kpad - chunk, a.shape[1]), a.dtype)], axis=0)

    for idx in range(n_seq * n_chunks):
        seq = idx // n_chunks
        sl = slice(idx * chunk, (idx + 1) * chunk)
        b = b_all[sl]
        b_last = b[chunk - 1:chunk, :]
        q_i = q_i_all[sl]
        k_i = pad_rows(k_i_all[sl]).astype(mx_dtype)
        k_end_t = pad_rows(k_all[sl] * jnp.exp(b_last - b)).T.astype(mx_dtype)
        decay = jnp.exp(pad_rows(b).T[:, chunk - 1:chunk])
        vb = pad_rows(vb_all[sl])
        for h in range(heads):
            qh = q_i[:, h * dk:(h + 1) * dk]
            kh = k_i[:, h * dk:(h + 1) * dk]
            vh = vb[:, h * dv:(h + 1) * dv]
            a = jnp.where(tril, _dot_nt(qh, kh), 0.0).astype(mx_dtype)
            s_old = s_scr[seq, h]
            o = _dot(a, vh) + _dot(qh, s_old.astype(mx_dtype))
            kv = _dot(k_end_t[h * dk:(h + 1) * dk, :], vh)
            s_scr[seq, h] = decay[h * dk:(h + 1) * dk, :] * s_old + kv
            og_ref[sl, h * dv:(h + 1) * dv] = (_rms(o, nw) * silu_r[sl, h * dv:(h + 1) * dv]).astype(og_ref.dtype)

    @pl.when(j == pl.num_programs(1) - 1)
    def _():
        sout_ref[...] = s_scr[...]


def _gla(gin, loga, s0, nw, *, n_outer, n_steps, n_seq, n_chunks, chunk, mx_dtype, out_dtype):
    t = gin.shape[0]
    _, heads, dk, dv = s0.shape
    rows = n_seq * n_chunks * chunk
    assert t == n_outer * n_steps * rows
    kern = functools.partial(_gla_kernel, chunk=chunk, n_chunks=n_chunks, n_seq=n_seq, heads=heads,
                             dk=dk, dv=dv, mx_dtype=mx_dtype)
    tok = lambda i, j: (i * n_steps + j, 0)
    st = lambda i, j: (i, 0, 0, 0)
    return pl.pallas_call(
        kern,
        grid=(n_outer, n_steps),
        in_specs=[
            pl.BlockSpec((rows, gin.shape[1]), tok),
            pl.BlockSpec((rows, loga.shape[1]), tok),
            pl.BlockSpec((n_seq, heads, dk, dv), st),
            pl.BlockSpec((1, dv), lambda i, j: (0, 0)),
        ],
        out_specs=(
            pl.BlockSpec((rows, heads * dv), tok),
            pl.BlockSpec((n_seq, heads, dk, dv), st),
        ),
        out_shape=(
            jax.ShapeDtypeStruct((t, heads * dv), out_dtype),
            jax.ShapeDtypeStruct(s0.shape, F32),
        ),
        scratch_shapes=[pltpu.VMEM((n_seq, heads, dk, dv), F32)],
        compiler_params=_cparams(("parallel", "arbitrary")),
        name="gla",
    )(gin, loga, s0, nw)


def _fox_prep_kernel(qf_ref, kf_ref, lft_ref, rq_ref, rk_ref, qat_ref, ka_ref, carry_scr, *, heads, dh):
    j = pl.program_id(1)
    tp = qf_ref.shape[0]

    @pl.when(j == 0)
    def _():
        carry_scr[...] = jnp.zeros_like(carry_scr)

    rio = lax.broadcasted_iota(jnp.int32, (tp, tp), 0)
    cio = lax.broadcasted_iota(jnp.int32, (tp, tp), 1)
    upper = jnp.where(rio <= cio, 1.0, 0.0).astype(BF16)
    l1, l2, l3 = _split3(lft_ref[...])
    cs = _dot(jnp.concatenate([l1, l2, l3, jnp.zeros_like(l1)], axis=0), upper)
    c_t = cs[0:heads] + cs[heads:2 * heads] + cs[2 * heads:3 * heads] + carry_scr[:, 0:1]
    carry_scr[...] = jnp.broadcast_to(c_t[:, tp - 1:tp], carry_scr.shape)
    c1, c2, c3 = _split3(c_t * LOG2E)
    pad_rows = LANES - 3 * heads - 8
    aug_t = jnp.concatenate([c1.astype(F32), c2.astype(F32), c3.astype(F32),
                             jnp.ones((8, tp), F32), jnp.zeros((pad_rows, tp), F32)], axis=0)
    aug = aug_t.T.astype(BF16)
    lq_t = jnp.concatenate([(qf_ref[...].T * (dh ** -0.5 * LOG2E)).astype(BF16), aug_t.astype(BF16)], axis=0)
    lk = jnp.concatenate([kf_ref[...].astype(BF16), aug], axis=1)
    qa_t = _dot(rq_ref[...], lq_t).astype(BF16)
    ka = _dot(lk, rk_ref[...]).astype(BF16)
    for h in range(heads):
        qat_ref[h] = qa_t[h * LANES:(h + 1) * LANES, :]
        ka_ref[h] = ka[:, h * LANES:(h + 1) * LANES]


def _placement_matrices(heads, dh):
    kin = heads * dh + LANES
    rq = np.zeros((kin, heads * LANES), np.float32)
    rk = np.zeros((kin, heads * LANES), np.float32)
    base = heads * dh
    ones_row = base + 3 * heads
    for h in range(heads):
        for d in range(dh):
            rq[h * dh + d, h * LANES + d] = 1.0
            rk[h * dh + d, h * LANES + d] = 1.0
        for p in range(3):
            rq[base + p * heads + h, h * LANES + dh + p] = 1.0
            rq[ones_row, h * LANES + dh + 3 + p] = 1.0
            rk[ones_row, h * LANES + dh + p] = 1.0
            rk[base + p * heads + h, h * LANES + dh + 3 + p] = -1.0
    return jnp.asarray(rq.T, BF16), jnp.asarray(rk, BF16)


def _fox_prep(qf, kf, lft, batch, seq, heads, dh, tp):
    rq, rk = _placement_matrices(heads, dh)
    n_steps = seq // tp
    tok = lambda b, j: (b * n_steps + j, 0)
    const = lambda b, j: (0, 0)
    return pl.pallas_call(
        functools.partial(_fox_prep_kernel, heads=heads, dh=dh),
        grid=(batch, n_steps),
        in_specs=[
            pl.BlockSpec((tp, heads * dh), tok),
            pl.BlockSpec((tp, heads * dh), tok),
            pl.BlockSpec((heads, tp), lambda b, j: (0, b * n_steps + j)),
            pl.BlockSpec(rq.shape, const),
            pl.BlockSpec(rk.shape, const),
        ],
        out_specs=(
            pl.BlockSpec((None, heads, LANES, tp), lambda b, j: (b, 0, 0, j)),
            pl.BlockSpec((None, heads, tp, LANES), lambda b, j: (b, 0, j, 0)),
        ),
        out_shape=(
            jax.ShapeDtypeStruct((batch, heads, LANES, seq), BF16),
            jax.ShapeDtypeStruct((batch, heads, seq, LANES), BF16),
        ),
        scratch_shapes=[pltpu.VMEM((heads, LANES), F32)],
        compiler_params=_cparams(("parallel", "arbitrary")),
        name="fox_prep",
    )(qf, kf, lft, rq, rk)


def _fox_flash_body(qi, ki, qat_ref, ka_ref, vt_ref, o_ref, m_scr, l_scr, acc_scr, s0_scr, s1_scr, *, dh, hps):
    tq = qat_ref.shape[2]
    tk = ka_ref.shape[1]
    hpl = LANES // dh

    @pl.when(ki == 0)
    def _():
        m_scr[...] = jnp.full_like(m_scr, NEG)
        l_scr[...] = jnp.zeros_like(l_scr)
        acc_scr[...] = jnp.zeros_like(acc_scr)

    def update(masked):
        def scores(h, buf):
            buf[...] = _dot(ka_ref[h], qat_ref[h])

        def soft_pv(h, buf):
            s = buf[...]
            if masked:
                rio = lax.broadcasted_iota(jnp.int32, (tk, tq), 0)
                cio = lax.broadcasted_iota(jnp.int32, (tk, tq), 1)
                s = jnp.where(rio <= cio, s, NEG)
            m_prev = m_scr[h]
            m_new = jnp.maximum(m_prev, jnp.max(s, axis=0, keepdims=True))
            a = jnp.exp2(m_prev - m_new)
            p = jnp.exp2(s - m_new)
            l_scr[h] = a * l_scr[h] + jnp.sum(p, axis=0, keepdims=True)
            v_t = vt_ref[h // hpl, pl.ds(pl.multiple_of((h % hpl) * dh, dh), dh), :]
            acc_scr[h] = a * acc_scr[h] + _dot(v_t, p.astype(BF16))
            m_scr[h] = m_new

        scores(0, s0_scr)

        def pair(i, carry):
            scores(2 * i + 1, s1_scr)
            soft_pv(2 * i, s0_scr)
            scores(jnp.minimum(2 * i + 2, hps - 1), s0_scr)
            soft_pv(2 * i + 1, s1_scr)
            return carry

        lax.fori_loop(0, hps // 2, pair, 0)

    @pl.when(ki < qi)
    def _():
        update(False)

    @pl.when(ki == qi)
    def _():
        update(True)
        o_t = jnp.concatenate([acc_scr[h] / l_scr[h] for h in range(hps)], axis=0)
        o_ref[...] = o_t.T.astype(o_ref.dtype)


def _fox_paged_body(j, n_steps, q_ref, kn_ref, vn_ref, lfnt_ref, k_refs, v_refs, lf_refs, o_ref,
                    qbd_scr, cncol_scr, cns_scr, base_scr, m_scr, l_scr, acc_scr, *, heads, dh, n_new, ppb, page):
    rows = heads * n_new
    width = heads * dh

    rr = lax.broadcasted_iota(jnp.int32, (rows, width), 0)
    cc = lax.broadcasted_iota(jnp.int32, (rows, width), 1)
    bd_mask = (rr // n_new) == (cc // dh)

    @pl.when(j == 0)
    def _():
        q = q_ref[...] * dh ** -0.5
        q_rep = jnp.broadcast_to(q[None], (heads, n_new, width)).reshape(rows, width)
        qbd_scr[...] = jnp.where(bd_mask, q_rep, 0.0).astype(BF16)
        lfnt = lfnt_ref[...]
        lane = lax.broadcasted_iota(jnp.int32, lfnt.shape, 1)
        cnt = jnp.zeros_like(lfnt)
        for s in range(n_new):
            col = jnp.sum(jnp.where(lane <= s, lfnt, 0.0), axis=1, keepdims=True)
            cnt = cnt + jnp.where(lane == s, col, 0.0)
        cns = jnp.broadcast_to(cnt[:, None, :], (heads, n_new, LANES)).reshape(rows, LANES)
        cns_scr[...] = cns
        r2 = lax.broadcasted_iota(jnp.int32, (rows, LANES), 0)
        l2 = lax.broadcasted_iota(jnp.int32, (rows, LANES), 1)
        cncol_scr[...] = jnp.sum(jnp.where(l2 == r2 % n_new, cns, 0.0), axis=1, keepdims=True)
        base_scr[...] = jnp.zeros_like(base_scr)
        m_scr[...] = jnp.full_like(m_scr, NEG)
        l_scr[...] = jnp.zeros_like(l_scr)
        acc_scr[...] = jnp.zeros_like(acc_scr)

    rio = lax.broadcasted_iota(jnp.int32, (page, page), 0)
    cio = lax.broadcasted_iota(jnp.int32, (page, page), 1)
    after = jnp.where(rio > cio, 1.0, 0.0).astype(BF16)
    lf_all = jnp.concatenate([lf_refs[i][...] for i in range(ppb)], axis=0)
    ss = _dot(jnp.concatenate(_split3(lf_all), axis=0), after)
    nr = ppb * heads
    suf_all = ss[0:nr] + ss[nr:2 * nr] + ss[2 * nr:3 * nr]
    tot_all = suf_all[:, 0:1] + lf_all[:, 0:1]
    base = base_scr[:, 0:1]
    biases = []
    for i in range(ppb):
        biases.append(base + suf_all[i * heads:(i + 1) * heads])
        base = base + tot_all[i * heads:(i + 1) * heads]
    base_scr[...] = jnp.broadcast_to(base, base_scr.shape)
    bias = jnp.concatenate(biases, axis=1)
    bias_rows = jnp.broadcast_to(bias[:, None, :], (heads, n_new, ppb * page)).reshape(rows, ppb * page)

    kt = jnp.concatenate([k_refs[i][...].astype(BF16) for i in range(ppb)], axis=1)
    vt = jnp.concatenate([v_refs[i][...].astype(BF16) for i in range(ppb)], axis=1)
    qbd = qbd_scr[...]
    s = _dot(qbd, kt) + cncol_scr[...] + bias_rows
    m_prev = m_scr[...]
    m_new = jnp.maximum(m_prev, jnp.max(s, axis=1, keepdims=True))
    a = jnp.exp(m_prev - m_new)
    p = jnp.exp(s - m_new)
    l_scr[...] = a * l_scr[...] + jnp.sum(p, axis=1, keepdims=True)
    acc_scr[...] = a * acc_scr[...] + _dot_nt(p.astype(BF16), vt)
    m_scr[...] = m_new

    @pl.when(j == n_steps - 1)
    def _():
        pad = jnp.zeros((LANES - n_new, width), F32)
        kn = jnp.concatenate([kn_ref[...], pad], axis=0).astype(BF16)
        vn = jnp.concatenate([vn_ref[...], pad], axis=0).astype(BF16)
        r2 = lax.broadcasted_iota(jnp.int32, (rows, LANES), 0)
        l2 = lax.broadcasted_iota(jnp.int32, (rows, LANES), 1)
        sn = _dot_nt(qbd, kn) + cncol_scr[...] - cns_scr[...]
        sn = jnp.where(l2 <= r2 % n_new, sn, NEG)
        m_p = m_scr[...]
        m_n = jnp.maximum(m_p, jnp.max(sn, axis=1, keepdims=True))
        a2 = jnp.exp(m_p - m_n)
        pn = jnp.exp(sn - m_n)
        l_f = a2 * l_scr[...] + jnp.sum(pn, axis=1, keepdims=True)
        acc_f = a2 * acc_scr[...] + _dot(pn.astype(BF16), vn)
        o = jnp.where(bd_mask, acc_f / l_f, 0.0)
        o_ref[...] = jnp.sum(o.reshape(heads, n_new, width), axis=0)


def _attn_kernel(qi_ref, ki_ref, pt_ref, qat_ref, ka_ref, vt_ref, q_ref, kn_ref, vn_ref, lfnt_ref, *refs,
                 n_flash, n_paged, n_pg_steps, dh, hps, heads, n_new, ppb, page):
    k_refs = refs[0:ppb]
    v_refs = refs[ppb:2 * ppb]
    lf_refs = refs[2 * ppb:3 * ppb]
    of_ref, op_ref = refs[3 * ppb:3 * ppb + 2]
    flash_scr = refs[3 * ppb + 2:3 * ppb + 7]
    paged_scr = refs[3 * ppb + 7:]
    i = pl.program_id(0)

    @pl.when(i < n_flash)
    def _():
        _fox_flash_body(qi_ref[i], ki_ref[i], qat_ref, ka_ref, vt_ref, of_ref, *flash_scr, dh=dh, hps=hps)

    @pl.when(i < n_paged)
    def _():
        _fox_paged_body(i % n_pg_steps, n_pg_steps, q_ref, kn_ref, vn_ref, lfnt_ref, k_refs, v_refs, lf_refs, op_ref,
                        *paged_scr, heads=heads, dh=dh, n_new=n_new, ppb=ppb, page=page)


def _attention(qat, ka, vt, pt_flat, qf, kf, vf, lfnt_pad, ck, cv, clf, *, batch, seq, tq, n_seq, n_new, n_pages,
               heads, dh, ppb):
    page = ck.shape[-1]
    width = heads * dh
    rows = heads * n_new
    hpl = LANES // dh
    nq = seq // tq
    qi_l, ki_l = [], []
    for _ in range(batch):
        for q in range(nq):
            for k in range(q + 1):
                qi_l.append(q)
                ki_l.append(k)
    n_tri = len(qi_l) // batch
    n_flash = len(qi_l)
    qi_a = jnp.asarray(np.array(qi_l, np.int32))
    ki_a = jnp.asarray(np.array(ki_l, np.int32))
    n_pg_steps = n_pages // ppb
    n_paged = n_seq * n_pg_steps
    n_grid = max(n_flash, n_paged)

    def fl(i):
        fi = jnp.minimum(i, n_flash - 1)
        return fi, fi // n_tri

    def pg(i):
        pi = jnp.minimum(i, n_paged - 1)
        return pi // n_pg_steps, pi % n_pg_steps

    def qat_map(i, qi, ki, pt):
        fi, b = fl(i)
        return (b, 0, 0, qi[fi])

    def ka_map(i, qi, ki, pt):
        fi, b = fl(i)
        return (b, 0, ki[fi], 0)

    def vt_map(i, qi, ki, pt):
        fi, b = fl(i)
        return (b, 0, 0, ki[fi])

    def of_map(i, qi, ki, pt):
        fi, b = fl(i)
        return (b * nq + qi[fi], 0)

    def tok_map(i, qi, ki, pt):
        return (pg(i)[0], 0)

    def page_map(k):
        def m(i, qi, ki, pt):
            sq, j = pg(i)
            return (pt[sq * n_pages + (n_pages - 1 - (j * ppb + k))], 0, 0)
        return m

    tok = pl.BlockSpec((n_new, width), tok_map)
    in_specs = [
        pl.BlockSpec((None, heads, LANES, tq), qat_map),
        pl.BlockSpec((None, heads, tq, LANES), ka_map),
        pl.BlockSpec((None, heads // hpl, LANES, tq), vt_map),
        tok, tok, tok,
        pl.BlockSpec((None, heads, LANES), lambda i, qi, ki, pt: (pg(i)[0], 0, 0)),
    ]
    in_specs += [pl.BlockSpec((None, width, page), page_map(k)) for k in range(ppb)]
    in_specs += [pl.BlockSpec((None, width, page), page_map(k)) for k in range(ppb)]
    in_specs += [pl.BlockSpec((None, heads, page), page_map(k)) for k in range(ppb)]
    grid_spec = pltpu.PrefetchScalarGridSpec(
        num_scalar_prefetch=3,
        grid=(n_grid,),
        in_specs=in_specs,
        out_specs=(pl.BlockSpec((tq, width), of_map), pl.BlockSpec((n_new, width), tok_map)),
        scratch_shapes=[
            pltpu.VMEM((heads, 1, tq), F32),
            pltpu.VMEM((heads, 1, tq), F32),
            pltpu.VMEM((heads, dh, tq), F32),
            pltpu.VMEM((tq, tq), F32),
            pltpu.VMEM((tq, tq), F32),
            pltpu.VMEM((rows, width), BF16),
            pltpu.VMEM((rows, 1), F32),
            pltpu.VMEM((rows, LANES), F32),
            pltpu.VMEM((heads, LANES), F32),
            pltpu.VMEM((rows, 1), F32),
            pltpu.VMEM((rows, 1), F32),
            pltpu.VMEM((rows, width), F32),
        ],
    )
    kern = functools.partial(_attn_kernel, n_flash=n_flash, n_paged=n_paged, n_pg_steps=n_pg_steps, dh=dh,
                             hps=heads, heads=heads, n_new=n_new, ppb=ppb, page=page)
    return pl.pallas_call(
        kern,
        grid_spec=grid_spec,
        out_shape=(jax.ShapeDtypeStruct((batch * seq, width), BF16),
                   jax.ShapeDtypeStruct((n_seq * n_new, width), F32)),
        compiler_params=_cparams(("arbitrary",)),
        name="attention",
    )(qi_a, ki_a, pt_flat, qat, ka, vt, qf, kf, vf, lfnt_pad, *([ck] * ppb), *([cv] * ppb), *([clf] * ppb))


def _out_proj_kernel(x_ref, og_ref, of_ref, sga_ref, sgb_ref, wug_ref, wuf_ref, wo_ref, nfw_ref, wr_ref, br_ref,
                     h_ref, xn_ref, gates_ref, *, n_experts, n_groups):
    up_a = _dot(og_ref[...].astype(BF16), wug_ref[...])
    up_b = _dot(of_ref[...].astype(BF16), wuf_ref[...])
    merged = sga_ref[...] * up_a + sgb_ref[...] * up_b
    h = x_ref[...] + _dot(merged.astype(BF16), wo_ref[...])
    h_ref[...] = h
    xn = _rms(h, nfw_ref[...])
    xn_ref[...] = xn.astype(BF16)

    x_hi = xn.astype(BF16)
    x_lo = (xn - x_hi.astype(F32)).astype(BF16)
    l_hi = _dot(x_hi, wr_ref[...])
    logits = l_hi[:, :LANES] + l_hi[:, LANES:] + _dot(x_lo, wr_ref[:, :LANES]) + br_ref[...]
    lane = lax.broadcasted_iota(jnp.int32, logits.shape, 1).astype(F32)
    epg = n_experts // n_groups
    big = 4.0 * LANES
    is_g = (lane >= n_experts) & (lane < n_experts + n_groups)
    gl = jnp.where(is_g, logits, NEG)
    gmax = jnp.max(gl, axis=1, keepdims=True)
    gsum = jnp.sum(jnp.exp(gl - gmax), axis=1, keepdims=True)
    p_g = 1.0 / gsum
    g_sel = jnp.min(jnp.where(gl == gmax, lane, big), axis=1, keepdims=True) - n_experts
    e_lo = g_sel * epg
    in_grp = (lane >= e_lo) & (lane < e_lo + epg)
    el = jnp.where(in_grp, logits, NEG)
    emax = jnp.max(el, axis=1, keepdims=True)
    ee = jnp.exp(el - emax)
    pe = ee / jnp.sum(ee, axis=1, keepdims=True)
    cand = jnp.where(in_grp, pe, -1.0)
    v1 = jnp.max(cand, axis=1, keepdims=True)
    i1 = jnp.min(jnp.where(cand == v1, lane, big), axis=1, keepdims=True)
    cand2 = jnp.where(lane == i1, -1.0, cand)
    v2 = jnp.max(cand2, axis=1, keepdims=True)
    i2 = jnp.min(jnp.where(cand2 == v2, lane, big), axis=1, keepdims=True)
    tot = v1 + v2
    gates_ref[...] = (jnp.where(lane == i1, p_g * (v1 / tot), 0.0) + jnp.where(lane == i2, p_g * (v2 / tot), 0.0)
                      + jnp.where(lane == n_experts, g_sel, 0.0))


def _out_proj(x2, og, of, sga, sgb, wug, wuf, wo, nfw, wr, br, *, n_experts, n_groups, tm):
    t, d_model = x2.shape
    row = lambda i: (i, 0)
    const = lambda i: (0, 0)
    return pl.pallas_call(
        functools.partial(_out_proj_kernel, n_experts=n_experts, n_groups=n_groups),
        grid=(t // tm,),
        in_specs=[
            pl.BlockSpec((tm, d_model), row),
            pl.BlockSpec((tm, og.shape[1]), row),
            pl.BlockSpec((tm, of.shape[1]), row),
            pl.BlockSpec((tm, d_model), row),
            pl.BlockSpec((tm, d_model), row),
            pl.BlockSpec(wug.shape, const, pipeline_mode=pl.Buffered(1)),
            pl.BlockSpec(wuf.shape, const, pipeline_mode=pl.Buffered(1)),
            pl.BlockSpec(wo.shape, const, pipeline_mode=pl.Buffered(1)),
            pl.BlockSpec((1, d_model), const),
            pl.BlockSpec(wr.shape, const),
            pl.BlockSpec((1, LANES), const),
        ],
        out_specs=(
            pl.BlockSpec((tm, d_model), row),
            pl.BlockSpec((tm, d_model), row),
            pl.BlockSpec((tm, LANES), row),
        ),
        out_shape=(
            jax.ShapeDtypeStruct((t, d_model), F32),
            jax.ShapeDtypeStruct((t, d_model), BF16),
            jax.ShapeDtypeStruct((t, LANES), F32),
        ),
        compiler_params=_cparams(("parallel",)),
        name="out_proj",
    )(x2, og, of, sga, sgb, wug, wuf, wo, nfw, wr, br)


def _moe_kernel(xn_ref, gates_ref, h_ref, wg_ref, wu_ref, wd_ref, fnw_ref, y_ref,
                xs_scr, ys_scr, gs_scr, selt_scr, off_smem, *, n_experts, n_groups, chunk, eps):
    e = pl.program_id(1)
    tm = xn_ref.shape[0]
    rows = selt_scr.shape[1]
    rows_p = xs_scr.shape[0]
    epg = n_experts // n_groups

    @pl.when(e == 0)
    def _():
        gates = gates_ref[...]
        lane = lax.broadcasted_iota(jnp.int32, (tm, LANES), 1).astype(F32)
        g_sel = gates[:, n_experts:n_experts + 1]
        in_g = lane == g_sel
        onehot = jnp.where(in_g, 1.0, 0.0).astype(BF16)
        rio = lax.broadcasted_iota(jnp.int32, (tm, tm), 0)
        cio = lax.broadcasted_iota(jnp.int32, (tm, tm), 1)
        tril = jnp.where(cio <= rio, 1.0, 0.0).astype(BF16)
        cum = _dot(tril, onehot)
        cnt = cum[tm - 1:tm, :].astype(jnp.int32)
        lane1 = lax.broadcasted_iota(jnp.int32, (1, LANES), 1)
        off = jnp.int32(0)
        off_vec = jnp.zeros((1, LANES), F32)
        for g in range(n_groups):
            off_smem[g] = off
            off_vec = off_vec + jnp.where(lane1 == g, off.astype(F32), 0.0)
            off = off + ((cnt[0, g] + 7) // 8) * 8
        off_smem[n_groups] = off
        pos = jnp.sum(jnp.where(in_g, off_vec + cum - 1.0, 0.0), axis=1, keepdims=True)
        r_lane = lax.broadcasted_iota(jnp.int32, (tm, rows), 1).astype(F32)
        selt_scr[...] = jnp.where(r_lane == pos, 1.0, 0.0).astype(BF16)
        pos_t = jnp.broadcast_to(pos, (tm, LANES)).T[0:1, :]
        r_sub = lax.broadcasted_iota(jnp.int32, (rows, tm), 0).astype(F32)
        sel = jnp.where(r_sub == pos_t, 1.0, 0.0).astype(BF16)
        xs_scr[0:rows, :] = _dot(sel, xn_ref[...])
        g1, g2, g3 = _split3(gates)
        gs = _dot(sel, jnp.concatenate([g1, g2, g3], axis=1))
        gs_scr[0:rows, :] = gs[:, :LANES] + gs[:, LANES:2 * LANES] + gs[:, 2 * LANES:]
        xs_scr[rows:rows_p, :] = jnp.zeros((rows_p - rows, xs_scr.shape[1]), F32)
        gs_scr[rows:rows_p, :] = jnp.zeros((rows_p - rows, LANES), F32)
        ys_scr[...] = jnp.zeros_like(ys_scr)

    g = (e * eps) // epg
    start = off_smem[g]
    n_rows = off_smem[g + 1] - start
    lane_c = lax.broadcasted_iota(jnp.int32, (chunk, LANES), 1)

    def body(c, carry):
        r0 = pl.multiple_of(start + c * chunk, 8)
        x = xs_scr[pl.ds(r0, chunk), :].astype(BF16)
        gs = gs_scr[pl.ds(r0, chunk), :]
        acc = None
        for i in range(eps):
            gg = _dot(x, wg_ref[i])
            uu = _dot(x, wu_ref[i])
            hid = (gg * _sigmoid(gg)) * uu
            out = _dot(hid.astype(BF16), wd_ref[i])
            gate = jnp.sum(jnp.where(lane_c == e * eps + i, gs, 0.0), axis=1, keepdims=True)
            acc = gate * out if acc is None else acc + gate * out
        ys_scr[pl.ds(r0, chunk), :] += acc
        return carry

    lax.fori_loop(0, (n_rows + chunk - 1) // chunk, body, 0)

    @pl.when(e == pl.num_programs(1) - 1)
    def _():
        ys = ys_scr[0:rows, :]
        hi = ys.astype(BF16)
        lo = (ys - hi.astype(F32)).astype(BF16)
        selt = selt_scr[...]
        moe = _dot(selt, hi) + _dot(selt, lo)
        y_ref[...] = _rms(h_ref[...] + moe, fnw_ref[...])


def _moe(xn, gates, h, wg, wu, wd, fnw, *, tm, n_groups, chunk=MOE_CHUNK_ROWS, eps=MOE_EXPERTS_PER_STEP):
    t, d_model = h.shape
    n_experts, _, d_exp = wg.shape
    rows = tm + LANES
    rows_p = rows + chunk
    row = lambda i, e: (i, 0)
    once = pl.Buffered(1)
    assert (n_experts // n_groups) % eps == 0
    kern = functools.partial(_moe_kernel, n_experts=n_experts, n_groups=n_groups, chunk=chunk, eps=eps)
    return pl.pallas_call(
        kern,
        grid=(t // tm, n_experts // eps),
        in_specs=[
            pl.BlockSpec((tm, d_model), row, pipeline_mode=once),
            pl.BlockSpec((tm, LANES), row, pipeline_mode=once),
            pl.BlockSpec((tm, d_model), row, pipeline_mode=once),
            pl.BlockSpec((eps, d_model, d_exp), lambda i, e: (e, 0, 0)),
            pl.BlockSpec((eps, d_model, d_exp), lambda i, e: (e, 0, 0)),
            pl.BlockSpec((eps, d_exp, d_model), lambda i, e: (e, 0, 0)),
            pl.BlockSpec((1, d_model), lambda i, e: (0, 0)),
        ],
        out_specs=pl.BlockSpec((tm, d_model), row),
        out_shape=jax.ShapeDtypeStruct((t, d_model), F32),
        scratch_shapes=[
            pltpu.VMEM((rows_p, d_model), F32),
            pltpu.VMEM((rows_p, d_model), F32),
            pltpu.VMEM((rows_p, LANES), F32),
            pltpu.VMEM((tm, rows), BF16),
            pltpu.SMEM((n_groups + 1,), jnp.int32),
        ],
        compiler_params=_cparams(("parallel", "arbitrary")),
        name="moe",
    )(xn, gates, h, wg, wu, wd, fnw)


def _pick_tile(n, pref):
    t = min(n, pref)
    while n % t:
        t //= 2
    return t


def kernel(x_prompt, x_sample, state_gla, cache_k, cache_v, cache_logf, page_table, norm_mix_w, w_in, w_gla_a2,
           b_gla_a2, gla_norm_w, b_fox_f, w_up_gla, w_up_fox, w_out, norm_ffn_w, w_router_group, b_router_group,
           w_router_expert, b_router_expert, w_exp_gate, w_exp_up, w_exp_down, final_norm_w):
    depth = w_in.shape[0]
    assert depth == 1
    batch, seq, d_model = x_prompt.shape
    n_seq, n_new, _ = x_sample.shape
    _, _, g_heads, dk, dv = state_gla.shape
    _, n_phys, page, f_heads, dh = cache_k.shape
    n_pages = page_table.shape[1]
    n_lr = w_gla_a2.shape[1]
    n_groups = w_router_group.shape[2]
    n_experts = w_router_expert.shape[2]
    qk = g_heads * dk
    vw = g_heads * dv
    fw = f_heads * dh
    n_gla = 2 * qk + 2 * vw
    n_fox = 3 * fw
    dims = (n_gla, n_fox, d_model, n_lr, f_heads)

    wi = w_in[0]
    o_za = n_gla
    o_fox = o_za + n_lr
    o_fp = o_fox + n_fox
    o_gate = o_fp + f_heads
    w_small = jnp.concatenate([wi[:, o_za:o_fox], wi[:, o_fp:o_gate],
                               jnp.zeros((d_model, LANES - n_lr - f_heads), wi.dtype)], axis=1)
    w_all = jnp.concatenate([wi[:, :o_za], wi[:, o_fox:o_fp], wi[:, o_gate:], w_small], axis=1).astype(BF16)
    w2p = jnp.concatenate([w_gla_a2[0], jnp.zeros((LANES - n_lr, qk), F32)], axis=0).astype(BF16)
    b2 = b_gla_a2[0].reshape(1, qk)
    bfc = b_fox_f[0].reshape(f_heads, 1)
    nw_mix = norm_mix_w[0].reshape(1, d_model)
    nw_gla = gla_norm_w[0].reshape(1, dv)
    wug = w_up_gla[0].astype(BF16)
    wuf = w_up_fox[0].astype(BF16)
    wo = w_out[0].astype(BF16)
    nfw = norm_ffn_w[0].reshape(1, d_model)
    wr32 = jnp.concatenate([w_router_expert[0], w_router_group[0],
                            jnp.zeros((d_model, LANES - n_experts - n_groups), F32)], axis=1)
    wr_hi = wr32.astype(BF16)
    wr = jnp.concatenate([wr_hi, (wr32 - wr_hi.astype(F32)).astype(BF16)], axis=1)
    br = jnp.concatenate([b_router_expert[0], b_router_group[0],
                          jnp.zeros((LANES - n_experts - n_groups,), F32)]).reshape(1, LANES)
    wg = w_exp_gate[0].astype(BF16)
    wu = w_exp_up[0].astype(BF16)
    wd = w_exp_down[0].astype(BF16)
    fnw = final_norm_w.reshape(1, d_model)

    def front(x2, seq_len=None):
        tm = _pick_tile(x2.shape[0] if seq_len is None else seq_len, 512)
        gin, loga, qf, *kv, sga, sgb, lft = _in_proj(x2, nw_mix, w_all, w2p, b2, bfc, dims, tm, seq=seq_len)
        return tm, gin, loga, qf, kv, sga, sgb, lft

    def back(x2, tm, og, of, sga, sgb):
        h, xn, gates = _out_proj(x2, og, of, sga, sgb, wug, wuf, wo, nfw, wr, br,
                                 n_experts=n_experts, n_groups=n_groups, tm=tm)
        return _moe(xn, gates, h, wg, wu, wd, fnw, tm=_pick_tile(x2.shape[0], 1024), n_groups=n_groups)

    xp = x_prompt.reshape(batch * seq, d_model)
    xs = x_sample.reshape(n_seq * n_new, d_model)
    tm_p, gin_p, loga_p, qf_p, (kb_p, kt_p, vt_p, vtb_p), sga_p, sgb_p, lft_p = front(xp, seq)
    tm_s, gin_s, loga_s, qf_s, (k_s, v_s), sga_s, sgb_s, lft_s = front(xs)

    chunk = math.gcd(seq, GLA_CHUNK)
    n_chunks = _pick_tile(seq // chunk, 8)
    og_p, s_p = _gla(gin_p, loga_p, jnp.zeros((batch, g_heads, dk, dv), F32), nw_gla, n_outer=batch,
                     n_steps=seq // (chunk * n_chunks), n_seq=1, n_chunks=n_chunks, chunk=chunk,
                     mx_dtype=BF16, out_dtype=BF16)
    assert math.gcd(n_new, GLA_CHUNK) == n_new
    gs = _pick_tile(n_seq, 8)
    og_s, s_s = _gla(gin_s, loga_s, state_gla[0], nw_gla, n_outer=n_seq // gs, n_steps=1, n_seq=gs,
                     n_chunks=1, chunk=n_new, mx_dtype=F32, out_dtype=F32)

    tq = _pick_tile(seq, 512)
    qat, ka = _fox_prep(qf_p, kb_p, lft_p, batch, seq, f_heads, dh, tq)
    ck = jnp.transpose(cache_k[0], (0, 2, 3, 1)).reshape(n_phys, fw, page)
    cv = jnp.transpose(cache_v[0], (0, 2, 3, 1)).reshape(n_phys, fw, page)
    clf = jnp.transpose(cache_logf[0], (0, 2, 1))
    pt_flat = page_table.reshape(-1).astype(jnp.int32)
    lfnt = jnp.transpose(lft_s.reshape(f_heads, n_seq, n_new), (1, 0, 2))
    lfnt_pad = jnp.pad(lfnt, ((0, 0), (0, 0), (0, LANES - n_new)))
    of_p, of_s = _attention(qat, ka, vtb_p, pt_flat, qf_s, k_s, v_s, lfnt_pad, ck, cv, clf, batch=batch, seq=seq,
                            tq=tq, n_seq=n_seq, n_new=n_new, n_pages=n_pages, heads=f_heads, dh=dh,
                            ppb=_pick_tile(n_pages, PAGES_PER_STEP))

    y_p = back(xp, tm_p, og_p, of_p, sga_p, sgb_p)
    y_s = back(xs, tm_s, og_s, of_s, sga_s, sgb_s)
    lf_p = lft_p.T
    lf_s = lft_s.T
    k_p = jnp.transpose(kt_p.reshape(batch, f_heads, dh, seq), (0, 3, 1, 2))
    v_p = jnp.transpose(vt_p.reshape(batch, f_heads, dh, seq), (0, 3, 1, 2))

    return (y_p.reshape(batch, seq, d_model),
            y_s.reshape(n_seq, n_new, d_model),
            s_p[None],
            s_s[None],
            k_p.reshape(1, batch, seq, f_heads, dh),
            v_p.reshape(1, batch, seq, f_heads, dh),
            lf_p.reshape(1, batch, seq, f_heads),
            k_s.reshape(1, n_seq, n_new, f_heads, dh),
            v_s.reshape(1, n_seq, n_new, f_heads, dh),
            lf_s.reshape(1, n_seq, n_new, f_heads))
```

```python
import functools
import math

import jax
import jax.numpy as jnp
import numpy as np
from jax import lax
from jax.experimental import pallas as pl
from jax.experimental.pallas import tpu as pltpu

F32 = jnp.float32
BF16 = jnp.bfloat16

RMS_EPS = 1e-6
GLA_TAU = 16.0
GLA_CHUNK = 64
PAGES_PER_STEP = 32
MOE_CHUNK_ROWS = 288
MOE_EXPERTS_PER_STEP = 4
NEG = -1e30
LOG2E = 1.4426950408889634

V7X_VMEM_LIMIT_BYTES = 56 * 1024 * 1024
LANES = 128


def _cparams(sem):
    return pltpu.CompilerParams(dimension_semantics=sem, vmem_limit_bytes=V7X_VMEM_LIMIT_BYTES)


def _sigmoid(x):
    return 1.0 / (1.0 + jnp.exp(-x))


def _log_sigmoid(x):
    return jnp.minimum(x, 0.0) - jnp.log(1.0 + jnp.exp(-jnp.abs(x)))


def _rms(x, w):
    ms = jnp.mean(x * x, axis=-1, keepdims=True)
    return x * lax.rsqrt(ms + RMS_EPS) * w


def _split3(x):
    p1 = x.astype(BF16)
    r1 = x - p1.astype(F32)
    p2 = r1.astype(BF16)
    p3 = (r1 - p2.astype(F32)).astype(BF16)
    return p1, p2, p3


def _dot(a, b):
    return jnp.dot(a, b, preferred_element_type=F32)


def _dot_nt(a, b):
    return lax.dot_general(a, b, (((1,), (1,)), ((), ())), preferred_element_type=F32)


def _in_proj_kernel(x_ref, nw_ref, w_ref, w2_ref, b2_ref, bf_ref, gla_ref, loga_ref, qf_ref, *refs, dims, kv_t):
    n_gla, n_fox, d_model, n_lr, n_fh = dims
    sga_ref, sgb_ref, lft_ref = refs[-3:]
    xn = _rms(x_ref[...], nw_ref[...]).astype(BF16)

    def mm(lo, hi):
        return _dot(xn, w_ref[:, lo:hi])

    o = 0
    gla_ref[...] = mm(o, o + n_gla)
    o += n_gla
    fw = n_fox // 3
    qf_ref[...] = mm(o, o + fw)
    kf = mm(o + fw, o + 2 * fw)
    vf = mm(o + 2 * fw, o + 3 * fw)
    if kv_t:
        kb_ref, kt_ref, vt_ref, vtb_ref = refs[:4]
        kb_ref[...] = kf.astype(BF16)
        kt_ref[...] = kf.T
        v_t = vf.T
        vt_ref[...] = v_t
        v_tb = v_t.astype(BF16)
        for p in range(fw // LANES):
            vtb_ref[p] = v_tb[p * LANES:(p + 1) * LANES, :]
    else:
        kf_ref, vf_ref = refs[:2]
        kf_ref[...] = kf
        vf_ref[...] = vf
    o += n_fox
    sga_ref[...] = _sigmoid(mm(o, o + d_model))
    sgb_ref[...] = _sigmoid(mm(o + d_model, o + 2 * d_model))
    o += 2 * d_model
    small = mm(o, o + LANES)
    pre = _dot(small.astype(BF16), w2_ref[...]) + b2_ref[...]
    loga_ref[...] = _log_sigmoid(pre) * (1.0 / GLA_TAU)
    small_t = small.T
    lft_ref[...] = _log_sigmoid(small_t[n_lr:n_lr + n_fh, :] + bf_ref[...])


def _in_proj(x2, nw, w_all, w2p, b2, bfc, dims, tm, seq=None):
    n_gla, n_fox, d_model, n_lr, n_fh = dims
    t = x2.shape[0]
    fw = n_fox // 3
    n_qk = w2p.shape[1]
    row = lambda i: (i, 0)
    const = lambda i: (0, 0)
    if seq is None:
        kv_shapes = (jax.ShapeDtypeStruct((t, fw), F32),) * 2
        kv_specs = (pl.BlockSpec((tm, fw), row),) * 2
    else:
        nsb = seq // tm
        batch = t // seq
        kv_shapes = (
            jax.ShapeDtypeStruct((t, fw), BF16),
            jax.ShapeDtypeStruct((batch, fw, seq), F32),
            jax.ShapeDtypeStruct((batch, fw, seq), F32),
            jax.ShapeDtypeStruct((batch, fw // LANES, LANES, seq), BF16),
        )
        kv_specs = (
            pl.BlockSpec((tm, fw), row),
            pl.BlockSpec((None, fw, tm), lambda i: (i // nsb, 0, i % nsb)),
            pl.BlockSpec((None, fw, tm), lambda i: (i // nsb, 0, i % nsb)),
            pl.BlockSpec((None, fw // LANES, LANES, tm), lambda i: (i // nsb, 0, 0, i % nsb)),
        )
    out_shape = (
        jax.ShapeDtypeStruct((t, n_gla), F32),
        jax.ShapeDtypeStruct((t, n_qk), F32),
        jax.ShapeDtypeStruct((t, fw), F32),
        *kv_shapes,
        jax.ShapeDtypeStruct((t, d_model), F32),
        jax.ShapeDtypeStruct((t, d_model), F32),
        jax.ShapeDtypeStruct((n_fh, t), F32),
    )
    return pl.pallas_call(
        functools.partial(_in_proj_kernel, dims=dims, kv_t=seq is not None),
        grid=(t // tm,),
        in_specs=[
            pl.BlockSpec((tm, d_model), row),
            pl.BlockSpec((1, d_model), const),
            pl.BlockSpec(w_all.shape, const, pipeline_mode=pl.Buffered(1)),
            pl.BlockSpec(w2p.shape, const),
            pl.BlockSpec((1, n_qk), const),
            pl.BlockSpec((n_fh, 1), const),
        ],
        out_specs=(
            pl.BlockSpec((tm, n_gla), row),
            pl.BlockSpec((tm, n_qk), row),
            pl.BlockSpec((tm, fw), row),
            *kv_specs,
            pl.BlockSpec((tm, d_model), row),
            pl.BlockSpec((tm, d_model), row),
            pl.BlockSpec((n_fh, tm), lambda i: (0, i)),
        ),
        out_shape=out_shape,
        compiler_params=_cparams(("parallel",)),
        name="in_proj",
    )(x2, nw, w_all, w2p, b2, bfc)


def _gla_kernel(gin_ref, loga_ref, s0_ref, nw_ref, og_ref, sout_ref, s_scr, *,
                chunk, n_chunks, n_seq, heads, dk, dv, mx_dtype):
    j = pl.program_id(1)
    qk = heads * dk
    vw = heads * dv
    kpad = max(chunk, LANES)

    @pl.when(j == 0)
    def _():
        s_scr[...] = s0_ref[...]

    rows = n_seq * n_chunks * chunk
    shift = chunk.bit_length() - 1
    assert chunk == 1 << shift
    nw = nw_ref[...]

    rr = lax.broadcasted_iota(jnp.int32, (rows, rows), 0)
    cc = lax.broadcasted_iota(jnp.int32, (rows, rows), 1)
    same_chunk = (rr >> shift) == (cc >> shift)
    tri = jnp.where(same_chunk, jnp.where(cc <= rr, 1.0, 0.0), 0.0).astype(mx_dtype)
    gs = jnp.concatenate([p.astype(mx_dtype) for p in _split3(loga_ref[...])], axis=1)
    bs = _dot(tri, gs)
    b_all = bs[:, :qk] + bs[:, qk:2 * qk] + bs[:, 2 * qk:]
    k_all = gin_ref[:, qk:2 * qk]
    q_i_all = (gin_ref[:, 0:qk] * jnp.exp(b_all) * dk ** -0.5).astype(mx_dtype)
    k_i_all = k_all * jnp.exp(-b_all)
    vb_all = gin_ref[:, 2 * qk:2 * qk + vw].astype(mx_dtype)
    r_all = gin_ref[:, 2 * qk + vw:2 * qk + 2 * vw]
    silu_r = r_all * _sigmoid(r_all)

    rio = lax.broadcasted_iota(jnp.int32, (chunk, kpad), 0)
    cio = lax.broadcasted_iota(jnp.int32, (chunk, kpad), 1)
    tril = cio <= rio

    def pad_rows(a):
        if kpad == chunk:
            return a
        return jnp.concatenate([a, jnp.zeros((kpad - chunk, a.shape[1]), a.dtype)], axis=0)

    for idx in range(n_seq * n_chunks):
        seq = idx // n_chunks
        sl = slice(idx * chunk, (idx + 1) * chunk)
        b = b_all[sl]
        b_last = b[chunk - 1:chunk, :]
        q_i = q_i_all[sl]
        k_i = pad_rows(k_i_all[sl]).astype(mx_dtype)
        k_end_t = pad_rows(k_all[sl] * jnp.exp(b_last - b)).T.astype(mx_dtype)
        decay = jnp.exp(pad_rows(b).T[:, chunk - 1:chunk])
        vb = pad_rows(vb_all[sl])
        for h in range(heads):
            qh = q_i[:, h * dk:(h + 1) * dk]
            kh = k_i[:, h * dk:(h + 1) * dk]
            vh = vb[:, h * dv:(h + 1) * dv]
            a = jnp.where(tril, _dot_nt(qh, kh), 0.0).astype(mx_dtype)
            s_old = s_scr[seq, h]
            o = _dot(a, vh) + _dot(qh, s_old.astype(mx_dtype))
            kv = _dot(k_end_t[h * dk:(h + 1) * dk, :], vh)
            s_scr[seq, h] = decay[h * dk:(h + 1) * dk, :] * s_old + kv
            og_ref[sl, h * dv:(h + 1) * dv] = (_rms(o, nw) * silu_r[sl, h * dv:(h + 1) * dv]).astype(og_ref.dtype)

    @pl.when(j == pl.num_programs(1) - 1)
    def _():
        sout_ref[...] = s_scr[...]


def _gla(gin, loga, s0, nw, *, n_outer, n_steps, n_seq, n_chunks, chunk, mx_dtype, out_dtype):
    t = gin.shape[0]
    _, heads, dk, dv = s0.shape
    rows = n_seq * n_chunks * chunk
    assert t == n_outer * n_steps * rows
    kern = functools.partial(_gla_kernel, chunk=chunk, n_chunks=n_chunks, n_seq=n_seq, heads=heads,
                             dk=dk, dv=dv, mx_dtype=mx_dtype)
    tok = lambda i, j: (i * n_steps + j, 0)
    st = lambda i, j: (i, 0, 0, 0)
    return pl.pallas_call(
        kern,
        grid=(n_outer, n_steps),
        in_specs=[
            pl.BlockSpec((rows, gin.shape[1]), tok),
            pl.BlockSpec((rows, loga.shape[1]), tok),
            pl.BlockSpec((n_seq, heads, dk, dv), st),
            pl.BlockSpec((1, dv), lambda i, j: (0, 0)),
        ],
        out_specs=(
            pl.BlockSpec((rows, heads * dv), tok),
            pl.BlockSpec((n_seq, heads, dk, dv), st),
        ),
        out_shape=(
            jax.ShapeDtypeStruct((t, heads * dv), out_dtype),
            jax.ShapeDtypeStruct(s0.shape, F32),
        ),
        scratch_shapes=[pltpu.VMEM((n_seq, heads, dk, dv), F32)],
        compiler_params=_cparams(("parallel", "arbitrary")),
        name="gla",
    )(gin, loga, s0, nw)


def _fox_prep_kernel(qf_ref, kf_ref, lft_ref, rq_ref, rk_ref, qat_ref, ka_ref, carry_scr, *, heads, dh):
    j = pl.program_id(1)
    tp = qf_ref.shape[0]

    @pl.when(j == 0)
    def _():
        carry_scr[...] = jnp.zeros_like(carry_scr)

    rio = lax.broadcasted_iota(jnp.int32, (tp, tp), 0)
    cio = lax.broadcasted_iota(jnp.int32, (tp, tp), 1)
    upper = jnp.where(rio <= cio, 1.0, 0.0).astype(BF16)
    l1, l2, l3 = _split3(lft_ref[...])
    cs = _dot(jnp.concatenate([l1, l2, l3, jnp.zeros_like(l1)], axis=0), upper)
    c_t = cs[0:heads] + cs[heads:2 * heads] + cs[2 * heads:3 * heads] + carry_scr[:, 0:1]
    carry_scr[...] = jnp.broadcast_to(c_t[:, tp - 1:tp], carry_scr.shape)
    c1, c2, c3 = _split3(c_t * LOG2E)
    pad_rows = LANES - 3 * heads - 8
    aug_t = jnp.concatenate([c1.astype(F32), c2.astype(F32), c3.astype(F32),
                             jnp.ones((8, tp), F32), jnp.zeros((pad_rows, tp), F32)], axis=0)
    aug = aug_t.T.astype(BF16)
    lq_t = jnp.concatenate([(qf_ref[...].T * (dh ** -0.5 * LOG2E)).astype(BF16), aug_t.astype(BF16)], axis=0)
    lk = jnp.concatenate([kf_ref[...].astype(BF16), aug], axis=1)
    qa_t = _dot(rq_ref[...], lq_t).astype(BF16)
    ka = _dot(lk, rk_ref[...]).astype(BF16)
    for h in range(heads):
        qat_ref[h] = qa_t[h * LANES:(h + 1) * LANES, :]
        ka_ref[h] = ka[:, h * LANES:(h + 1) * LANES]


def _placement_matrices(heads, dh):
    kin = heads * dh + LANES
    rq = np.zeros((kin, heads * LANES), np.float32)
    rk = np.zeros((kin, heads * LANES), np.float32)
    base = heads * dh
    ones_row = base + 3 * heads
    for h in range(heads):
        for d in range(dh):
            rq[h * dh + d, h * LANES + d] = 1.0
            rk[h * dh + d, h * LANES + d] = 1.0
        for p in range(3):
            rq[base + p * heads + h, h * LANES + dh + p] = 1.0
            rq[ones_row, h * LANES + dh + 3 + p] = 1.0
            rk[ones_row, h * LANES + dh + p] = 1.0
            rk[base + p * heads + h, h * LANES + dh + 3 + p] = -1.0
    return jnp.asarray(rq.T, BF16), jnp.asarray(rk, BF16)


def _fox_prep(qf, kf, lft, batch, seq, heads, dh, tp):
    rq, rk = _placement_matrices(heads, dh)
    n_steps = seq // tp
    tok = lambda b, j: (b * n_steps + j, 0)
    const = lambda b, j: (0, 0)
    return pl.pallas_call(
        functools.partial(_fox_prep_kernel, heads=heads, dh=dh),
        grid=(batch, n_steps),
        in_specs=[
            pl.BlockSpec((tp, heads * dh), tok),
            pl.BlockSpec((tp, heads * dh), tok),
            pl.BlockSpec((heads, tp), lambda b, j: (0, b * n_steps + j)),
            pl.BlockSpec(rq.shape, const),
            pl.BlockSpec(rk.shape, const),
        ],
        out_specs=(
            pl.BlockSpec((None, heads, LANES, tp), lambda b, j: (b, 0, 0, j)),
            pl.BlockSpec((None, heads, tp, LANES), lambda b, j: (b, 0, j, 0)),
        ),
        out_shape=(
            jax.ShapeDtypeStruct((batch, heads, LANES, seq), BF16),
            jax.ShapeDtypeStruct((batch, heads, seq, LANES), BF16),
        ),
        scratch_shapes=[pltpu.VMEM((heads, LANES), F32)],
        compiler_params=_cparams(("parallel", "arbitrary")),
        name="fox_prep",
    )(qf, kf, lft, rq, rk)


def _fox_flash_body(qi, ki, qat_ref, ka_ref, vt_ref, o_ref, m_scr, l_scr, acc_scr, s0_scr, s1_scr, *, dh, hps):
    tq = qat_ref.shape[2]
    tk = ka_ref.shape[1]
    hpl = LANES // dh

    @pl.when(ki == 0)
    def _():
        m_scr[...] = jnp.full_like(m_scr, NEG)
        l_scr[...] = jnp.zeros_like(l_scr)
        acc_scr[...] = jnp.zeros_like(acc_scr)

    def update(masked):
        def scores(h, buf):
            buf[...] = _dot(ka_ref[h], qat_ref[h])

        def soft_pv(h, buf):
            s = buf[...]
            if masked:
                rio = lax.broadcasted_iota(jnp.int32, (tk, tq), 0)
                cio = lax.broadcasted_iota(jnp.int32, (tk, tq), 1)
                s = jnp.where(rio <= cio, s, NEG)
            m_prev = m_scr[h]
            m_new = jnp.maximum(m_prev, jnp.max(s, axis=0, keepdims=True))
            a = jnp.exp2(m_prev - m_new)
            p = jnp.exp2(s - m_new)
            l_scr[h] = a * l_scr[h] + jnp.sum(p, axis=0, keepdims=True)
            v_t = vt_ref[h // hpl, pl.ds(pl.multiple_of((h % hpl) * dh, dh), dh), :]
            acc_scr[h] = a * acc_scr[h] + _dot(v_t, p.astype(BF16))
            m_scr[h] = m_new

        scores(0, s0_scr)

        def pair(i, carry):
            scores(2 * i + 1, s1_scr)
            soft_pv(2 * i, s0_scr)
            scores(jnp.minimum(2 * i + 2, hps - 1), s0_scr)
            soft_pv(2 * i + 1, s1_scr)
            return carry

        lax.fori_loop(0, hps // 2, pair, 0)

    @pl.when(ki < qi)
    def _():
        update(False)

    @pl.when(ki == qi)
    def _():
        update(True)
        o_t = jnp.concatenate([acc_scr[h] / l_scr[h] for h in range(hps)], axis=0)
        o_ref[...] = o_t.T.astype(o_ref.dtype)


def _fox_paged_body(j, n_steps, q_ref, kn_ref, vn_ref, lfnt_ref, k_refs, v_refs, lf_refs, o_ref,
                    qbd_scr, cncol_scr, cns_scr, base_scr, m_scr, l_scr, acc_scr, *, heads, dh, n_new, ppb, page):
    rows = heads * n_new
    width = heads * dh

    rr = lax.broadcasted_iota(jnp.int32, (rows, width), 0)
    cc = lax.broadcasted_iota(jnp.int32, (rows, width), 1)
    bd_mask = (rr // n_new) == (cc // dh)

    @pl.when(j == 0)
    def _():
        q = q_ref[...] * dh ** -0.5
        q_rep = jnp.broadcast_to(q[None], (heads, n_new, width)).reshape(rows, width)
        qbd_scr[...] = jnp.where(bd_mask, q_rep, 0.0).astype(BF16)
        lfnt = lfnt_ref[...]
        lane = lax.broadcasted_iota(jnp.int32, lfnt.shape, 1)
        cnt = jnp.zeros_like(lfnt)
        for s in range(n_new):
            col = jnp.sum(jnp.where(lane <= s, lfnt, 0.0), axis=1, keepdims=True)
            cnt = cnt + jnp.where(lane == s, col, 0.0)
        cns = jnp.broadcast_to(cnt[:, None, :], (heads, n_new, LANES)).reshape(rows, LANES)
        cns_scr[...] = cns
        r2 = lax.broadcasted_iota(jnp.int32, (rows, LANES), 0)
        l2 = lax.broadcasted_iota(jnp.int32, (rows, LANES), 1)
        cncol_scr[...] = jnp.sum(jnp.where(l2 == r2 % n_new, cns, 0.0), axis=1, keepdims=True)
        base_scr[...] = jnp.zeros_like(base_scr)
        m_scr[...] = jnp.full_like(m_scr, NEG)
        l_scr[...] = jnp.zeros_like(l_scr)
        acc_scr[...] = jnp.zeros_like(acc_scr)

    rio = lax.broadcasted_iota(jnp.int32, (page, page), 0)
    cio = lax.broadcasted_iota(jnp.int32, (page, page), 1)
    after = jnp.where(rio > cio, 1.0, 0.0).astype(BF16)
    lf_all = jnp.concatenate([lf_refs[i][...] for i in range(ppb)], axis=0)
    ss = _dot(jnp.concatenate(_split3(lf_all), axis=0), after)
    nr = ppb * heads
    suf_all = ss[0:nr] + ss[nr:2 * nr] + ss[2 * nr:3 * nr]
    tot_all = suf_all[:, 0:1] + lf_all[:, 0:1]
    base = base_scr[:, 0:1]
    biases = []
    for i in range(ppb):
        biases.append(base + suf_all[i * heads:(i + 1) * heads])
        base = base + tot_all[i * heads:(i + 1) * heads]
    base_scr[...] = jnp.broadcast_to(base, base_scr.shape)
    bias = jnp.concatenate(biases, axis=1)
    bias_rows = jnp.broadcast_to(bias[:, None, :], (heads, n_new, ppb * page)).reshape(rows, ppb * page)

    kt = jnp.concatenate([k_refs[i][...].astype(BF16) for i in range(ppb)], axis=1)
    vt = jnp.concatenate([v_refs[i][...].astype(BF16) for i in range(ppb)], axis=1)
    qbd = qbd_scr[...]
    s = _dot(qbd, kt) + cncol_scr[...] + bias_rows
    m_prev = m_scr[...]
    m_new = jnp.maximum(m_prev, jnp.max(s, axis=1, keepdims=True))
    a = jnp.exp(m_prev - m_new)
    p = jnp.exp(s - m_new)
    l_scr[...] = a * l_scr[...] + jnp.sum(p, axis=1, keepdims=True)
    acc_scr[...] = a * acc_scr[...] + _dot_nt(p.astype(BF16), vt)
    m_scr[...] = m_new

    @pl.when(j == n_steps - 1)
    def _():
        pad = jnp.zeros((LANES - n_new, width), F32)
        kn = jnp.concatenate([kn_ref[...], pad], axis=0).astype(BF16)
        vn = jnp.concatenate([vn_ref[...], pad], axis=0).astype(BF16)
        r2 = lax.broadcasted_iota(jnp.int32, (rows, LANES), 0)
        l2 = lax.broadcasted_iota(jnp.int32, (rows, LANES), 1)
        sn = _dot_nt(qbd, kn) + cncol_scr[...] - cns_scr[...]
        sn = jnp.where(l2 <= r2 % n_new, sn, NEG)
        m_p = m_scr[...]
        m_n = jnp.maximum(m_p, jnp.max(sn, axis=1, keepdims=True))
        a2 = jnp.exp(m_p - m_n)
        pn = jnp.exp(sn - m_n)
        l_f = a2 * l_scr[...] + jnp.sum(pn, axis=1, keepdims=True)
        acc_f = a2 * acc_scr[...] + _dot(pn.astype(BF16), vn)
        o = jnp.where(bd_mask, acc_f / l_f, 0.0)
        o_ref[...] = jnp.sum(o.reshape(heads, n_new, width), axis=0)


def _attn_kernel(qi_ref, ki_ref, fb_ref, sq_ref, pg_ref, qat_ref, ka_ref, vt_ref, q_ref, kn_ref, vn_ref, lfnt_ref,
                 ck_hbm, cv_hbm, clf_hbm, of_ref, op_ref, *scr, n_flash, n_paged, n_pg_steps, dh, hps, heads, n_new,
                 ppb, page):
    flash_scr = scr[0:5]
    paged_scr = scr[5:12]
    kbuf, vbuf, lfbuf, ksem, vsem, lfsem = scr[12:]
    i = pl.program_id(0)
    slot = i % 2

    def page_copies(step, slt):
        out = []
        for k in range(ppb):
            pid = pg_ref[step * ppb + k]
            out.append(pltpu.make_async_copy(ck_hbm.at[pid], kbuf.at[slt, k], ksem.at[slt]))
            out.append(pltpu.make_async_copy(cv_hbm.at[pid], vbuf.at[slt, k], vsem.at[slt]))
            out.append(pltpu.make_async_copy(clf_hbm.at[pid], lfbuf.at[slt, k], lfsem.at[slt]))
        return out

    @pl.when(i == 0)
    def _():
        for c in page_copies(0, 0):
            c.start()

    @pl.when(i + 1 < n_paged)
    def _():
        for c in page_copies(i + 1, 1 - slot):
            c.start()

    @pl.when(i < n_flash)
    def _():
        _fox_flash_body(qi_ref[i], ki_ref[i], qat_ref, ka_ref, vt_ref, of_ref, *flash_scr, dh=dh, hps=hps)

    @pl.when(i < n_paged)
    def _():
        for c in page_copies(i, slot):
            c.wait()
        k_refs = [kbuf.at[slot, k] for k in range(ppb)]
        v_refs = [vbuf.at[slot, k] for k in range(ppb)]
        lf_refs = [lfbuf.at[slot, k] for k in range(ppb)]
        _fox_paged_body(i % n_pg_steps, n_pg_steps, q_ref, kn_ref, vn_ref, lfnt_ref, k_refs, v_refs, lf_refs, op_ref,
                        *paged_scr, heads=heads, dh=dh, n_new=n_new, ppb=ppb, page=page)


def _attention(qat, ka, vt, page_table, qf, kf, vf, lfnt_pad, ck, cv, clf, *, batch, seq, tq, n_seq, n_new, n_pages,
               heads, dh, ppb):
    page = ck.shape[-1]
    width = heads * dh
    rows = heads * n_new
    hpl = LANES // dh
    nq = seq // tq
    n_pg_steps = n_pages // ppb
    n_paged = n_seq * n_pg_steps
    tri = [(b, q, k) for b in range(batch) for q in range(nq) for k in range(q + 1)]
    n_flash = len(tri)
    n_grid = max(n_flash, n_paged)
    tri = tri + [tri[-1]] * (n_grid - n_flash)
    fb_a = jnp.asarray(np.array([t[0] for t in tri], np.int32))
    qi_a = jnp.asarray(np.array([t[1] for t in tri], np.int32))
    ki_a = jnp.asarray(np.array([t[2] for t in tri], np.int32))
    pstep = np.minimum(np.arange(n_grid), n_paged - 1)
    sq_np = (pstep // n_pg_steps).astype(np.int32)
    col_np = n_pages - 1 - ((pstep % n_pg_steps)[:, None] * ppb + np.arange(ppb)[None, :])
    sq_a = jnp.asarray(sq_np)
    pg_a = page_table.astype(jnp.int32)[sq_np[:, None], col_np].reshape(-1)

    tok_map = lambda i, qi, ki, fb, sq, pg: (sq[i], 0)
    tok = pl.BlockSpec((n_new, width), tok_map)
    in_specs = [
        pl.BlockSpec((None, heads, LANES, tq), lambda i, qi, ki, fb, sq, pg: (fb[i], 0, 0, qi[i])),
        pl.BlockSpec((None, heads, tq, LANES), lambda i, qi, ki, fb, sq, pg: (fb[i], 0, ki[i], 0)),
        pl.BlockSpec((None, heads // hpl, LANES, tq), lambda i, qi, ki, fb, sq, pg: (fb[i], 0, 0, ki[i])),
        tok, tok, tok,
        pl.BlockSpec((None, heads, LANES), lambda i, qi, ki, fb, sq, pg: (sq[i], 0, 0)),
    ]
    in_specs += [pl.BlockSpec(memory_space=pl.ANY)] * 3
    grid_spec = pltpu.PrefetchScalarGridSpec(
        num_scalar_prefetch=5,
        grid=(n_grid,),
        in_specs=in_specs,
        out_specs=(pl.BlockSpec((tq, width), lambda i, qi, ki, fb, sq, pg: (fb[i] * nq + qi[i], 0)),
                   pl.BlockSpec((n_new, width), tok_map)),
        scratch_shapes=[
            pltpu.VMEM((heads, 1, tq), F32),
            pltpu.VMEM((heads, 1, tq), F32),
            pltpu.VMEM((heads, dh, tq), F32),
            pltpu.VMEM((tq, tq), F32),
            pltpu.VMEM((tq, tq), F32),
            pltpu.VMEM((rows, width), BF16),
            pltpu.VMEM((rows, 1), F32),
            pltpu.VMEM((rows, LANES), F32),
            pltpu.VMEM((heads, LANES), F32),
            pltpu.VMEM((rows, 1), F32),
            pltpu.VMEM((rows, 1), F32),
            pltpu.VMEM((rows, width), F32),
            pltpu.VMEM((2, ppb, width, page), F32),
            pltpu.VMEM((2, ppb, width, page), F32),
            pltpu.VMEM((2, ppb, heads, page), F32),
            pltpu.SemaphoreType.DMA((2,)),
            pltpu.SemaphoreType.DMA((2,)),
            pltpu.SemaphoreType.DMA((2,)),
        ],
    )
    kern = functools.partial(_attn_kernel, n_flash=n_flash, n_paged=n_paged, n_pg_steps=n_pg_steps, dh=dh,
                             hps=heads, heads=heads, n_new=n_new, ppb=ppb, page=page)
    return pl.pallas_call(
        kern,
        grid_spec=grid_spec,
        out_shape=(jax.ShapeDtypeStruct((batch * seq, width), BF16),
                   jax.ShapeDtypeStruct((n_seq * n_new, width), F32)),
        compiler_params=_cparams(("arbitrary",)),
        name="attention",
    )(qi_a, ki_a, fb_a, sq_a, pg_a, qat, ka, vt, qf, kf, vf, lfnt_pad, ck, cv, clf)


def _out_proj_kernel(x_ref, og_ref, of_ref, sga_ref, sgb_ref, wug_ref, wuf_ref, wo_ref, nfw_ref, wr_ref, br_ref,
                     h_ref, xn_ref, gates_ref, *, n_experts, n_groups):
    up_a = _dot(og_ref[...].astype(BF16), wug_ref[...])
    up_b = _dot(of_ref[...].astype(BF16), wuf_ref[...])
    merged = sga_ref[...] * up_a + sgb_ref[...] * up_b
    h = x_ref[...] + _dot(merged.astype(BF16), wo_ref[...])
    h_ref[...] = h
    xn = _rms(h, nfw_ref[...])
    xn_ref[...] = xn.astype(BF16)

    x_hi = xn.astype(BF16)
    x_lo = (xn - x_hi.astype(F32)).astype(BF16)
    l_hi = _dot(x_hi, wr_ref[...])
    logits = l_hi[:, :LANES] + l_hi[:, LANES:] + _dot(x_lo, wr_ref[:, :LANES]) + br_ref[...]
    lane = lax.broadcasted_iota(jnp.int32, logits.shape, 1).astype(F32)
    epg = n_experts // n_groups
    big = 4.0 * LANES
    is_g = (lane >= n_experts) & (lane < n_experts + n_groups)
    gl = jnp.where(is_g, logits, NEG)
    gmax = jnp.max(gl, axis=1, keepdims=True)
    gsum = jnp.sum(jnp.exp(gl - gmax), axis=1, keepdims=True)
    p_g = 1.0 / gsum
    g_sel = jnp.min(jnp.where(gl == gmax, lane, big), axis=1, keepdims=True) - n_experts
    e_lo = g_sel * epg
    in_grp = (lane >= e_lo) & (lane < e_lo + epg)
    el = jnp.where(in_grp, logits, NEG)
    emax = jnp.max(el, axis=1, keepdims=True)
    ee = jnp.exp(el - emax)
    pe = ee / jnp.sum(ee, axis=1, keepdims=True)
    cand = jnp.where(in_grp, pe, -1.0)
    v1 = jnp.max(cand, axis=1, keepdims=True)
    i1 = jnp.min(jnp.where(cand == v1, lane, big), axis=1, keepdims=True)
    cand2 = jnp.where(lane == i1, -1.0, cand)
    v2 = jnp.max(cand2, axis=1, keepdims=True)
    i2 = jnp.min(jnp.where(cand2 == v2, lane, big), axis=1, keepdims=True)
    tot = v1 + v2
    gates_ref[...] = (jnp.where(lane == i1, p_g * (v1 / tot), 0.0) + jnp.where(lane == i2, p_g * (v2 / tot), 0.0)
                      + jnp.where(lane == n_experts, g_sel, 0.0))


def _out_proj(x2, og, of, sga, sgb, wug, wuf, wo, nfw, wr, br, *, n_experts, n_groups, tm):
    t, d_model = x2.shape
    row = lambda i: (i, 0)
    const = lambda i: (0, 0)
    return pl.pallas_call(
        functools.partial(_out_proj_kernel, n_experts=n_experts, n_groups=n_groups),
        grid=(t // tm,),
        in_specs=[
            pl.BlockSpec((tm, d_model), row),
            pl.BlockSpec((tm, og.shape[1]), row),
            pl.BlockSpec((tm, of.shape[1]), row),
            pl.BlockSpec((tm, d_model), row),
            pl.BlockSpec((tm, d_model), row),
            pl.BlockSpec(wug.shape, const, pipeline_mode=pl.Buffered(1)),
            pl.BlockSpec(wuf.shape, const, pipeline_mode=pl.Buffered(1)),
            pl.BlockSpec(wo.shape, const, pipeline_mode=pl.Buffered(1)),
            pl.BlockSpec((1, d_model), const),
            pl.BlockSpec(wr.shape, const),
            pl.BlockSpec((1, LANES), const),
        ],
        out_specs=(
            pl.BlockSpec((tm, d_model), row),
            pl.BlockSpec((tm, d_model), row),
            pl.BlockSpec((tm, LANES), row),
        ),
        out_shape=(
            jax.ShapeDtypeStruct((t, d_model), F32),
            jax.ShapeDtypeStruct((t, d_model), BF16),
            jax.ShapeDtypeStruct((t, LANES), F32),
        ),
        compiler_params=_cparams(("parallel",)),
        name="out_proj",
    )(x2, og, of, sga, sgb, wug, wuf, wo, nfw, wr, br)


def _moe_kernel(xn_ref, gates_ref, h_ref, wg_ref, wu_ref, wd_ref, fnw_ref, y_ref,
                xs_scr, ys_scr, gs_scr, selt_scr, off_smem, *, n_experts, n_groups, chunk, eps):
    e = pl.program_id(1)
    tm = xn_ref.shape[0]
    rows = selt_scr.shape[1]
    rows_p = xs_scr.shape[0]
    epg = n_experts // n_groups

    @pl.when(e == 0)
    def _():
        gates = gates_ref[...]
        lane = lax.broadcasted_iota(jnp.int32, (tm, LANES), 1).astype(F32)
        g_sel = gates[:, n_experts:n_experts + 1]
        in_g = lane == g_sel
        onehot = jnp.where(in_g, 1.0, 0.0).astype(BF16)
        rio = lax.broadcasted_iota(jnp.int32, (tm, tm), 0)
        cio = lax.broadcasted_iota(jnp.int32, (tm, tm), 1)
        tril = jnp.where(cio <= rio, 1.0, 0.0).astype(BF16)
        cum = _dot(tril, onehot)
        cnt = cum[tm - 1:tm, :].astype(jnp.int32)
        lane1 = lax.broadcasted_iota(jnp.int32, (1, LANES), 1)
        off = jnp.int32(0)
        off_vec = jnp.zeros((1, LANES), F32)
        for g in range(n_groups):
            off_smem[g] = off
            off_vec = off_vec + jnp.where(lane1 == g, off.astype(F32), 0.0)
            off = off + ((cnt[0, g] + 7) // 8) * 8
        off_smem[n_groups] = off
        pos = jnp.sum(jnp.where(in_g, off_vec + cum - 1.0, 0.0), axis=1, keepdims=True)
        r_lane = lax.broadcasted_iota(jnp.int32, (tm, rows), 1).astype(F32)
        selt_scr[...] = jnp.where(r_lane == pos, 1.0, 0.0).astype(BF16)
        pos_t = jnp.broadcast_to(pos, (tm, LANES)).T[0:1, :]
        r_sub = lax.broadcasted_iota(jnp.int32, (rows, tm), 0).astype(F32)
        sel = jnp.where(r_sub == pos_t, 1.0, 0.0).astype(BF16)
        xs_scr[0:rows, :] = _dot(sel, xn_ref[...])
        g1, g2, g3 = _split3(gates)
        gs = _dot(sel, jnp.concatenate([g1, g2, g3], axis=1))
        gs_scr[0:rows, :] = gs[:, :LANES] + gs[:, LANES:2 * LANES] + gs[:, 2 * LANES:]
        xs_scr[rows:rows_p, :] = jnp.zeros((rows_p - rows, xs_scr.shape[1]), F32)
        gs_scr[rows:rows_p, :] = jnp.zeros((rows_p - rows, LANES), F32)
        ys_scr[...] = jnp.zeros_like(ys_scr)

    g = (e * eps) // epg
    start = off_smem[g]
    n_rows = off_smem[g + 1] - start
    lane_c = lax.broadcasted_iota(jnp.int32, (chunk, LANES), 1)

    def body(c, carry):
        r0 = pl.multiple_of(start + c * chunk, 8)
        x = xs_scr[pl.ds(r0, chunk), :].astype(BF16)
        gs = gs_scr[pl.ds(r0, chunk), :]
        acc = None
        for i in range(eps):
            gg = _dot(x, wg_ref[i])
            uu = _dot(x, wu_ref[i])
            hid = (gg * _sigmoid(gg)) * uu
            out = _dot(hid.astype(BF16), wd_ref[i])
            gate = jnp.sum(jnp.where(lane_c == e * eps + i, gs, 0.0), axis=1, keepdims=True)
            acc = gate * out if acc is None else acc + gate * out
        ys_scr[pl.ds(r0, chunk), :] += acc
        return carry

    lax.fori_loop(0, (n_rows + chunk - 1) // chunk, body, 0)

    @pl.when(e == pl.num_programs(1) - 1)
    def _():
        ys = ys_scr[0:rows, :]
        hi = ys.astype(BF16)
        lo = (ys - hi.astype(F32)).astype(BF16)
        selt = selt_scr[...]
        moe = _dot(selt, hi) + _dot(selt, lo)
        y_ref[...] = _rms(h_ref[...] + moe, fnw_ref[...])


def _moe(xn, gates, h, wg, wu, wd, fnw, *, tm, n_groups, chunk=MOE_CHUNK_ROWS, eps=MOE_EXPERTS_PER_STEP):
    t, d_model = h.shape
    n_experts, _, d_exp = wg.shape
    rows = tm + LANES
    rows_p = rows + chunk
    row = lambda i, e: (i, 0)
    once = pl.Buffered(1)
    assert (n_experts // n_groups) % eps == 0
    kern = functools.partial(_moe_kernel, n_experts=n_experts, n_groups=n_groups, chunk=chunk, eps=eps)
    return pl.pallas_call(
        kern,
        grid=(t // tm, n_experts // eps),
        in_specs=[
            pl.BlockSpec((tm, d_model), row, pipeline_mode=once),
            pl.BlockSpec((tm, LANES), row, pipeline_mode=once),
            pl.BlockSpec((tm, d_model), row, pipeline_mode=once),
            pl.BlockSpec((eps, d_model, d_exp), lambda i, e: (e, 0, 0)),
            pl.BlockSpec((eps, d_model, d_exp), lambda i, e: (e, 0, 0)),
            pl.BlockSpec((eps, d_exp, d_model), lambda i, e: (e, 0, 0)),
            pl.BlockSpec((1, d_model), lambda i, e: (0, 0)),
        ],
        out_specs=pl.BlockSpec((tm, d_model), row),
        out_shape=jax.ShapeDtypeStruct((t, d_model), F32),
        scratch_shapes=[
            pltpu.VMEM((rows_p, d_model), F32),
            pltpu.VMEM((rows_p, d_model), F32),
            pltpu.VMEM((rows_p, LANES), F32),
            pltpu.VMEM((tm, rows), BF16),
            pltpu.SMEM((n_groups + 1,), jnp.int32),
        ],
        compiler_params=_cparams(("parallel", "arbitrary")),
        name="moe",
    )(xn, gates, h, wg, wu, wd, fnw)


def _pick_tile(n, pref):
    t = min(n, pref)
    while n % t:
        t //= 2
    return t


def kernel(x_prompt, x_sample, state_gla, cache_k, cache_v, cache_logf, page_table, norm_mix_w, w_in, w_gla_a2,
           b_gla_a2, gla_norm_w, b_fox_f, w_up_gla, w_up_fox, w_out, norm_ffn_w, w_router_group, b_router_group,
           w_router_expert, b_router_expert, w_exp_gate, w_exp_up, w_exp_down, final_norm_w):
    depth = w_in.shape[0]
    assert depth == 1
    batch, seq, d_model = x_prompt.shape
    n_seq, n_new, _ = x_sample.shape
    _, _, g_heads, dk, dv = state_gla.shape
    _, n_phys, page, f_heads, dh = cache_k.shape
    n_pages = page_table.shape[1]
    n_lr = w_gla_a2.shape[1]
    n_groups = w_router_group.shape[2]
    n_experts = w_router_expert.shape[2]
    qk = g_heads * dk
    vw = g_heads * dv
    fw = f_heads * dh
    n_gla = 2 * qk + 2 * vw
    n_fox = 3 * fw
    dims = (n_gla, n_fox, d_model, n_lr, f_heads)

    wi = w_in[0]
    o_za = n_gla
    o_fox = o_za + n_lr
    o_fp = o_fox + n_fox
    o_gate = o_fp + f_heads
    w_small = jnp.concatenate([wi[:, o_za:o_fox], wi[:, o_fp:o_gate],
                               jnp.zeros((d_model, LANES - n_lr - f_heads), wi.dtype)], axis=1)
    w_all = jnp.concatenate([wi[:, :o_za], wi[:, o_fox:o_fp], wi[:, o_gate:], w_small], axis=1).astype(BF16)
    w2p = jnp.concatenate([w_gla_a2[0], jnp.zeros((LANES - n_lr, qk), F32)], axis=0).astype(BF16)
    b2 = b_gla_a2[0].reshape(1, qk)
    bfc = b_fox_f[0].reshape(f_heads, 1)
    nw_mix = norm_mix_w[0].reshape(1, d_model)
    nw_gla = gla_norm_w[0].reshape(1, dv)
    wug = w_up_gla[0].astype(BF16)
    wuf = w_up_fox[0].astype(BF16)
    wo = w_out[0].astype(BF16)
    nfw = norm_ffn_w[0].reshape(1, d_model)
    wr32 = jnp.concatenate([w_router_expert[0], w_router_group[0],
                            jnp.zeros((d_model, LANES - n_experts - n_groups), F32)], axis=1)
    wr_hi = wr32.astype(BF16)
    wr = jnp.concatenate([wr_hi, (wr32 - wr_hi.astype(F32)).astype(BF16)], axis=1)
    br = jnp.concatenate([b_router_expert[0], b_router_group[0],
                          jnp.zeros((LANES - n_experts - n_groups,), F32)]).reshape(1, LANES)
    wg = w_exp_gate[0].astype(BF16)
    wu = w_exp_up[0].astype(BF16)
    wd = w_exp_down[0].astype(BF16)
    fnw = final_norm_w.reshape(1, d_model)

    def front(x2, seq_len=None):
        tm = _pick_tile(x2.shape[0] if seq_len is None else seq_len, 512)
        gin, loga, qf, *kv, sga, sgb, lft = _in_proj(x2, nw_mix, w_all, w2p, b2, bfc, dims, tm, seq=seq_len)
        return tm, gin, loga, qf, kv, sga, sgb, lft

    def back(x2, tm, og, of, sga, sgb):
        h, xn, gates = _out_proj(x2, og, of, sga, sgb, wug, wuf, wo, nfw, wr, br,
                                 n_experts=n_experts, n_groups=n_groups, tm=tm)
        return _moe(xn, gates, h, wg, wu, wd, fnw, tm=_pick_tile(x2.shape[0], 1024), n_groups=n_groups)

    xp = x_prompt.reshape(batch * seq, d_model)
    xs = x_sample.reshape(n_seq * n_new, d_model)
    tm_p, gin_p, loga_p, qf_p, (kb_p, kt_p, vt_p, vtb_p), sga_p, sgb_p, lft_p = front(xp, seq)
    tm_s, gin_s, loga_s, qf_s, (k_s, v_s), sga_s, sgb_s, lft_s = front(xs)

    chunk = math.gcd(seq, GLA_CHUNK)
    n_chunks = _pick_tile(seq // chunk, 8)
    og_p, s_p = _gla(gin_p, loga_p, jnp.zeros((batch, g_heads, dk, dv), F32), nw_gla, n_outer=batch,
                     n_steps=seq // (chunk * n_chunks), n_seq=1, n_chunks=n_chunks, chunk=chunk,
                     mx_dtype=BF16, out_dtype=BF16)
    assert math.gcd(n_new, GLA_CHUNK) == n_new
    gs = _pick_tile(n_seq, 8)
    og_s, s_s = _gla(gin_s, loga_s, state_gla[0], nw_gla, n_outer=n_seq // gs, n_steps=1, n_seq=gs,
                     n_chunks=1, chunk=n_new, mx_dtype=F32, out_dtype=F32)

    tq = _pick_tile(seq, 512)
    qat, ka = _fox_prep(qf_p, kb_p, lft_p, batch, seq, f_heads, dh, tq)
    ck = jnp.transpose(cache_k[0], (0, 2, 3, 1)).reshape(n_phys, fw, page)
    cv = jnp.transpose(cache_v[0], (0, 2, 3, 1)).reshape(n_phys, fw, page)
    clf = jnp.transpose(cache_logf[0], (0, 2, 1))
    lfnt = jnp.transpose(lft_s.reshape(f_heads, n_seq, n_new), (1, 0, 2))
    lfnt_pad = jnp.pad(lfnt, ((0, 0), (0, 0), (0, LANES - n_new)))
    of_p, of_s = _attention(qat, ka, vtb_p, page_table, qf_s, k_s, v_s, lfnt_pad, ck, cv, clf, batch=batch, seq=seq,
                            tq=tq, n_seq=n_seq, n_new=n_new, n_pages=n_pages, heads=f_heads, dh=dh,
                            ppb=_pick_tile(n_pages, PAGES_PER_STEP))

    y_p = back(xp, tm_p, og_p, of_p, sga_p, sgb_p)
    y_s = back(xs, tm_s, og_s, of_s, sga_s, sgb_s)
    lf_p = lft_p.T
    lf_s = lft_s.T
    k_p = jnp.transpose(kt_p.reshape(batch, f_heads, dh, seq), (0, 3, 1, 2))
    v_p = jnp.transpose(vt_p.reshape(batch, f_heads, dh, seq), (0, 3, 1, 2))

    return (y_p.reshape(batch, seq, d_model),
            y_s.reshape(n_seq, n_new, d_model),
            s_p[None],
            s_s[None],
            k_p.reshape(1, batch, seq, f_heads, dh),
            v_p.reshape(1, batch, seq, f_heads, dh),
            lf_p.reshape(1, batch, seq, f_heads),
            k_s.reshape(1, n_seq, n_new, f_heads, dh),
            v_s.reshape(1, n_seq, n_new, f_heads, dh),
            lf_s.reshape(1, n_seq, n_new, f_heads))
```

```python
import functools
import math

import jax
import jax.numpy as jnp
import numpy as np
from jax import lax
from jax.experimental import pallas as pl
from jax.experimental.pallas import tpu as pltpu

F32 = jnp.float32
BF16 = jnp.bfloat16

RMS_EPS = 1e-6
GLA_TAU = 16.0
GLA_CHUNK = 64
PAGES_PER_STEP = 32
MOE_CHUNK_ROWS = 288
MOE_EXPERTS_PER_STEP = 4
NEG = -1e30
LOG2E = 1.4426950408889634

V7X_VMEM_LIMIT_BYTES = 56 * 1024 * 1024
LANES = 128


def _cparams(sem):
    return pltpu.CompilerParams(dimension_semantics=sem, vmem_limit_bytes=V7X_VMEM_LIMIT_BYTES)


def _sigmoid(x):
    return 1.0 / (1.0 + jnp.exp(-x))


def _log_sigmoid(x):
    return jnp.minimum(x, 0.0) - jnp.log(1.0 + jnp.exp(-jnp.abs(x)))


def _rms(x, w):
    ms = jnp.mean(x * x, axis=-1, keepdims=True)
    return x * lax.rsqrt(ms + RMS_EPS) * w


def _split3(x):
    p1 = x.astype(BF16)
    r1 = x - p1.astype(F32)
    p2 = r1.astype(BF16)
    p3 = (r1 - p2.astype(F32)).astype(BF16)
    return p1, p2, p3


def _dot(a, b):
    return jnp.dot(a, b, preferred_element_type=F32)


def _dot_nt(a, b):
    return lax.dot_general(a, b, (((1,), (1,)), ((), ())), preferred_element_type=F32)


def _in_proj_kernel(x_ref, nw_ref, w_ref, w2_ref, b2_ref, bf_ref, gla_ref, loga_ref, qf_ref, *refs, dims, kv_t):
    n_gla, n_fox, d_model, n_lr, n_fh = dims
    sga_ref, sgb_ref, lft_ref = refs[-3:]
    xn = _rms(x_ref[...], nw_ref[...]).astype(BF16)

    def mm(lo, hi):
        return _dot(xn, w_ref[:, lo:hi])

    o = 0
    gla_ref[...] = mm(o, o + n_gla)
    o += n_gla
    fw = n_fox // 3
    qf_ref[...] = mm(o, o + fw)
    kf = mm(o + fw, o + 2 * fw)
    vf = mm(o + 2 * fw, o + 3 * fw)
    if kv_t:
        kb_ref, kt_ref, vt_ref, vtb_ref = refs[:4]
        kb_ref[...] = kf.astype(BF16)
        kt_ref[...] = kf.T
        v_t = vf.T
        vt_ref[...] = v_t
        v_tb = v_t.astype(BF16)
        for p in range(fw // LANES):
            vtb_ref[p] = v_tb[p * LANES:(p + 1) * LANES, :]
    else:
        kf_ref, vf_ref = refs[:2]
        kf_ref[...] = kf
        vf_ref[...] = vf
    o += n_fox
    sga_ref[...] = _sigmoid(mm(o, o + d_model))
    sgb_ref[...] = _sigmoid(mm(o + d_model, o + 2 * d_model))
    o += 2 * d_model
    small = mm(o, o + LANES)
    pre = _dot(small.astype(BF16), w2_ref[...]) + b2_ref[...]
    loga_ref[...] = _log_sigmoid(pre) * (1.0 / GLA_TAU)
    small_t = small.T
    lft_ref[...] = _log_sigmoid(small_t[n_lr:n_lr + n_fh, :] + bf_ref[...])


def _in_proj(x2, nw, w_all, w2p, b2, bfc, dims, tm, seq=None):
    n_gla, n_fox, d_model, n_lr, n_fh = dims
    t = x2.shape[0]
    fw = n_fox // 3
    n_qk = w2p.shape[1]
    row = lambda i: (i, 0)
    const = lambda i: (0, 0)
    if seq is None:
        kv_shapes = (jax.ShapeDtypeStruct((t, fw), F32),) * 2
        kv_specs = (pl.BlockSpec((tm, fw), row),) * 2
    else:
        nsb = seq // tm
        batch = t // seq
        kv_shapes = (
            jax.ShapeDtypeStruct((t, fw), BF16),
            jax.ShapeDtypeStruct((batch, fw, seq), F32),
            jax.ShapeDtypeStruct((batch, fw, seq), F32),
            jax.ShapeDtypeStruct((batch, fw // LANES, LANES, seq), BF16),
        )
        kv_specs = (
            pl.BlockSpec((tm, fw), row),
            pl.BlockSpec((None, fw, tm), lambda i: (i // nsb, 0, i % nsb)),
            pl.BlockSpec((None, fw, tm), lambda i: (i // nsb, 0, i % nsb)),
            pl.BlockSpec((None, fw // LANES, LANES, tm), lambda i: (i // nsb, 0, 0, i % nsb)),
        )
    out_shape = (
        jax.ShapeDtypeStruct((t, n_gla), F32),
        jax.ShapeDtypeStruct((t, n_qk), F32),
        jax.ShapeDtypeStruct((t, fw), F32),
        *kv_shapes,
        jax.ShapeDtypeStruct((t, d_model), F32),
        jax.ShapeDtypeStruct((t, d_model), F32),
        jax.ShapeDtypeStruct((n_fh, t), F32),
    )
    return pl.pallas_call(
        functools.partial(_in_proj_kernel, dims=dims, kv_t=seq is not None),
        grid=(t // tm,),
        in_specs=[
            pl.BlockSpec((tm, d_model), row),
            pl.BlockSpec((1, d_model), const),
            pl.BlockSpec(w_all.shape, const, pipeline_mode=pl.Buffered(1)),
            pl.BlockSpec(w2p.shape, const),
            pl.BlockSpec((1, n_qk), const),
            pl.BlockSpec((n_fh, 1), const),
        ],
        out_specs=(
            pl.BlockSpec((tm, n_gla), row),
            pl.BlockSpec((tm, n_qk), row),
            pl.BlockSpec((tm, fw), row),
            *kv_specs,
            pl.BlockSpec((tm, d_model), row),
            pl.BlockSpec((tm, d_model), row),
            pl.BlockSpec((n_fh, tm), lambda i: (0, i)),
        ),
        out_shape=out_shape,
        compiler_params=_cparams(("parallel",)),
        name="in_proj",
    )(x2, nw, w_all, w2p, b2, bfc)


def _gla_kernel(gin_ref, loga_ref, s0_ref, nw_ref, og_ref, sout_ref, s_scr, *,
                chunk, n_chunks, n_seq, heads, dk, dv, mx_dtype):
    j = pl.program_id(1)
    qk = heads * dk
    vw = heads * dv
    kpad = max(chunk, LANES)

    @pl.when(j == 0)
    def _():
        s_scr[...] = s0_ref[...]

    rows = n_seq * n_chunks * chunk
    shift = chunk.bit_length() - 1
    assert chunk == 1 << shift
    nw = nw_ref[...]

    rr = lax.broadcasted_iota(jnp.int32, (rows, rows), 0)
    cc = lax.broadcasted_iota(jnp.int32, (rows, rows), 1)
    same_chunk = (rr >> shift) == (cc >> shift)
    tri = jnp.where(same_chunk, jnp.where(cc <= rr, 1.0, 0.0), 0.0).astype(mx_dtype)
    gs = jnp.concatenate([p.astype(mx_dtype) for p in _split3(loga_ref[...])], axis=1)
    bs = _dot(tri, gs)
    b_all = bs[:, :qk] + bs[:, qk:2 * qk] + bs[:, 2 * qk:]
    k_all = gin_ref[:, qk:2 * qk]
    q_i_all = (gin_ref[:, 0:qk] * jnp.exp(b_all) * dk ** -0.5).astype(mx_dtype)
    k_i_all = k_all * jnp.exp(-b_all)
    vb_all = gin_ref[:, 2 * qk:2 * qk + vw].astype(mx_dtype)
    r_all = gin_ref[:, 2 * qk + vw:2 * qk + 2 * vw]
    silu_r = r_all * _sigmoid(r_all)

    if n_seq == 1 and n_chunks > 1 and rows % LANES == 0:
        ones_bd = jnp.where(same_chunk, 1.0, 0.0).astype(mx_dtype)
        bl = _dot(ones_bd, gs)
        b_tot = bl[:, :qk] + bl[:, qk:2 * qk] + bl[:, 2 * qk:]
        k_end_t = (k_all * jnp.exp(b_tot - b_all)).T.astype(mx_dtype)
        decay_t = jnp.exp(b_tot.T)
        k_i_m = k_i_all.astype(mx_dtype)
        keep = same_chunk & (cc <= rr)
        for h in range(heads):
            qh = q_i_all[:, h * dk:(h + 1) * dk]
            vh = vb_all[:, h * dv:(h + 1) * dv]
            a = jnp.where(keep, _dot_nt(qh, k_i_m[:, h * dk:(h + 1) * dk]), 0.0).astype(mx_dtype)
            o_intra = _dot(a, vh)
            state = s_scr[0, h]
            o_inter = []
            for c in range(n_chunks):
                sl = slice(c * chunk, (c + 1) * chunk)
                o_inter.append(_dot(qh[sl], state.astype(mx_dtype)))
                kv = _dot(k_end_t[h * dk:(h + 1) * dk, sl], vh[sl])
                state = decay_t[h * dk:(h + 1) * dk, c * chunk:c * chunk + 1] * state + kv
            s_scr[0, h] = state
            o = o_intra + jnp.concatenate(o_inter, axis=0)
            og_ref[:, h * dv:(h + 1) * dv] = (_rms(o, nw) * silu_r[:, h * dv:(h + 1) * dv]).astype(og_ref.dtype)

        @pl.when(j == pl.num_programs(1) - 1)
        def _():
            sout_ref[...] = s_scr[...]
        return

    rio = lax.broadcasted_iota(jnp.int32, (chunk, kpad), 0)
    cio = lax.broadcasted_iota(jnp.int32, (chunk, kpad), 1)
    tril = cio <= rio

    def pad_rows(a):
        if kpad == chunk:
            return a
        return jnp.concatenate([a, jnp.zeros((kpad - chunk, a.shape[1]), a.dtype)], axis=0)

    for idx in range(n_seq * n_chunks):
        seq = idx // n_chunks
        sl = slice(idx * chunk, (idx + 1) * chunk)
        b = b_all[sl]
        b_last = b[chunk - 1:chunk, :]
        q_i = q_i_all[sl]
        k_i = pad_rows(k_i_all[sl]).astype(mx_dtype)
        k_end_t = pad_rows(k_all[sl] * jnp.exp(b_last - b)).T.astype(mx_dtype)
        decay = jnp.exp(pad_rows(b).T[:, chunk - 1:chunk])
        vb = pad_rows(vb_all[sl])
        for h in range(heads):
            qh = q_i[:, h * dk:(h + 1) * dk]
            kh = k_i[:, h * dk:(h + 1) * dk]
            vh = vb[:, h * dv:(h + 1) * dv]
            a = jnp.where(tril, _dot_nt(qh, kh), 0.0).astype(mx_dtype)
            s_old = s_scr[seq, h]
            o = _dot(a, vh) + _dot(qh, s_old.astype(mx_dtype))
            kv = _dot(k_end_t[h * dk:(h + 1) * dk, :], vh)
            s_scr[seq, h] = decay[h * dk:(h + 1) * dk, :] * s_old + kv
            og_ref[sl, h * dv:(h + 1) * dv] = (_rms(o, nw) * silu_r[sl, h * dv:(h + 1) * dv]).astype(og_ref.dtype)

    @pl.when(j == pl.num_programs(1) - 1)
    def _():
        sout_ref[...] = s_scr[...]


def _gla(gin, loga, s0, nw, *, n_outer, n_steps, n_seq, n_chunks, chunk, mx_dtype, out_dtype):
    t = gin.shape[0]
    _, heads, dk, dv = s0.shape
    rows = n_seq * n_chunks * chunk
    assert t == n_outer * n_steps * rows
    kern = functools.partial(_gla_kernel, chunk=chunk, n_chunks=n_chunks, n_seq=n_seq, heads=heads,
                             dk=dk, dv=dv, mx_dtype=mx_dtype)
    tok = lambda i, j: (i * n_steps + j, 0)
    st = lambda i, j: (i, 0, 0, 0)
    return pl.pallas_call(
        kern,
        grid=(n_outer, n_steps),
        in_specs=[
            pl.BlockSpec((rows, gin.shape[1]), tok),
            pl.BlockSpec((rows, loga.shape[1]), tok),
            pl.BlockSpec((n_seq, heads, dk, dv), st),
            pl.BlockSpec((1, dv), lambda i, j: (0, 0)),
        ],
        out_specs=(
            pl.BlockSpec((rows, heads * dv), tok),
            pl.BlockSpec((n_seq, heads, dk, dv), st),
        ),
        out_shape=(
            jax.ShapeDtypeStruct((t, heads * dv), out_dtype),
            jax.ShapeDtypeStruct(s0.shape, F32),
        ),
        scratch_shapes=[pltpu.VMEM((n_seq, heads, dk, dv), F32)],
        compiler_params=_cparams(("parallel", "arbitrary")),
        name="gla",
    )(gin, loga, s0, nw)


def _fox_prep_kernel(qf_ref, kf_ref, lft_ref, rq_ref, rk_ref, qat_ref, ka_ref, carry_scr, *, heads, dh):
    j = pl.program_id(1)
    tp = qf_ref.shape[0]

    @pl.when(j == 0)
    def _():
        carry_scr[...] = jnp.zeros_like(carry_scr)

    rio = lax.broadcasted_iota(jnp.int32, (tp, tp), 0)
    cio = lax.broadcasted_iota(jnp.int32, (tp, tp), 1)
    upper = jnp.where(rio <= cio, 1.0, 0.0).astype(BF16)
    l1, l2, l3 = _split3(lft_ref[...])
    cs = _dot(jnp.concatenate([l1, l2, l3, jnp.zeros_like(l1)], axis=0), upper)
    c_t = cs[0:heads] + cs[heads:2 * heads] + cs[2 * heads:3 * heads] + carry_scr[:, 0:1]
    carry_scr[...] = jnp.broadcast_to(c_t[:, tp - 1:tp], carry_scr.shape)
    c1, c2, c3 = _split3(c_t * LOG2E)
    pad_rows = LANES - 3 * heads - 8
    aug_t = jnp.concatenate([c1.astype(F32), c2.astype(F32), c3.astype(F32),
                             jnp.ones((8, tp), F32), jnp.zeros((pad_rows, tp), F32)], axis=0)
    aug = aug_t.T.astype(BF16)
    lq_t = jnp.concatenate([(qf_ref[...].T * (dh ** -0.5 * LOG2E)).astype(BF16), aug_t.astype(BF16)], axis=0)
    lk = jnp.concatenate([kf_ref[...].astype(BF16), aug], axis=1)
    qa_t = _dot(rq_ref[...], lq_t).astype(BF16)
    ka = _dot(lk, rk_ref[...]).astype(BF16)
    for h in range(heads):
        qat_ref[h] = qa_t[h * LANES:(h + 1) * LANES, :]
        ka_ref[h] = ka[:, h * LANES:(h + 1) * LANES]


def _placement_matrices(heads, dh):
    kin = heads * dh + LANES
    rq = np.zeros((kin, heads * LANES), np.float32)
    rk = np.zeros((kin, heads * LANES), np.float32)
    base = heads * dh
    ones_row = base + 3 * heads
    for h in range(heads):
        for d in range(dh):
            rq[h * dh + d, h * LANES + d] = 1.0
            rk[h * dh + d, h * LANES + d] = 1.0
        for p in range(3):
            rq[base + p * heads + h, h * LANES + dh + p] = 1.0
            rq[ones_row, h * LANES + dh + 3 + p] = 1.0
            rk[ones_row, h * LANES + dh + p] = 1.0
            rk[base + p * heads + h, h * LANES + dh + 3 + p] = -1.0
    return jnp.asarray(rq.T, BF16), jnp.asarray(rk, BF16)


def _fox_prep(qf, kf, lft, batch, seq, heads, dh, tp):
    rq, rk = _placement_matrices(heads, dh)
    n_steps = seq // tp
    tok = lambda b, j: (b * n_steps + j, 0)
    const = lambda b, j: (0, 0)
    return pl.pallas_call(
        functools.partial(_fox_prep_kernel, heads=heads, dh=dh),
        grid=(batch, n_steps),
        in_specs=[
            pl.BlockSpec((tp, heads * dh), tok),
            pl.BlockSpec((tp, heads * dh), tok),
            pl.BlockSpec((heads, tp), lambda b, j: (0, b * n_steps + j)),
            pl.BlockSpec(rq.shape, const),
            pl.BlockSpec(rk.shape, const),
        ],
        out_specs=(
            pl.BlockSpec((None, heads, LANES, tp), lambda b, j: (b, 0, 0, j)),
            pl.BlockSpec((None, heads, tp, LANES), lambda b, j: (b, 0, j, 0)),
        ),
        out_shape=(
            jax.ShapeDtypeStruct((batch, heads, LANES, seq), BF16),
            jax.ShapeDtypeStruct((batch, heads, seq, LANES), BF16),
        ),
        scratch_shapes=[pltpu.VMEM((heads, LANES), F32)],
        compiler_params=_cparams(("parallel", "arbitrary")),
        name="fox_prep",
    )(qf, kf, lft, rq, rk)


def _fox_flash_body(qi, ki, qat_ref, ka_ref, vt_ref, o_ref, m_scr, l_scr, acc_scr, s0_scr, s1_scr, *, dh, hps):
    tq = qat_ref.shape[2]
    tk = ka_ref.shape[1]
    hpl = LANES // dh

    @pl.when(ki == 0)
    def _():
        m_scr[...] = jnp.full_like(m_scr, NEG)
        l_scr[...] = jnp.zeros_like(l_scr)
        acc_scr[...] = jnp.zeros_like(acc_scr)

    def update(masked):
        def scores(h, buf):
            buf[...] = _dot(ka_ref[h], qat_ref[h])

        def soft_pv(h, buf):
            s = buf[...]
            if masked:
                rio = lax.broadcasted_iota(jnp.int32, (tk, tq), 0)
                cio = lax.broadcasted_iota(jnp.int32, (tk, tq), 1)
                s = jnp.where(rio <= cio, s, NEG)
            m_prev = m_scr[h]
            m_new = jnp.maximum(m_prev, jnp.max(s, axis=0, keepdims=True))
            a = jnp.exp2(m_prev - m_new)
            p = jnp.exp2(s - m_new)
            l_scr[h] = a * l_scr[h] + jnp.sum(p, axis=0, keepdims=True)
            v_t = vt_ref[h // hpl, pl.ds(pl.multiple_of((h % hpl) * dh, dh), dh), :]
            acc_scr[h] = a * acc_scr[h] + _dot(v_t, p.astype(BF16))
            m_scr[h] = m_new

        scores(0, s0_scr)

        def pair(i, carry):
            scores(2 * i + 1, s1_scr)
            soft_pv(2 * i, s0_scr)
            scores(jnp.minimum(2 * i + 2, hps - 1), s0_scr)
            soft_pv(2 * i + 1, s1_scr)
            return carry

        lax.fori_loop(0, hps // 2, pair, 0)

    @pl.when(ki < qi)
    def _():
        update(False)

    @pl.when(ki == qi)
    def _():
        update(True)
        o_t = jnp.concatenate([acc_scr[h] / l_scr[h] for h in range(hps)], axis=0)
        o_ref[...] = o_t.T.astype(o_ref.dtype)


def _fox_paged_body(j, n_steps, q_ref, kn_ref, vn_ref, lfnt_ref, k_refs, v_refs, lf_refs, o_ref,
                    qbd_scr, cncol_scr, cns_scr, base_scr, m_scr, l_scr, acc_scr, *, heads, dh, n_new, ppb, page):
    rows = heads * n_new
    width = heads * dh

    rr = lax.broadcasted_iota(jnp.int32, (rows, width), 0)
    cc = lax.broadcasted_iota(jnp.int32, (rows, width), 1)
    bd_mask = (rr // n_new) == (cc // dh)

    @pl.when(j == 0)
    def _():
        q = q_ref[...] * dh ** -0.5
        q_rep = jnp.broadcast_to(q[None], (heads, n_new, width)).reshape(rows, width)
        qbd_scr[...] = jnp.where(bd_mask, q_rep, 0.0).astype(BF16)
        lfnt = lfnt_ref[...]
        lane = lax.broadcasted_iota(jnp.int32, lfnt.shape, 1)
        cnt = jnp.zeros_like(lfnt)
        for s in range(n_new):
            col = jnp.sum(jnp.where(lane <= s, lfnt, 0.0), axis=1, keepdims=True)
            cnt = cnt + jnp.where(lane == s, col, 0.0)
        cns = jnp.broadcast_to(cnt[:, None, :], (heads, n_new, LANES)).reshape(rows, LANES)
        cns_scr[...] = cns
        r2 = lax.broadcasted_iota(jnp.int32, (rows, LANES), 0)
        l2 = lax.broadcasted_iota(jnp.int32, (rows, LANES), 1)
        cncol_scr[...] = jnp.sum(jnp.where(l2 == r2 % n_new, cns, 0.0), axis=1, keepdims=True)
        base_scr[...] = jnp.zeros_like(base_scr)
        m_scr[...] = jnp.full_like(m_scr, NEG)
        l_scr[...] = jnp.zeros_like(l_scr)
        acc_scr[...] = jnp.zeros_like(acc_scr)

    rio = lax.broadcasted_iota(jnp.int32, (page, page), 0)
    cio = lax.broadcasted_iota(jnp.int32, (page, page), 1)
    after = jnp.where(rio > cio, 1.0, 0.0).astype(BF16)
    lf_all = jnp.concatenate([lf_refs[i][...] for i in range(ppb)], axis=0)
    ss = _dot(jnp.concatenate(_split3(lf_all), axis=0), after)
    nr = ppb * heads
    suf_all = ss[0:nr] + ss[nr:2 * nr] + ss[2 * nr:3 * nr]
    tot_all = suf_all[:, 0:1] + lf_all[:, 0:1]
    base = base_scr[:, 0:1]
    biases = []
    for i in range(ppb):
        biases.append(base + suf_all[i * heads:(i + 1) * heads])
        base = base + tot_all[i * heads:(i + 1) * heads]
    base_scr[...] = jnp.broadcast_to(base, base_scr.shape)
    bias = jnp.concatenate(biases, axis=1)
    bias_rows = jnp.broadcast_to(bias[:, None, :], (heads, n_new, ppb * page)).reshape(rows, ppb * page)

    kt = jnp.concatenate([k_refs[i][...].astype(BF16) for i in range(ppb)], axis=1)
    vt = jnp.concatenate([v_refs[i][...].astype(BF16) for i in range(ppb)], axis=1)
    qbd = qbd_scr[...]
    s = _dot(qbd, kt) + cncol_scr[...] + bias_rows
    m_prev = m_scr[...]
    m_new = jnp.maximum(m_prev, jnp.max(s, axis=1, keepdims=True))
    a = jnp.exp(m_prev - m_new)
    p = jnp.exp(s - m_new)
    l_scr[...] = a * l_scr[...] + jnp.sum(p, axis=1, keepdims=True)
    acc_scr[...] = a * acc_scr[...] + _dot_nt(p.astype(BF16), vt)
    m_scr[...] = m_new

    @pl.when(j == n_steps - 1)
    def _():
        pad = jnp.zeros((LANES - n_new, width), F32)
        kn = jnp.concatenate([kn_ref[...], pad], axis=0).astype(BF16)
        vn = jnp.concatenate([vn_ref[...], pad], axis=0).astype(BF16)
        r2 = lax.broadcasted_iota(jnp.int32, (rows, LANES), 0)
        l2 = lax.broadcasted_iota(jnp.int32, (rows, LANES), 1)
        sn = _dot_nt(qbd, kn) + cncol_scr[...] - cns_scr[...]
        sn = jnp.where(l2 <= r2 % n_new, sn, NEG)
        m_p = m_scr[...]
        m_n = jnp.maximum(m_p, jnp.max(sn, axis=1, keepdims=True))
        a2 = jnp.exp(m_p - m_n)
        pn = jnp.exp(sn - m_n)
        l_f = a2 * l_scr[...] + jnp.sum(pn, axis=1, keepdims=True)
        acc_f = a2 * acc_scr[...] + _dot(pn.astype(BF16), vn)
        o = jnp.where(bd_mask, acc_f / l_f, 0.0)
        o_ref[...] = jnp.sum(o.reshape(heads, n_new, width), axis=0)


def _attn_kernel(qi_ref, ki_ref, fb_ref, sq_ref, pg_ref, qat_ref, ka_ref, vt_ref, q_ref, kn_ref, vn_ref, lfnt_ref,
                 ck_hbm, cv_hbm, clf_hbm, of_ref, op_ref, *scr, n_flash, n_paged, n_pg_steps, dh, hps, heads, n_new,
                 ppb, page):
    flash_scr = scr[0:5]
    paged_scr = scr[5:12]
    kbuf, vbuf, lfbuf, ksem, vsem, lfsem = scr[12:]
    i = pl.program_id(0)
    slot = i % 2

    def page_copies(step, slt):
        out = []
        for k in range(ppb):
            pid = pg_ref[step * ppb + k]
            out.append(pltpu.make_async_copy(ck_hbm.at[pid], kbuf.at[slt, k], ksem.at[slt]))
            out.append(pltpu.make_async_copy(cv_hbm.at[pid], vbuf.at[slt, k], vsem.at[slt]))
            out.append(pltpu.make_async_copy(clf_hbm.at[pid], lfbuf.at[slt, k], lfsem.at[slt]))
        return out

    @pl.when(i == 0)
    def _():
        for c in page_copies(0, 0):
            c.start()

    @pl.when(i + 1 < n_paged)
    def _():
        for c in page_copies(i + 1, 1 - slot):
            c.start()

    @pl.when(i < n_flash)
    def _():
        _fox_flash_body(qi_ref[i], ki_ref[i], qat_ref, ka_ref, vt_ref, of_ref, *flash_scr, dh=dh, hps=hps)

    @pl.when(i < n_paged)
    def _():
        for c in page_copies(i, slot):
            c.wait()
        k_refs = [kbuf.at[slot, k] for k in range(ppb)]
        v_refs = [vbuf.at[slot, k] for k in range(ppb)]
        lf_refs = [lfbuf.at[slot, k] for k in range(ppb)]
        _fox_paged_body(i % n_pg_steps, n_pg_steps, q_ref, kn_ref, vn_ref, lfnt_ref, k_refs, v_refs, lf_refs, op_ref,
                        *paged_scr, heads=heads, dh=dh, n_new=n_new, ppb=ppb, page=page)


def _attention(qat, ka, vt, page_table, qf, kf, vf, lfnt_pad, ck, cv, clf, *, batch, seq, tq, n_seq, n_new, n_pages,
               heads, dh, ppb):
    page = ck.shape[-1]
    width = heads * dh
    rows = heads * n_new
    hpl = LANES // dh
    nq = seq // tq
    n_pg_steps = n_pages // ppb
    n_paged = n_seq * n_pg_steps
    tri = [(b, q, k) for b in range(batch) for q in range(nq) for k in range(q + 1)]
    n_flash = len(tri)
    n_grid = max(n_flash, n_paged)
    tri = tri + [tri[-1]] * (n_grid - n_flash)
    fb_a = jnp.asarray(np.array([t[0] for t in tri], np.int32))
    qi_a = jnp.asarray(np.array([t[1] for t in tri], np.int32))
    ki_a = jnp.asarray(np.array([t[2] for t in tri], np.int32))
    pstep = np.minimum(np.arange(n_grid), n_paged - 1)
    sq_np = (pstep // n_pg_steps).astype(np.int32)
    col_np = n_pages - 1 - ((pstep % n_pg_steps)[:, None] * ppb + np.arange(ppb)[None, :])
    sq_a = jnp.asarray(sq_np)
    pg_a = page_table.astype(jnp.int32)[sq_np[:, None], col_np].reshape(-1)

    tok_map = lambda i, qi, ki, fb, sq, pg: (sq[i], 0)
    tok = pl.BlockSpec((n_new, width), tok_map)
    in_specs = [
        pl.BlockSpec((None, heads, LANES, tq), lambda i, qi, ki, fb, sq, pg: (fb[i], 0, 0, qi[i])),
        pl.BlockSpec((None, heads, tq, LANES), lambda i, qi, ki, fb, sq, pg: (fb[i], 0, ki[i], 0)),
        pl.BlockSpec((None, heads // hpl, LANES, tq), lambda i, qi, ki, fb, sq, pg: (fb[i], 0, 0, ki[i])),
        tok, tok, tok,
        pl.BlockSpec((None, heads, LANES), lambda i, qi, ki, fb, sq, pg: (sq[i], 0, 0)),
    ]
    in_specs += [pl.BlockSpec(memory_space=pl.ANY)] * 3
    grid_spec = pltpu.PrefetchScalarGridSpec(
        num_scalar_prefetch=5,
        grid=(n_grid,),
        in_specs=in_specs,
        out_specs=(pl.BlockSpec((tq, width), lambda i, qi, ki, fb, sq, pg: (fb[i] * nq + qi[i], 0)),
                   pl.BlockSpec((n_new, width), tok_map)),
        scratch_shapes=[
            pltpu.VMEM((heads, 1, tq), F32),
            pltpu.VMEM((heads, 1, tq), F32),
            pltpu.VMEM((heads, dh, tq), F32),
            pltpu.VMEM((tq, tq), F32),
            pltpu.VMEM((tq, tq), F32),
            pltpu.VMEM((rows, width), BF16),
            pltpu.VMEM((rows, 1), F32),
            pltpu.VMEM((rows, LANES), F32),
            pltpu.VMEM((heads, LANES), F32),
            pltpu.VMEM((rows, 1), F32),
            pltpu.VMEM((rows, 1), F32),
            pltpu.VMEM((rows, width), F32),
            pltpu.VMEM((2, ppb, width, page), F32),
            pltpu.VMEM((2, ppb, width, page), F32),
            pltpu.VMEM((2, ppb, heads, page), F32),
            pltpu.SemaphoreType.DMA((2,)),
            pltpu.SemaphoreType.DMA((2,)),
            pltpu.SemaphoreType.DMA((2,)),
        ],
    )
    kern = functools.partial(_attn_kernel, n_flash=n_flash, n_paged=n_paged, n_pg_steps=n_pg_steps, dh=dh,
                             hps=heads, heads=heads, n_new=n_new, ppb=ppb, page=page)
    return pl.pallas_call(
        kern,
        grid_spec=grid_spec,
        out_shape=(jax.ShapeDtypeStruct((batch * seq, width), BF16),
                   jax.ShapeDtypeStruct((n_seq * n_new, width), F32)),
        compiler_params=_cparams(("arbitrary",)),
        name="attention",
    )(qi_a, ki_a, fb_a, sq_a, pg_a, qat, ka, vt, qf, kf, vf, lfnt_pad, ck, cv, clf)


def _out_proj_kernel(x_ref, og_ref, of_ref, sga_ref, sgb_ref, wug_ref, wuf_ref, wo_ref, nfw_ref, wr_ref, br_ref,
                     h_ref, xn_ref, gates_ref, *, n_experts, n_groups):
    up_a = _dot(og_ref[...].astype(BF16), wug_ref[...])
    up_b = _dot(of_ref[...].astype(BF16), wuf_ref[...])
    merged = sga_ref[...] * up_a + sgb_ref[...] * up_b
    h = x_ref[...] + _dot(merged.astype(BF16), wo_ref[...])
    h_ref[...] = h
    xn = _rms(h, nfw_ref[...])
    xn_ref[...] = xn.astype(BF16)

    x_hi = xn.astype(BF16)
    x_lo = (xn - x_hi.astype(F32)).astype(BF16)
    l_hi = _dot(x_hi, wr_ref[...])
    logits = l_hi[:, :LANES] + l_hi[:, LANES:] + _dot(x_lo, wr_ref[:, :LANES]) + br_ref[...]
    lane = lax.broadcasted_iota(jnp.int32, logits.shape, 1).astype(F32)
    epg = n_experts // n_groups
    big = 4.0 * LANES
    is_g = (lane >= n_experts) & (lane < n_experts + n_groups)
    gl = jnp.where(is_g, logits, NEG)
    gmax = jnp.max(gl, axis=1, keepdims=True)
    gsum = jnp.sum(jnp.exp(gl - gmax), axis=1, keepdims=True)
    p_g = 1.0 / gsum
    g_sel = jnp.min(jnp.where(gl == gmax, lane, big), axis=1, keepdims=True) - n_experts
    e_lo = g_sel * epg
    in_grp = (lane >= e_lo) & (lane < e_lo + epg)
    el = jnp.where(in_grp, logits, NEG)
    emax = jnp.max(el, axis=1, keepdims=True)
    ee = jnp.exp(el - emax)
    pe = ee / jnp.sum(ee, axis=1, keepdims=True)
    cand = jnp.where(in_grp, pe, -1.0)
    v1 = jnp.max(cand, axis=1, keepdims=True)
    i1 = jnp.min(jnp.where(cand == v1, lane, big), axis=1, keepdims=True)
    cand2 = jnp.where(lane == i1, -1.0, cand)
    v2 = jnp.max(cand2, axis=1, keepdims=True)
    i2 = jnp.min(jnp.where(cand2 == v2, lane, big), axis=1, keepdims=True)
    tot = v1 + v2
    gates_ref[...] = (jnp.where(lane == i1, p_g * (v1 / tot), 0.0) + jnp.where(lane == i2, p_g * (v2 / tot), 0.0)
                      + jnp.where(lane == n_experts, g_sel, 0.0))


def _out_proj(x2, og, of, sga, sgb, wug, wuf, wo, nfw, wr, br, *, n_experts, n_groups, tm):
    t, d_model = x2.shape
    row = lambda i: (i, 0)
    const = lambda i: (0, 0)
    return pl.pallas_call(
        functools.partial(_out_proj_kernel, n_experts=n_experts, n_groups=n_groups),
        grid=(t // tm,),
        in_specs=[
            pl.BlockSpec((tm, d_model), row),
            pl.BlockSpec((tm, og.shape[1]), row),
            pl.BlockSpec((tm, of.shape[1]), row),
            pl.BlockSpec((tm, d_model), row),
            pl.BlockSpec((tm, d_model), row),
            pl.BlockSpec(wug.shape, const, pipeline_mode=pl.Buffered(1)),
            pl.BlockSpec(wuf.shape, const, pipeline_mode=pl.Buffered(1)),
            pl.BlockSpec(wo.shape, const, pipeline_mode=pl.Buffered(1)),
            pl.BlockSpec((1, d_model), const),
            pl.BlockSpec(wr.shape, const),
            pl.BlockSpec((1, LANES), const),
        ],
        out_specs=(
            pl.BlockSpec((tm, d_model), row),
            pl.BlockSpec((tm, d_model), row),
            pl.BlockSpec((tm, LANES), row),
        ),
        out_shape=(
            jax.ShapeDtypeStruct((t, d_model), F32),
            jax.ShapeDtypeStruct((t, d_model), BF16),
            jax.ShapeDtypeStruct((t, LANES), F32),
        ),
        compiler_params=_cparams(("parallel",)),
        name="out_proj",
    )(x2, og, of, sga, sgb, wug, wuf, wo, nfw, wr, br)


def _moe_kernel(xn_ref, gates_ref, h_ref, wg_ref, wu_ref, wd_ref, fnw_ref, y_ref,
                xs_scr, ys_scr, gs_scr, selt_scr, off_smem, *, n_experts, n_groups, chunk, eps):
    e = pl.program_id(1)
    tm = xn_ref.shape[0]
    rows = selt_scr.shape[1]
    rows_p = xs_scr.shape[0]
    epg = n_experts // n_groups

    @pl.when(e == 0)
    def _():
        gates = gates_ref[...]
        lane = lax.broadcasted_iota(jnp.int32, (tm, LANES), 1).astype(F32)
        g_sel = gates[:, n_experts:n_experts + 1]
        in_g = lane == g_sel
        onehot = jnp.where(in_g, 1.0, 0.0).astype(BF16)
        rio = lax.broadcasted_iota(jnp.int32, (tm, tm), 0)
        cio = lax.broadcasted_iota(jnp.int32, (tm, tm), 1)
        tril = jnp.where(cio <= rio, 1.0, 0.0).astype(BF16)
        cum = _dot(tril, onehot)
        cnt = cum[tm - 1:tm, :].astype(jnp.int32)
        lane1 = lax.broadcasted_iota(jnp.int32, (1, LANES), 1)
        off = jnp.int32(0)
        off_vec = jnp.zeros((1, LANES), F32)
        for g in range(n_groups):
            off_smem[g] = off
            off_vec = off_vec + jnp.where(lane1 == g, off.astype(F32), 0.0)
            off = off + ((cnt[0, g] + 7) // 8) * 8
        off_smem[n_groups] = off
        pos = jnp.sum(jnp.where(in_g, off_vec + cum - 1.0, 0.0), axis=1, keepdims=True)
        r_lane = lax.broadcasted_iota(jnp.int32, (tm, rows), 1).astype(F32)
        selt_scr[...] = jnp.where(r_lane == pos, 1.0, 0.0).astype(BF16)
        pos_t = jnp.broadcast_to(pos, (tm, LANES)).T[0:1, :]
        r_sub = lax.broadcasted_iota(jnp.int32, (rows, tm), 0).astype(F32)
        sel = jnp.where(r_sub == pos_t, 1.0, 0.0).astype(BF16)
        xs_scr[0:rows, :] = _dot(sel, xn_ref[...])
        g1, g2, g3 = _split3(gates)
        gs = _dot(sel, jnp.concatenate([g1, g2, g3], axis=1))
        gs_scr[0:rows, :] = gs[:, :LANES] + gs[:, LANES:2 * LANES] + gs[:, 2 * LANES:]
        xs_scr[rows:rows_p, :] = jnp.zeros((rows_p - rows, xs_scr.shape[1]), F32)
        gs_scr[rows:rows_p, :] = jnp.zeros((rows_p - rows, LANES), F32)
        ys_scr[...] = jnp.zeros_like(ys_scr)

    g = (e * eps) // epg
    start = off_smem[g]
    n_rows = off_smem[g + 1] - start
    lane_c = lax.broadcasted_iota(jnp.int32, (chunk, LANES), 1)

    def body(c, carry):
        r0 = pl.multiple_of(start + c * chunk, 8)
        x = xs_scr[pl.ds(r0, chunk), :].astype(BF16)
        gs = gs_scr[pl.ds(r0, chunk), :]
        acc = None
        for i in range(eps):
            gg = _dot(x, wg_ref[i])
            uu = _dot(x, wu_ref[i])
            hid = (gg * _sigmoid(gg)) * uu
            out = _dot(hid.astype(BF16), wd_ref[i])
            gate = jnp.sum(jnp.where(lane_c == e * eps + i, gs, 0.0), axis=1, keepdims=True)
            acc = gate * out if acc is None else acc + gate * out
        ys_scr[pl.ds(r0, chunk), :] += acc
        return carry

    lax.fori_loop(0, (n_rows + chunk - 1) // chunk, body, 0)

    @pl.when(e == pl.num_programs(1) - 1)
    def _():
        ys = ys_scr[0:rows, :]
        hi = ys.astype(BF16)
        lo = (ys - hi.astype(F32)).astype(BF16)
        selt = selt_scr[...]
        moe = _dot(selt, hi) + _dot(selt, lo)
        y_ref[...] = _rms(h_ref[...] + moe, fnw_ref[...])


def _moe(xn, gates, h, wg, wu, wd, fnw, *, tm, n_groups, chunk=MOE_CHUNK_ROWS, eps=MOE_EXPERTS_PER_STEP):
    t, d_model = h.shape
    n_experts, _, d_exp = wg.shape
    rows = tm + LANES
    rows_p = rows + chunk
    row = lambda i, e: (i, 0)
    once = pl.Buffered(1)
    assert (n_experts // n_groups) % eps == 0
    kern = functools.partial(_moe_kernel, n_experts=n_experts, n_groups=n_groups, chunk=chunk, eps=eps)
    return pl.pallas_call(
        kern,
        grid=(t // tm, n_experts // eps),
        in_specs=[
            pl.BlockSpec((tm, d_model), row, pipeline_mode=once),
            pl.BlockSpec((tm, LANES), row, pipeline_mode=once),
            pl.BlockSpec((tm, d_model), row, pipeline_mode=once),
            pl.BlockSpec((eps, d_model, d_exp), lambda i, e: (e, 0, 0)),
            pl.BlockSpec((eps, d_model, d_exp), lambda i, e: (e, 0, 0)),
            pl.BlockSpec((eps, d_exp, d_model), lambda i, e: (e, 0, 0)),
            pl.BlockSpec((1, d_model), lambda i, e: (0, 0)),
        ],
        out_specs=pl.BlockSpec((tm, d_model), row),
        out_shape=jax.ShapeDtypeStruct((t, d_model), F32),
        scratch_shapes=[
            pltpu.VMEM((rows_p, d_model), F32),
            pltpu.VMEM((rows_p, d_model), F32),
            pltpu.VMEM((rows_p, LANES), F32),
            pltpu.VMEM((tm, rows), BF16),
            pltpu.SMEM((n_groups + 1,), jnp.int32),
        ],
        compiler_params=_cparams(("parallel", "arbitrary")),
        name="moe",
    )(xn, gates, h, wg, wu, wd, fnw)


def _pick_tile(n, pref):
    t = min(n, pref)
    while n % t:
        t //= 2
    return t


def kernel(x_prompt, x_sample, state_gla, cache_k, cache_v, cache_logf, page_table, norm_mix_w, w_in, w_gla_a2,
           b_gla_a2, gla_norm_w, b_fox_f, w_up_gla, w_up_fox, w_out, norm_ffn_w, w_router_group, b_router_group,
           w_router_expert, b_router_expert, w_exp_gate, w_exp_up, w_exp_down, final_norm_w):
    depth = w_in.shape[0]
    assert depth == 1
    batch, seq, d_model = x_prompt.shape
    n_seq, n_new, _ = x_sample.shape
    _, _, g_heads, dk, dv = state_gla.shape
    _, n_phys, page, f_heads, dh = cache_k.shape
    n_pages = page_table.shape[1]
    n_lr = w_gla_a2.shape[1]
    n_groups = w_router_group.shape[2]
    n_experts = w_router_expert.shape[2]
    qk = g_heads * dk
    vw = g_heads * dv
    fw = f_heads * dh
    n_gla = 2 * qk + 2 * vw
    n_fox = 3 * fw
    dims = (n_gla, n_fox, d_model, n_lr, f_heads)

    wi = w_in[0]
    o_za = n_gla
    o_fox = o_za + n_lr
    o_fp = o_fox + n_fox
    o_gate = o_fp + f_heads
    w_small = jnp.concatenate([wi[:, o_za:o_fox], wi[:, o_fp:o_gate],
                               jnp.zeros((d_model, LANES - n_lr - f_heads), wi.dtype)], axis=1)
    w_all = jnp.concatenate([wi[:, :o_za], wi[:, o_fox:o_fp], wi[:, o_gate:], w_small], axis=1).astype(BF16)
    w2p = jnp.concatenate([w_gla_a2[0], jnp.zeros((LANES - n_lr, qk), F32)], axis=0).astype(BF16)
    b2 = b_gla_a2[0].reshape(1, qk)
    bfc = b_fox_f[0].reshape(f_heads, 1)
    nw_mix = norm_mix_w[0].reshape(1, d_model)
    nw_gla = gla_norm_w[0].reshape(1, dv)
    wug = w_up_gla[0].astype(BF16)
    wuf = w_up_fox[0].astype(BF16)
    wo = w_out[0].astype(BF16)
    nfw = norm_ffn_w[0].reshape(1, d_model)
    wr32 = jnp.concatenate([w_router_expert[0], w_router_group[0],
                            jnp.zeros((d_model, LANES - n_experts - n_groups), F32)], axis=1)
    wr_hi = wr32.astype(BF16)
    wr = jnp.concatenate([wr_hi, (wr32 - wr_hi.astype(F32)).astype(BF16)], axis=1)
    br = jnp.concatenate([b_router_expert[0], b_router_group[0],
                          jnp.zeros((LANES - n_experts - n_groups,), F32)]).reshape(1, LANES)
    wg = w_exp_gate[0].astype(BF16)
    wu = w_exp_up[0].astype(BF16)
    wd = w_exp_down[0].astype(BF16)
    fnw = final_norm_w.reshape(1, d_model)

    def front(x2, seq_len=None):
        tm = _pick_tile(x2.shape[0] if seq_len is None else seq_len, 512)
        gin, loga, qf, *kv, sga, sgb, lft = _in_proj(x2, nw_mix, w_all, w2p, b2, bfc, dims, tm, seq=seq_len)
        return tm, gin, loga, qf, kv, sga, sgb, lft

    def back(x2, tm, og, of, sga, sgb):
        h, xn, gates = _out_proj(x2, og, of, sga, sgb, wug, wuf, wo, nfw, wr, br,
                                 n_experts=n_experts, n_groups=n_groups, tm=tm)
        return _moe(xn, gates, h, wg, wu, wd, fnw, tm=_pick_tile(x2.shape[0], 1024), n_groups=n_groups)

    xp = x_prompt.reshape(batch * seq, d_model)
    xs = x_sample.reshape(n_seq * n_new, d_model)
    tm_p, gin_p, loga_p, qf_p, (kb_p, kt_p, vt_p, vtb_p), sga_p, sgb_p, lft_p = front(xp, seq)
    tm_s, gin_s, loga_s, qf_s, (k_s, v_s), sga_s, sgb_s, lft_s = front(xs)

    chunk = math.gcd(seq, GLA_CHUNK)
    n_chunks = _pick_tile(seq // chunk, 8)
    og_p, s_p = _gla(gin_p, loga_p, jnp.zeros((batch, g_heads, dk, dv), F32), nw_gla, n_outer=batch,
                     n_steps=seq // (chunk * n_chunks), n_seq=1, n_chunks=n_chunks, chunk=chunk,
                     mx_dtype=BF16, out_dtype=BF16)
    assert math.gcd(n_new, GLA_CHUNK) == n_new
    gs = _pick_tile(n_seq, 8)
    og_s, s_s = _gla(gin_s, loga_s, state_gla[0], nw_gla, n_outer=n_seq // gs, n_steps=1, n_seq=gs,
                     n_chunks=1, chunk=n_new, mx_dtype=F32, out_dtype=F32)

    tq = _pick_tile(seq, 512)
    qat, ka = _fox_prep(qf_p, kb_p, lft_p, batch, seq, f_heads, dh, tq)
    ck = jnp.transpose(cache_k[0], (0, 2, 3, 1)).reshape(n_phys, fw, page)
    cv = jnp.transpose(cache_v[0], (0, 2, 3, 1)).reshape(n_phys, fw, page)
    clf = jnp.transpose(cache_logf[0], (0, 2, 1))
    lfnt = jnp.transpose(lft_s.reshape(f_heads, n_seq, n_new), (1, 0, 2))
    lfnt_pad = jnp.pad(lfnt, ((0, 0), (0, 0), (0, LANES - n_new)))
    of_p, of_s = _attention(qat, ka, vtb_p, page_table, qf_s, k_s, v_s, lfnt_pad, ck, cv, clf, batch=batch, seq=seq,
                            tq=tq, n_seq=n_seq, n_new=n_new, n_pages=n_pages, heads=f_heads, dh=dh,
                            ppb=_pick_tile(n_pages, PAGES_PER_STEP))

    y_p = back(xp, tm_p, og_p, of_p, sga_p, sgb_p)
    y_s = back(xs, tm_s, og_s, of_s, sga_s, sgb_s)
    lf_p = lft_p.T
    lf_s = lft_s.T
    k_p = jnp.transpose(kt_p.reshape(batch, f_heads, dh, seq), (0, 3, 1, 2))
    v_p = jnp.transpose(vt_p.reshape(batch, f_heads, dh, seq), (0, 3, 1, 2))

    return (y_p.reshape(batch, seq, d_model),
            y_s.reshape(n_seq, n_new, d_model),
            s_p[None],
            s_s[None],
            k_p.reshape(1, batch, seq, f_heads, dh),
            v_p.reshape(1, batch, seq, f_heads, dh),
            lf_p.reshape(1, batch, seq, f_heads),
            k_s.reshape(1, n_seq, n_new, f_heads, dh),
            v_s.reshape(1, n_seq, n_new, f_heads, dh),
            lf_s.reshape(1, n_seq, n_new, f_heads))
```

```python
import functools
import math

import jax
import jax.numpy as jnp
import numpy as np
from jax import lax
from jax.experimental import pallas as pl
from jax.experimental.pallas import tpu as pltpu

F32 = jnp.float32
BF16 = jnp.bfloat16

RMS_EPS = 1e-6
GLA_TAU = 16.0
GLA_CHUNK = 64
PAGES_PER_STEP = 32
MOE_CHUNK_ROWS = 288
MOE_EXPERTS_PER_STEP = 4
NEG = -1e30
LOG2E = 1.4426950408889634

V7X_VMEM_LIMIT_BYTES = 56 * 1024 * 1024
LANES = 128


def _cparams(sem):
    return pltpu.CompilerParams(dimension_semantics=sem, vmem_limit_bytes=V7X_VMEM_LIMIT_BYTES)


def _sigmoid(x):
    return 1.0 / (1.0 + jnp.exp(-x))


def _log_sigmoid(x):
    return jnp.minimum(x, 0.0) - jnp.log(1.0 + jnp.exp(-jnp.abs(x)))


def _rms(x, w):
    ms = jnp.mean(x * x, axis=-1, keepdims=True)
    return x * lax.rsqrt(ms + RMS_EPS) * w


def _split3(x):
    p1 = x.astype(BF16)
    r1 = x - p1.astype(F32)
    p2 = r1.astype(BF16)
    p3 = (r1 - p2.astype(F32)).astype(BF16)
    return p1, p2, p3


def _dot(a, b):
    return jnp.dot(a, b, preferred_element_type=F32)


def _dot_nt(a, b):
    return lax.dot_general(a, b, (((1,), (1,)), ((), ())), preferred_element_type=F32)


def _in_proj_kernel(x_ref, nw_ref, w_ref, w2_ref, b2_ref, bf_ref, gla_ref, loga_ref, qf_ref, *refs, dims, kv_t):
    n_gla, n_fox, d_model, n_lr, n_fh = dims
    sga_ref, sgb_ref, lft_ref = refs[-3:]
    xn = _rms(x_ref[...], nw_ref[...]).astype(BF16)

    def mm(lo, hi):
        return _dot(xn, w_ref[:, lo:hi])

    o = 0
    gla_ref[...] = mm(o, o + n_gla)
    o += n_gla
    fw = n_fox // 3
    qf_ref[...] = mm(o, o + fw)
    kf = mm(o + fw, o + 2 * fw)
    vf = mm(o + 2 * fw, o + 3 * fw)
    if kv_t:
        kb_ref, kt_ref, vt_ref, vtb_ref = refs[:4]
        kb_ref[...] = kf.astype(BF16)
        kt_ref[...] = kf.T
        v_t = vf.T
        vt_ref[...] = v_t
        v_tb = v_t.astype(BF16)
        for p in range(fw // LANES):
            vtb_ref[p] = v_tb[p * LANES:(p + 1) * LANES, :]
    else:
        kf_ref, vf_ref = refs[:2]
        kf_ref[...] = kf
        vf_ref[...] = vf
    o += n_fox
    sga_ref[...] = _sigmoid(mm(o, o + d_model))
    sgb_ref[...] = _sigmoid(mm(o + d_model, o + 2 * d_model))
    o += 2 * d_model
    small = mm(o, o + LANES)
    pre = _dot(small.astype(BF16), w2_ref[...]) + b2_ref[...]
    loga_ref[...] = _log_sigmoid(pre) * (1.0 / GLA_TAU)
    small_t = small.T
    lft_ref[...] = _log_sigmoid(small_t[n_lr:n_lr + n_fh, :] + bf_ref[...])


def _in_proj(x2, nw, w_all, w2p, b2, bfc, dims, tm, seq=None):
    n_gla, n_fox, d_model, n_lr, n_fh = dims
    t = x2.shape[0]
    fw = n_fox // 3
    n_qk = w2p.shape[1]
    row = lambda i: (i, 0)
    const = lambda i: (0, 0)
    if seq is None:
        kv_shapes = (jax.ShapeDtypeStruct((t, fw), F32),) * 2
        kv_specs = (pl.BlockSpec((tm, fw), row),) * 2
    else:
        nsb = seq // tm
        batch = t // seq
        kv_shapes = (
            jax.ShapeDtypeStruct((t, fw), BF16),
            jax.ShapeDtypeStruct((batch, fw, seq), F32),
            jax.ShapeDtypeStruct((batch, fw, seq), F32),
            jax.ShapeDtypeStruct((batch, fw // LANES, LANES, seq), BF16),
        )
        kv_specs = (
            pl.BlockSpec((tm, fw), row),
            pl.BlockSpec((None, fw, tm), lambda i: (i // nsb, 0, i % nsb)),
            pl.BlockSpec((None, fw, tm), lambda i: (i // nsb, 0, i % nsb)),
            pl.BlockSpec((None, fw // LANES, LANES, tm), lambda i: (i // nsb, 0, 0, i % nsb)),
        )
    out_shape = (
        jax.ShapeDtypeStruct((t, n_gla), F32),
        jax.ShapeDtypeStruct((t, n_qk), F32),
        jax.ShapeDtypeStruct((t, fw), F32),
        *kv_shapes,
        jax.ShapeDtypeStruct((t, d_model), F32),
        jax.ShapeDtypeStruct((t, d_model), F32),
        jax.ShapeDtypeStruct((n_fh, t), F32),
    )
    return pl.pallas_call(
        functools.partial(_in_proj_kernel, dims=dims, kv_t=seq is not None),
        grid=(t // tm,),
        in_specs=[
            pl.BlockSpec((tm, d_model), row),
            pl.BlockSpec((1, d_model), const),
            pl.BlockSpec(w_all.shape, const, pipeline_mode=pl.Buffered(1)),
            pl.BlockSpec(w2p.shape, const),
            pl.BlockSpec((1, n_qk), const),
            pl.BlockSpec((n_fh, 1), const),
        ],
        out_specs=(
            pl.BlockSpec((tm, n_gla), row),
            pl.BlockSpec((tm, n_qk), row),
            pl.BlockSpec((tm, fw), row),
            *kv_specs,
            pl.BlockSpec((tm, d_model), row),
            pl.BlockSpec((tm, d_model), row),
            pl.BlockSpec((n_fh, tm), lambda i: (0, i)),
        ),
        out_shape=out_shape,
        compiler_params=_cparams(("parallel",)),
        name="in_proj",
    )(x2, nw, w_all, w2p, b2, bfc)


def _gla_kernel(gin_ref, loga_ref, s0_ref, nw_ref, og_ref, sout_ref, s_scr, *,
                chunk, n_chunks, n_seq, heads, dk, dv, mx_dtype):
    j = pl.program_id(1)
    qk = heads * dk
    vw = heads * dv
    kpad = max(chunk, LANES)

    @pl.when(j == 0)
    def _():
        s_scr[...] = s0_ref[...]

    rows = n_seq * n_chunks * chunk
    shift = chunk.bit_length() - 1
    assert chunk == 1 << shift
    nw = nw_ref[...]

    rr = lax.broadcasted_iota(jnp.int32, (rows, rows), 0)
    cc = lax.broadcasted_iota(jnp.int32, (rows, rows), 1)
    same_chunk = (rr >> shift) == (cc >> shift)
    tri = jnp.where(same_chunk, jnp.where(cc <= rr, 1.0, 0.0), 0.0).astype(mx_dtype)
    gs = jnp.concatenate([p.astype(mx_dtype) for p in _split3(loga_ref[...])], axis=1)
    bs = _dot(tri, gs)
    b_all = bs[:, :qk] + bs[:, qk:2 * qk] + bs[:, 2 * qk:]
    k_all = gin_ref[:, qk:2 * qk]
    q_i_all = (gin_ref[:, 0:qk] * jnp.exp(b_all) * dk ** -0.5).astype(mx_dtype)
    k_i_all = k_all * jnp.exp(-b_all)
    vb_all = gin_ref[:, 2 * qk:2 * qk + vw].astype(mx_dtype)
    r_all = gin_ref[:, 2 * qk + vw:2 * qk + 2 * vw]
    silu_r = r_all * _sigmoid(r_all)

    if n_seq == 1 and n_chunks > 1 and rows % LANES == 0:
        ones_bd = jnp.where(same_chunk, 1.0, 0.0).astype(mx_dtype)
        bl = _dot(ones_bd, gs)
        b_tot = bl[:, :qk] + bl[:, qk:2 * qk] + bl[:, 2 * qk:]
        k_end_t = (k_all * jnp.exp(b_tot - b_all)).T.astype(mx_dtype)
        decay_t = jnp.exp(b_tot.T)
        k_i_m = k_i_all.astype(mx_dtype)
        keep = same_chunk & (cc <= rr)
        for h in range(heads):
            qh = q_i_all[:, h * dk:(h + 1) * dk]
            vh = vb_all[:, h * dv:(h + 1) * dv]
            a = jnp.where(keep, _dot_nt(qh, k_i_m[:, h * dk:(h + 1) * dk]), 0.0).astype(mx_dtype)
            o_intra = _dot(a, vh)
            state = s_scr[0, h]
            o_inter = []
            for c in range(n_chunks):
                sl = slice(c * chunk, (c + 1) * chunk)
                o_inter.append(_dot(qh[sl], state.astype(mx_dtype)))
                kv = _dot(k_end_t[h * dk:(h + 1) * dk, sl], vh[sl])
                state = decay_t[h * dk:(h + 1) * dk, c * chunk:c * chunk + 1] * state + kv
            s_scr[0, h] = state
            o = o_intra + jnp.concatenate(o_inter, axis=0)
            og_ref[:, h * dv:(h + 1) * dv] = (_rms(o, nw) * silu_r[:, h * dv:(h + 1) * dv]).astype(og_ref.dtype)

        @pl.when(j == pl.num_programs(1) - 1)
        def _():
            sout_ref[...] = s_scr[...]
        return

    rio = lax.broadcasted_iota(jnp.int32, (chunk, kpad), 0)
    cio = lax.broadcasted_iota(jnp.int32, (chunk, kpad), 1)
    tril = cio <= rio

    def pad_rows(a):
        if kpad == chunk:
            return a
        return jnp.concatenate([a, jnp.zeros((kpad - chunk, a.shape[1]), a.dtype)], axis=0)

    for idx in range(n_seq * n_chunks):
        seq = idx // n_chunks
        sl = slice(idx * chunk, (idx + 1) * chunk)
        b = b_all[sl]
        b_last = b[chunk - 1:chunk, :]
        q_i = q_i_all[sl]
        k_i = pad_rows(k_i_all[sl]).astype(mx_dtype)
        k_end_t = pad_rows(k_all[sl] * jnp.exp(b_last - b)).T.astype(mx_dtype)
        decay = jnp.exp(pad_rows(b).T[:, chunk - 1:chunk])
        vb = pad_rows(vb_all[sl])
        for h in range(heads):
            qh = q_i[:, h * dk:(h + 1) * dk]
            kh = k_i[:, h * dk:(h + 1) * dk]
            vh = vb[:, h * dv:(h + 1) * dv]
            a = jnp.where(tril, _dot_nt(qh, kh), 0.0).astype(mx_dtype)
            s_old = s_scr[seq, h]
            o = _dot(a, vh) + _dot(qh, s_old.astype(mx_dtype))
            kv = _dot(k_end_t[h * dk:(h + 1) * dk, :], vh)
            s_scr[seq, h] = decay[h * dk:(h + 1) * dk, :] * s_old + kv
            og_ref[sl, h * dv:(h + 1) * dv] = (_rms(o, nw) * silu_r[sl, h * dv:(h + 1) * dv]).astype(og_ref.dtype)

    @pl.when(j == pl.num_programs(1) - 1)
    def _():
        sout_ref[...] = s_scr[...]


def _gla(gin, loga, s0, nw, *, n_outer, n_steps, n_seq, n_chunks, chunk, mx_dtype, out_dtype):
    t = gin.shape[0]
    _, heads, dk, dv = s0.shape
    rows = n_seq * n_chunks * chunk
    assert t == n_outer * n_steps * rows
    kern = functools.partial(_gla_kernel, chunk=chunk, n_chunks=n_chunks, n_seq=n_seq, heads=heads,
                             dk=dk, dv=dv, mx_dtype=mx_dtype)
    tok = lambda i, j: (i * n_steps + j, 0)
    st = lambda i, j: (i, 0, 0, 0)
    return pl.pallas_call(
        kern,
        grid=(n_outer, n_steps),
        in_specs=[
            pl.BlockSpec((rows, gin.shape[1]), tok),
            pl.BlockSpec((rows, loga.shape[1]), tok),
            pl.BlockSpec((n_seq, heads, dk, dv), st),
            pl.BlockSpec((1, dv), lambda i, j: (0, 0)),
        ],
        out_specs=(
            pl.BlockSpec((rows, heads * dv), tok),
            pl.BlockSpec((n_seq, heads, dk, dv), st),
        ),
        out_shape=(
            jax.ShapeDtypeStruct((t, heads * dv), out_dtype),
            jax.ShapeDtypeStruct(s0.shape, F32),
        ),
        scratch_shapes=[pltpu.VMEM((n_seq, heads, dk, dv), F32)],
        compiler_params=_cparams(("parallel", "arbitrary")),
        name="gla",
    )(gin, loga, s0, nw)


def _fox_prep_kernel(qf_ref, kf_ref, lft_ref, rq_ref, rk_ref, qat_ref, ka_ref, carry_scr, *, heads, dh):
    j = pl.program_id(1)
    tp = qf_ref.shape[0]

    @pl.when(j == 0)
    def _():
        carry_scr[...] = jnp.zeros_like(carry_scr)

    rio = lax.broadcasted_iota(jnp.int32, (tp, tp), 0)
    cio = lax.broadcasted_iota(jnp.int32, (tp, tp), 1)
    upper = jnp.where(rio <= cio, 1.0, 0.0).astype(BF16)
    l1, l2, l3 = _split3(lft_ref[...])
    cs = _dot(jnp.concatenate([l1, l2, l3, jnp.zeros_like(l1)], axis=0), upper)
    c_t = cs[0:heads] + cs[heads:2 * heads] + cs[2 * heads:3 * heads] + carry_scr[:, 0:1]
    carry_scr[...] = jnp.broadcast_to(c_t[:, tp - 1:tp], carry_scr.shape)
    c1, c2, c3 = _split3(c_t * LOG2E)
    pad_rows = LANES - 3 * heads - 8
    aug_t = jnp.concatenate([c1.astype(F32), c2.astype(F32), c3.astype(F32),
                             jnp.ones((8, tp), F32), jnp.zeros((pad_rows, tp), F32)], axis=0)
    aug_b = aug_t.astype(BF16)
    aq_t = _dot(rq_ref[...], aug_b)
    ak_t = _dot(rk_ref[...], aug_b)
    q_t = qf_ref[...].T * (dh ** -0.5 * LOG2E)
    k_t = kf_ref[...].astype(F32).T
    for h in range(heads):
        rows = slice(h * dh, (h + 1) * dh)
        qat_ref[h] = jnp.concatenate([q_t[rows], aq_t[rows]], axis=0).astype(BF16)
        ka_ref[h] = jnp.concatenate([k_t[rows], ak_t[rows]], axis=0).T.astype(BF16)


def _placement_matrices(heads, dh):
    rq = np.zeros((heads * dh, LANES), np.float32)
    rk = np.zeros((heads * dh, LANES), np.float32)
    ones_row = 3 * heads
    for h in range(heads):
        for p in range(3):
            rq[h * dh + p, p * heads + h] = 1.0
            rq[h * dh + 3 + p, ones_row] = 1.0
            rk[h * dh + p, ones_row] = 1.0
            rk[h * dh + 3 + p, p * heads + h] = -1.0
    return jnp.asarray(rq, BF16), jnp.asarray(rk, BF16)


def _fox_prep(qf, kf, lft, batch, seq, heads, dh, tp):
    rq, rk = _placement_matrices(heads, dh)
    n_steps = seq // tp
    tok = lambda b, j: (b * n_steps + j, 0)
    const = lambda b, j: (0, 0)
    return pl.pallas_call(
        functools.partial(_fox_prep_kernel, heads=heads, dh=dh),
        grid=(batch, n_steps),
        in_specs=[
            pl.BlockSpec((tp, heads * dh), tok),
            pl.BlockSpec((tp, heads * dh), tok),
            pl.BlockSpec((heads, tp), lambda b, j: (0, b * n_steps + j)),
            pl.BlockSpec(rq.shape, const),
            pl.BlockSpec(rk.shape, const),
        ],
        out_specs=(
            pl.BlockSpec((None, heads, LANES, tp), lambda b, j: (b, 0, 0, j)),
            pl.BlockSpec((None, heads, tp, LANES), lambda b, j: (b, 0, j, 0)),
        ),
        out_shape=(
            jax.ShapeDtypeStruct((batch, heads, LANES, seq), BF16),
            jax.ShapeDtypeStruct((batch, heads, seq, LANES), BF16),
        ),
        scratch_shapes=[pltpu.VMEM((heads, LANES), F32)],
        compiler_params=_cparams(("parallel", "arbitrary")),
        name="fox_prep",
    )(qf, kf, lft, rq, rk)


def _fox_flash_body(qi, ki, qat_ref, ka_ref, vt_ref, o_ref, m_scr, l_scr, acc_scr, s0_scr, s1_scr, *, dh, hps):
    tq = qat_ref.shape[2]
    tk = ka_ref.shape[1]
    hpl = LANES // dh

    @pl.when(ki == 0)
    def _():
        m_scr[...] = jnp.full_like(m_scr, NEG)
        l_scr[...] = jnp.zeros_like(l_scr)
        acc_scr[...] = jnp.zeros_like(acc_scr)

    def update(masked):
        def scores(h, buf):
            buf[...] = _dot(ka_ref[h], qat_ref[h])

        def soft_pv(h, buf):
            s = buf[...]
            if masked:
                rio = lax.broadcasted_iota(jnp.int32, (tk, tq), 0)
                cio = lax.broadcasted_iota(jnp.int32, (tk, tq), 1)
                s = jnp.where(rio <= cio, s, NEG)
            m_prev = m_scr[h]
            m_new = jnp.maximum(m_prev, jnp.max(s, axis=0, keepdims=True))
            a = jnp.exp2(m_prev - m_new)
            p = jnp.exp2(s - m_new)
            l_scr[h] = a * l_scr[h] + jnp.sum(p, axis=0, keepdims=True)
            v_t = vt_ref[h // hpl, pl.ds(pl.multiple_of((h % hpl) * dh, dh), dh), :]
            acc_scr[h] = a * acc_scr[h] + _dot(v_t, p.astype(BF16))
            m_scr[h] = m_new

        scores(0, s0_scr)

        def pair(i, carry):
            scores(2 * i + 1, s1_scr)
            soft_pv(2 * i, s0_scr)
            scores(jnp.minimum(2 * i + 2, hps - 1), s0_scr)
            soft_pv(2 * i + 1, s1_scr)
            return carry

        lax.fori_loop(0, hps // 2, pair, 0)

    @pl.when(ki < qi)
    def _():
        update(False)

    @pl.when(ki == qi)
    def _():
        update(True)
        o_t = jnp.concatenate([acc_scr[h] / l_scr[h] for h in range(hps)], axis=0)
        o_ref[...] = o_t.T.astype(o_ref.dtype)


def _fox_paged_body(j, n_steps, q_ref, kn_ref, vn_ref, lfnt_ref, k_refs, v_refs, lf_refs, o_ref,
                    qbd_scr, cncol_scr, cns_scr, base_scr, m_scr, l_scr, acc_scr, *, heads, dh, n_new, ppb, page):
    rows = heads * n_new
    width = heads * dh

    rr = lax.broadcasted_iota(jnp.int32, (rows, width), 0)
    cc = lax.broadcasted_iota(jnp.int32, (rows, width), 1)
    bd_mask = (rr // n_new) == (cc // dh)

    @pl.when(j == 0)
    def _():
        q = q_ref[...] * dh ** -0.5
        q_rep = jnp.broadcast_to(q[None], (heads, n_new, width)).reshape(rows, width)
        qbd_scr[...] = jnp.where(bd_mask, q_rep, 0.0).astype(BF16)
        lfnt = lfnt_ref[...]
        lane = lax.broadcasted_iota(jnp.int32, lfnt.shape, 1)
        cnt = jnp.zeros_like(lfnt)
        for s in range(n_new):
            col = jnp.sum(jnp.where(lane <= s, lfnt, 0.0), axis=1, keepdims=True)
            cnt = cnt + jnp.where(lane == s, col, 0.0)
        cns = jnp.broadcast_to(cnt[:, None, :], (heads, n_new, LANES)).reshape(rows, LANES)
        cns_scr[...] = cns
        r2 = lax.broadcasted_iota(jnp.int32, (rows, LANES), 0)
        l2 = lax.broadcasted_iota(jnp.int32, (rows, LANES), 1)
        cncol_scr[...] = jnp.sum(jnp.where(l2 == r2 % n_new, cns, 0.0), axis=1, keepdims=True)
        base_scr[...] = jnp.zeros_like(base_scr)
        m_scr[...] = jnp.full_like(m_scr, NEG)
        l_scr[...] = jnp.zeros_like(l_scr)
        acc_scr[...] = jnp.zeros_like(acc_scr)

    rio = lax.broadcasted_iota(jnp.int32, (page, page), 0)
    cio = lax.broadcasted_iota(jnp.int32, (page, page), 1)
    after = jnp.where(rio > cio, 1.0, 0.0).astype(BF16)
    lf_all = jnp.concatenate([lf_refs[i][...] for i in range(ppb)], axis=0)
    ss = _dot(jnp.concatenate(_split3(lf_all), axis=0), after)
    nr = ppb * heads
    suf_all = ss[0:nr] + ss[nr:2 * nr] + ss[2 * nr:3 * nr]
    tot_all = suf_all[:, 0:1] + lf_all[:, 0:1]
    base = base_scr[:, 0:1]
    biases = []
    for i in range(ppb):
        biases.append(base + suf_all[i * heads:(i + 1) * heads])
        base = base + tot_all[i * heads:(i + 1) * heads]
    base_scr[...] = jnp.broadcast_to(base, base_scr.shape)
    bias = jnp.concatenate(biases, axis=1)
    bias_rows = jnp.broadcast_to(bias[:, None, :], (heads, n_new, ppb * page)).reshape(rows, ppb * page)

    kt = jnp.concatenate([k_refs[i][...].astype(BF16) for i in range(ppb)], axis=1)
    vt = jnp.concatenate([v_refs[i][...].astype(BF16) for i in range(ppb)], axis=1)
    qbd = qbd_scr[...]
    s = _dot(qbd, kt) + cncol_scr[...] + bias_rows
    m_prev = m_scr[...]
    m_new = jnp.maximum(m_prev, jnp.max(s, axis=1, keepdims=True))
    a = jnp.exp(m_prev - m_new)
    p = jnp.exp(s - m_new)
    l_scr[...] = a * l_scr[...] + jnp.sum(p, axis=1, keepdims=True)
    acc_scr[...] = a * acc_scr[...] + _dot_nt(p.astype(BF16), vt)
    m_scr[...] = m_new

    @pl.when(j == n_steps - 1)
    def _():
        pad = jnp.zeros((LANES - n_new, width), F32)
        kn = jnp.concatenate([kn_ref[...], pad], axis=0).astype(BF16)
        vn = jnp.concatenate([vn_ref[...], pad], axis=0).astype(BF16)
        r2 = lax.broadcasted_iota(jnp.int32, (rows, LANES), 0)
        l2 = lax.broadcasted_iota(jnp.int32, (rows, LANES), 1)
        sn = _dot_nt(qbd, kn) + cncol_scr[...] - cns_scr[...]
        sn = jnp.where(l2 <= r2 % n_new, sn, NEG)
        m_p = m_scr[...]
        m_n = jnp.maximum(m_p, jnp.max(sn, axis=1, keepdims=True))
        a2 = jnp.exp(m_p - m_n)
        pn = jnp.exp(sn - m_n)
        l_f = a2 * l_scr[...] + jnp.sum(pn, axis=1, keepdims=True)
        acc_f = a2 * acc_scr[...] + _dot(pn.astype(BF16), vn)
        o = jnp.where(bd_mask, acc_f / l_f, 0.0)
        o_ref[...] = jnp.sum(o.reshape(heads, n_new, width), axis=0)


def _attn_kernel(qi_ref, ki_ref, fb_ref, sq_ref, pg_ref, qat_ref, ka_ref, vt_ref, q_ref, kn_ref, vn_ref, lfnt_ref,
                 ck_hbm, cv_hbm, clf_hbm, of_ref, op_ref, *scr, n_flash, n_paged, n_pg_steps, dh, hps, heads, n_new,
                 ppb, page):
    flash_scr = scr[0:5]
    paged_scr = scr[5:12]
    kbuf, vbuf, lfbuf, ksem, vsem, lfsem = scr[12:]
    i = pl.program_id(0)
    slot = i % 2

    def page_copies(step, slt):
        out = []
        for k in range(ppb):
            pid = pg_ref[step * ppb + k]
            out.append(pltpu.make_async_copy(ck_hbm.at[pid], kbuf.at[slt, k], ksem.at[slt]))
            out.append(pltpu.make_async_copy(cv_hbm.at[pid], vbuf.at[slt, k], vsem.at[slt]))
            out.append(pltpu.make_async_copy(clf_hbm.at[pid], lfbuf.at[slt, k], lfsem.at[slt]))
        return out

    @pl.when(i == 0)
    def _():
        for c in page_copies(0, 0):
            c.start()

    @pl.when(i + 1 < n_paged)
    def _():
        for c in page_copies(i + 1, 1 - slot):
            c.start()

    @pl.when(i < n_flash)
    def _():
        _fox_flash_body(qi_ref[i], ki_ref[i], qat_ref, ka_ref, vt_ref, of_ref, *flash_scr, dh=dh, hps=hps)

    @pl.when(i < n_paged)
    def _():
        for c in page_copies(i, slot):
            c.wait()
        k_refs = [kbuf.at[slot, k] for k in range(ppb)]
        v_refs = [vbuf.at[slot, k] for k in range(ppb)]
        lf_refs = [lfbuf.at[slot, k] for k in range(ppb)]
        _fox_paged_body(i % n_pg_steps, n_pg_steps, q_ref, kn_ref, vn_ref, lfnt_ref, k_refs, v_refs, lf_refs, op_ref,
                        *paged_scr, heads=heads, dh=dh, n_new=n_new, ppb=ppb, page=page)


def _attention(qat, ka, vt, page_table, qf, kf, vf, lfnt_pad, ck, cv, clf, *, batch, seq, tq, n_seq, n_new, n_pages,
               heads, dh, ppb):
    page = ck.shape[-1]
    width = heads * dh
    rows = heads * n_new
    hpl = LANES // dh
    nq = seq // tq
    n_pg_steps = n_pages // ppb
    n_paged = n_seq * n_pg_steps
    tri = [(b, q, k) for b in range(batch) for q in range(nq) for k in range(q + 1)]
    n_flash = len(tri)
    n_grid = max(n_flash, n_paged)
    tri = tri + [tri[-1]] * (n_grid - n_flash)
    fb_a = jnp.asarray(np.array([t[0] for t in tri], np.int32))
    qi_a = jnp.asarray(np.array([t[1] for t in tri], np.int32))
    ki_a = jnp.asarray(np.array([t[2] for t in tri], np.int32))
    pstep = np.minimum(np.arange(n_grid), n_paged - 1)
    sq_np = (pstep // n_pg_steps).astype(np.int32)
    col_np = n_pages - 1 - ((pstep % n_pg_steps)[:, None] * ppb + np.arange(ppb)[None, :])
    sq_a = jnp.asarray(sq_np)
    pg_a = page_table.astype(jnp.int32)[sq_np[:, None], col_np].reshape(-1)

    tok_map = lambda i, qi, ki, fb, sq, pg: (sq[i], 0)
    tok = pl.BlockSpec((n_new, width), tok_map)
    in_specs = [
        pl.BlockSpec((None, heads, LANES, tq), lambda i, qi, ki, fb, sq, pg: (fb[i], 0, 0, qi[i])),
        pl.BlockSpec((None, heads, tq, LANES), lambda i, qi, ki, fb, sq, pg: (fb[i], 0, ki[i], 0)),
        pl.BlockSpec((None, heads // hpl, LANES, tq), lambda i, qi, ki, fb, sq, pg: (fb[i], 0, 0, ki[i])),
        tok, tok, tok,
        pl.BlockSpec((None, heads, LANES), lambda i, qi, ki, fb, sq, pg: (sq[i], 0, 0)),
    ]
    in_specs += [pl.BlockSpec(memory_space=pl.ANY)] * 3
    grid_spec = pltpu.PrefetchScalarGridSpec(
        num_scalar_prefetch=5,
        grid=(n_grid,),
        in_specs=in_specs,
        out_specs=(pl.BlockSpec((tq, width), lambda i, qi, ki, fb, sq, pg: (fb[i] * nq + qi[i], 0)),
                   pl.BlockSpec((n_new, width), tok_map)),
        scratch_shapes=[
            pltpu.VMEM((heads, 1, tq), F32),
            pltpu.VMEM((heads, 1, tq), F32),
            pltpu.VMEM((heads, dh, tq), F32),
            pltpu.VMEM((tq, tq), F32),
            pltpu.VMEM((tq, tq), F32),
            pltpu.VMEM((rows, width), BF16),
            pltpu.VMEM((rows, 1), F32),
            pltpu.VMEM((rows, LANES), F32),
            pltpu.VMEM((heads, LANES), F32),
            pltpu.VMEM((rows, 1), F32),
            pltpu.VMEM((rows, 1), F32),
            pltpu.VMEM((rows, width), F32),
            pltpu.VMEM((2, ppb, width, page), F32),
            pltpu.VMEM((2, ppb, width, page), F32),
            pltpu.VMEM((2, ppb, heads, page), F32),
            pltpu.SemaphoreType.DMA((2,)),
            pltpu.SemaphoreType.DMA((2,)),
            pltpu.SemaphoreType.DMA((2,)),
        ],
    )
    kern = functools.partial(_attn_kernel, n_flash=n_flash, n_paged=n_paged, n_pg_steps=n_pg_steps, dh=dh,
                             hps=heads, heads=heads, n_new=n_new, ppb=ppb, page=page)
    return pl.pallas_call(
        kern,
        grid_spec=grid_spec,
        out_shape=(jax.ShapeDtypeStruct((batch * seq, width), BF16),
                   jax.ShapeDtypeStruct((n_seq * n_new, width), F32)),
        compiler_params=_cparams(("arbitrary",)),
        name="attention",
    )(qi_a, ki_a, fb_a, sq_a, pg_a, qat, ka, vt, qf, kf, vf, lfnt_pad, ck, cv, clf)


def _out_proj_kernel(x_ref, og_ref, of_ref, sga_ref, sgb_ref, wug_ref, wuf_ref, wo_ref, nfw_ref, wr_ref, br_ref,
                     h_ref, xn_ref, gates_ref, *, n_experts, n_groups):
    up_a = _dot(og_ref[...].astype(BF16), wug_ref[...])
    up_b = _dot(of_ref[...].astype(BF16), wuf_ref[...])
    merged = sga_ref[...] * up_a + sgb_ref[...] * up_b
    h = x_ref[...] + _dot(merged.astype(BF16), wo_ref[...])
    h_ref[...] = h
    xn = _rms(h, nfw_ref[...])
    xn_ref[...] = xn.astype(BF16)

    x_hi = xn.astype(BF16)
    x_lo = (xn - x_hi.astype(F32)).astype(BF16)
    l_hi = _dot(x_hi, wr_ref[...])
    logits = l_hi[:, :LANES] + l_hi[:, LANES:] + _dot(x_lo, wr_ref[:, :LANES]) + br_ref[...]
    lane = lax.broadcasted_iota(jnp.int32, logits.shape, 1).astype(F32)
    epg = n_experts // n_groups
    big = 4.0 * LANES
    is_g = (lane >= n_experts) & (lane < n_experts + n_groups)
    gl = jnp.where(is_g, logits, NEG)
    gmax = jnp.max(gl, axis=1, keepdims=True)
    gsum = jnp.sum(jnp.exp(gl - gmax), axis=1, keepdims=True)
    p_g = 1.0 / gsum
    g_sel = jnp.min(jnp.where(gl == gmax, lane, big), axis=1, keepdims=True) - n_experts
    e_lo = g_sel * epg
    in_grp = (lane >= e_lo) & (lane < e_lo + epg)
    el = jnp.where(in_grp, logits, NEG)
    emax = jnp.max(el, axis=1, keepdims=True)
    ee = jnp.exp(el - emax)
    pe = ee / jnp.sum(ee, axis=1, keepdims=True)
    cand = jnp.where(in_grp, pe, -1.0)
    v1 = jnp.max(cand, axis=1, keepdims=True)
    i1 = jnp.min(jnp.where(cand == v1, lane, big), axis=1, keepdims=True)
    cand2 = jnp.where(lane == i1, -1.0, cand)
    v2 = jnp.max(cand2, axis=1, keepdims=True)
    i2 = jnp.min(jnp.where(cand2 == v2, lane, big), axis=1, keepdims=True)
    tot = v1 + v2
    gates_ref[...] = (jnp.where(lane == i1, p_g * (v1 / tot), 0.0) + jnp.where(lane == i2, p_g * (v2 / tot), 0.0)
                      + jnp.where(lane == n_experts, g_sel, 0.0))


def _out_proj(x2, og, of, sga, sgb, wug, wuf, wo, nfw, wr, br, *, n_experts, n_groups, tm):
    t, d_model = x2.shape
    row = lambda i: (i, 0)
    const = lambda i: (0, 0)
    return pl.pallas_call(
        functools.partial(_out_proj_kernel, n_experts=n_experts, n_groups=n_groups),
        grid=(t // tm,),
        in_specs=[
            pl.BlockSpec((tm, d_model), row),
            pl.BlockSpec((tm, og.shape[1]), row),
            pl.BlockSpec((tm, of.shape[1]), row),
            pl.BlockSpec((tm, d_model), row),
            pl.BlockSpec((tm, d_model), row),
            pl.BlockSpec(wug.shape, const, pipeline_mode=pl.Buffered(1)),
            pl.BlockSpec(wuf.shape, const, pipeline_mode=pl.Buffered(1)),
            pl.BlockSpec(wo.shape, const, pipeline_mode=pl.Buffered(1)),
            pl.BlockSpec((1, d_model), const),
            pl.BlockSpec(wr.shape, const),
            pl.BlockSpec((1, LANES), const),
        ],
        out_specs=(
            pl.BlockSpec((tm, d_model), row),
            pl.BlockSpec((tm, d_model), row),
            pl.BlockSpec((tm, LANES), row),
        ),
        out_shape=(
            jax.ShapeDtypeStruct((t, d_model), F32),
            jax.ShapeDtypeStruct((t, d_model), BF16),
            jax.ShapeDtypeStruct((t, LANES), F32),
        ),
        compiler_params=_cparams(("parallel",)),
        name="out_proj",
    )(x2, og, of, sga, sgb, wug, wuf, wo, nfw, wr, br)


def _moe_kernel(xn_ref, gates_ref, h_ref, wg_ref, wu_ref, wd_ref, fnw_ref, y_ref,
                xs_scr, ys_scr, gs_scr, selt_scr, off_smem, *, n_experts, n_groups, chunk, eps):
    e = pl.program_id(1)
    tm = xn_ref.shape[0]
    rows = selt_scr.shape[1]
    rows_p = xs_scr.shape[0]
    epg = n_experts // n_groups

    @pl.when(e == 0)
    def _():
        gates = gates_ref[...]
        lane = lax.broadcasted_iota(jnp.int32, (tm, LANES), 1).astype(F32)
        g_sel = gates[:, n_experts:n_experts + 1]
        in_g = lane == g_sel
        onehot = jnp.where(in_g, 1.0, 0.0).astype(BF16)
        rio = lax.broadcasted_iota(jnp.int32, (tm, tm), 0)
        cio = lax.broadcasted_iota(jnp.int32, (tm, tm), 1)
        tril = jnp.where(cio <= rio, 1.0, 0.0).astype(BF16)
        cum = _dot(tril, onehot)
        cnt = cum[tm - 1:tm, :].astype(jnp.int32)
        lane1 = lax.broadcasted_iota(jnp.int32, (1, LANES), 1)
        off = jnp.int32(0)
        off_vec = jnp.zeros((1, LANES), F32)
        for g in range(n_groups):
            off_smem[g] = off
            off_vec = off_vec + jnp.where(lane1 == g, off.astype(F32), 0.0)
            off = off + ((cnt[0, g] + 7) // 8) * 8
        off_smem[n_groups] = off
        pos = jnp.sum(jnp.where(in_g, off_vec + cum - 1.0, 0.0), axis=1, keepdims=True)
        r_lane = lax.broadcasted_iota(jnp.int32, (tm, rows), 1).astype(F32)
        selt_scr[...] = jnp.where(r_lane == pos, 1.0, 0.0).astype(BF16)
        pos_t = jnp.broadcast_to(pos, (tm, LANES)).T[0:1, :]
        r_sub = lax.broadcasted_iota(jnp.int32, (rows, tm), 0).astype(F32)
        sel = jnp.where(r_sub == pos_t, 1.0, 0.0).astype(BF16)
        xs_scr[0:rows, :] = _dot(sel, xn_ref[...])
        g1, g2, g3 = _split3(gates)
        gs = _dot(sel, jnp.concatenate([g1, g2, g3], axis=1))
        gs_scr[0:rows, :] = gs[:, :LANES] + gs[:, LANES:2 * LANES] + gs[:, 2 * LANES:]
        xs_scr[rows:rows_p, :] = jnp.zeros((rows_p - rows, xs_scr.shape[1]), F32)
        gs_scr[rows:rows_p, :] = jnp.zeros((rows_p - rows, LANES), F32)
        ys_scr[...] = jnp.zeros_like(ys_scr)

    g = (e * eps) // epg
    start = off_smem[g]
    n_rows = off_smem[g + 1] - start
    lane_c = lax.broadcasted_iota(jnp.int32, (chunk, LANES), 1)

    def body(c, carry):
        r0 = pl.multiple_of(start + c * chunk, 8)
        x = xs_scr[pl.ds(r0, chunk), :].astype(BF16)
        gs = gs_scr[pl.ds(r0, chunk), :]
        acc = None
        for i in range(eps):
            gg = _dot(x, wg_ref[i])
            uu = _dot(x, wu_ref[i])
            hid = (gg * _sigmoid(gg)) * uu
            out = _dot(hid.astype(BF16), wd_ref[i])
            gate = jnp.sum(jnp.where(lane_c == e * eps + i, gs, 0.0), axis=1, keepdims=True)
            acc = gate * out if acc is None else acc + gate * out
        ys_scr[pl.ds(r0, chunk), :] += acc
        return carry

    lax.fori_loop(0, (n_rows + chunk - 1) // chunk, body, 0)

    @pl.when(e == pl.num_programs(1) - 1)
    def _():
        ys = ys_scr[0:rows, :]
        hi = ys.astype(BF16)
        lo = (ys - hi.astype(F32)).astype(BF16)
        selt = selt_scr[...]
        moe = _dot(selt, hi) + _dot(selt, lo)
        y_ref[...] = _rms(h_ref[...] + moe, fnw_ref[...])


def _moe(xn, gates, h, wg, wu, wd, fnw, *, tm, n_groups, chunk=MOE_CHUNK_ROWS, eps=MOE_EXPERTS_PER_STEP):
    t, d_model = h.shape
    n_experts, _, d_exp = wg.shape
    rows = tm + LANES
    rows_p = rows + chunk
    row = lambda i, e: (i, 0)
    once = pl.Buffered(1)
    assert (n_experts // n_groups) % eps == 0
    kern = functools.partial(_moe_kernel, n_experts=n_experts, n_groups=n_groups, chunk=chunk, eps=eps)
    return pl.pallas_call(
        kern,
        grid=(t // tm, n_experts // eps),
        in_specs=[
            pl.BlockSpec((tm, d_model), row, pipeline_mode=once),
            pl.BlockSpec((tm, LANES), row, pipeline_mode=once),
            pl.BlockSpec((tm, d_model), row, pipeline_mode=once),
            pl.BlockSpec((eps, d_model, d_exp), lambda i, e: (e, 0, 0)),
            pl.BlockSpec((eps, d_model, d_exp), lambda i, e: (e, 0, 0)),
            pl.BlockSpec((eps, d_exp, d_model), lambda i, e: (e, 0, 0)),
            pl.BlockSpec((1, d_model), lambda i, e: (0, 0)),
        ],
        out_specs=pl.BlockSpec((tm, d_model), row),
        out_shape=jax.ShapeDtypeStruct((t, d_model), F32),
        scratch_shapes=[
            pltpu.VMEM((rows_p, d_model), F32),
            pltpu.VMEM((rows_p, d_model), F32),
            pltpu.VMEM((rows_p, LANES), F32),
            pltpu.VMEM((tm, rows), BF16),
            pltpu.SMEM((n_groups + 1,), jnp.int32),
        ],
        compiler_params=_cparams(("parallel", "arbitrary")),
        name="moe",
    )(xn, gates, h, wg, wu, wd, fnw)


def _pick_tile(n, pref):
    t = min(n, pref)
    while n % t:
        t //= 2
    return t


def kernel(x_prompt, x_sample, state_gla, cache_k, cache_v, cache_logf, page_table, norm_mix_w, w_in, w_gla_a2,
           b_gla_a2, gla_norm_w, b_fox_f, w_up_gla, w_up_fox, w_out, norm_ffn_w, w_router_group, b_router_group,
           w_router_expert, b_router_expert, w_exp_gate, w_exp_up, w_exp_down, final_norm_w):
    depth = w_in.shape[0]
    assert depth == 1
    batch, seq, d_model = x_prompt.shape
    n_seq, n_new, _ = x_sample.shape
    _, _, g_heads, dk, dv = state_gla.shape
    _, n_phys, page, f_heads, dh = cache_k.shape
    n_pages = page_table.shape[1]
    n_lr = w_gla_a2.shape[1]
    n_groups = w_router_group.shape[2]
    n_experts = w_router_expert.shape[2]
    qk = g_heads * dk
    vw = g_heads * dv
    fw = f_heads * dh
    n_gla = 2 * qk + 2 * vw
    n_fox = 3 * fw
    dims = (n_gla, n_fox, d_model, n_lr, f_heads)

    wi = w_in[0]
    o_za = n_gla
    o_fox = o_za + n_lr
    o_fp = o_fox + n_fox
    o_gate = o_fp + f_heads
    w_small = jnp.concatenate([wi[:, o_za:o_fox], wi[:, o_fp:o_gate],
                               jnp.zeros((d_model, LANES - n_lr - f_heads), wi.dtype)], axis=1)
    w_all = jnp.concatenate([wi[:, :o_za], wi[:, o_fox:o_fp], wi[:, o_gate:], w_small], axis=1).astype(BF16)
    w2p = jnp.concatenate([w_gla_a2[0], jnp.zeros((LANES - n_lr, qk), F32)], axis=0).astype(BF16)
    b2 = b_gla_a2[0].reshape(1, qk)
    bfc = b_fox_f[0].reshape(f_heads, 1)
    nw_mix = norm_mix_w[0].reshape(1, d_model)
    nw_gla = gla_norm_w[0].reshape(1, dv)
    wug = w_up_gla[0].astype(BF16)
    wuf = w_up_fox[0].astype(BF16)
    wo = w_out[0].astype(BF16)
    nfw = norm_ffn_w[0].reshape(1, d_model)
    wr32 = jnp.concatenate([w_router_expert[0], w_router_group[0],
                            jnp.zeros((d_model, LANES - n_experts - n_groups), F32)], axis=1)
    wr_hi = wr32.astype(BF16)
    wr = jnp.concatenate([wr_hi, (wr32 - wr_hi.astype(F32)).astype(BF16)], axis=1)
    br = jnp.concatenate([b_router_expert[0], b_router_group[0],
                          jnp.zeros((LANES - n_experts - n_groups,), F32)]).reshape(1, LANES)
    wg = w_exp_gate[0].astype(BF16)
    wu = w_exp_up[0].astype(BF16)
    wd = w_exp_down[0].astype(BF16)
    fnw = final_norm_w.reshape(1, d_model)

    def front(x2, seq_len=None):
        tm = _pick_tile(x2.shape[0] if seq_len is None else seq_len, 512)
        gin, loga, qf, *kv, sga, sgb, lft = _in_proj(x2, nw_mix, w_all, w2p, b2, bfc, dims, tm, seq=seq_len)
        return tm, gin, loga, qf, kv, sga, sgb, lft

    def back(x2, tm, og, of, sga, sgb):
        h, xn, gates = _out_proj(x2, og, of, sga, sgb, wug, wuf, wo, nfw, wr, br,
                                 n_experts=n_experts, n_groups=n_groups, tm=tm)
        return _moe(xn, gates, h, wg, wu, wd, fnw, tm=_pick_tile(x2.shape[0], 1024), n_groups=n_groups)

    xp = x_prompt.reshape(batch * seq, d_model)
    xs = x_sample.reshape(n_seq * n_new, d_model)
    tm_p, gin_p, loga_p, qf_p, (kb_p, kt_p, vt_p, vtb_p), sga_p, sgb_p, lft_p = front(xp, seq)
    tm_s, gin_s, loga_s, qf_s, (k_s, v_s), sga_s, sgb_s, lft_s = front(xs)

    chunk = math.gcd(seq, GLA_CHUNK)
    n_chunks = _pick_tile(seq // chunk, 8)
    og_p, s_p = _gla(gin_p, loga_p, jnp.zeros((batch, g_heads, dk, dv), F32), nw_gla, n_outer=batch,
                     n_steps=seq // (chunk * n_chunks), n_seq=1, n_chunks=n_chunks, chunk=chunk,
                     mx_dtype=BF16, out_dtype=BF16)
    assert math.gcd(n_new, GLA_CHUNK) == n_new
    gs = _pick_tile(n_seq, 8)
    og_s, s_s = _gla(gin_s, loga_s, state_gla[0], nw_gla, n_outer=n_seq // gs, n_steps=1, n_seq=gs,
                     n_chunks=1, chunk=n_new, mx_dtype=F32, out_dtype=F32)

    tq = _pick_tile(seq, 512)
    qat, ka = _fox_prep(qf_p, kb_p, lft_p, batch, seq, f_heads, dh, tq)
    ck = jnp.transpose(cache_k[0], (0, 2, 3, 1)).reshape(n_phys, fw, page)
    cv = jnp.transpose(cache_v[0], (0, 2, 3, 1)).reshape(n_phys, fw, page)
    clf = jnp.transpose(cache_logf[0], (0, 2, 1))
    lfnt = jnp.transpose(lft_s.reshape(f_heads, n_seq, n_new), (1, 0, 2))
    lfnt_pad = jnp.pad(lfnt, ((0, 0), (0, 0), (0, LANES - n_new)))
    of_p, of_s = _attention(qat, ka, vtb_p, page_table, qf_s, k_s, v_s, lfnt_pad, ck, cv, clf, batch=batch, seq=seq,
                            tq=tq, n_seq=n_seq, n_new=n_new, n_pages=n_pages, heads=f_heads, dh=dh,
                            ppb=_pick_tile(n_pages, PAGES_PER_STEP))

    y_p = back(xp, tm_p, og_p, of_p, sga_p, sgb_p)
    y_s = back(xs, tm_s, og_s, of_s, sga_s, sgb_s)
    lf_p = lft_p.T
    lf_s = lft_s.T
    k_p = jnp.transpose(kt_p.reshape(batch, f_heads, dh, seq), (0, 3, 1, 2))
    v_p = jnp.transpose(vt_p.reshape(batch, f_heads, dh, seq), (0, 3, 1, 2))

    return (y_p.reshape(batch, seq, d_model),
            y_s.reshape(n_seq, n_new, d_model),
            s_p[None],
            s_s[None],
            k_p.reshape(1, batch, seq, f_heads, dh),
            v_p.reshape(1, batch, seq, f_heads, dh),
            lf_p.reshape(1, batch, seq, f_heads),
            k_s.reshape(1, n_seq, n_new, f_heads, dh),
            v_s.reshape(1, n_seq, n_new, f_heads, dh),
            lf_s.reshape(1, n_seq, n_new, f_heads))
```
